```python
import math
import jax, jax.numpy as jnp
from jax import lax
import numpy as np

D_MODEL = 4096
BATCH = 1
SEQ = 8192
DEPTH = 1
DEC_BATCH = 128
DEC_SEQ = 4
PAST_LEN = 8192
PAGE_SIZE = 128

LRU_WIDTH = D_MODEL
LRU_BLOCKS = 16
LRU_BLOCK = LRU_WIDTH // LRU_BLOCKS
CONV_WIDTH = 4
LRU_C = 8.0
HEAD_DIM = 64
N_HEADS = D_MODEL // 128
N_KV_HEADS = N_HEADS // 4
GROUP = N_HEADS // N_KV_HEADS
ATTN_WIDTH = N_HEADS * HEAD_DIM
KV_WIDTH = N_KV_HEADS * HEAD_DIM
WINDOW = 128
IN_SPLITS = (LRU_WIDTH, LRU_WIDTH, ATTN_WIDTH, KV_WIDTH, KV_WIDTH, ATTN_WIDTH, D_MODEL, D_MODEL)
IN_WIDTH = 2 * LRU_WIDTH + 2 * ATTN_WIDTH + 2 * KV_WIDTH + 2 * D_MODEL
RMS_EPS = 1e-6
NEG_INF = -1e30

kernel_name = "hawk_swa_sink_parallel_gated_decoder_step"


def _rmsnorm(x, g):
    xf = x.astype(jnp.float32)
    y = xf * lax.rsqrt(jnp.mean(xf * xf, axis=-1, keepdims=True) + RMS_EPS)
    return (y * g.astype(jnp.float32)).astype(x.dtype)


def _causal_conv(xp, w, b):
    t = xp.shape[1] - (CONV_WIDTH - 1)
    out = b
    for tap in range(CONV_WIDTH):
        out = out + xp[:, tap:tap + t] * w[tap]
    return out


def _rg_lru(x, h0, w_a, b_a, w_x, b_x, lam):
    n, t, _ = x.shape
    xb = x.reshape(n, t, LRU_BLOCKS, LRU_BLOCK)
    gate_a = jax.nn.sigmoid(jnp.einsum('ntbi,bij->ntbj', xb, w_a.astype(jnp.float32)).reshape(n, t, LRU_WIDTH) + b_a.astype(jnp.float32))
    gate_x = jax.nn.sigmoid(jnp.einsum('ntbi,bij->ntbj', xb, w_x.astype(jnp.float32)).reshape(n, t, LRU_WIDTH) + b_x.astype(jnp.float32))
    log_a = -LRU_C * gate_a * jax.nn.softplus(-lam.astype(jnp.float32))
    a = jnp.exp(log_a)
    b = jnp.sqrt(-jnp.expm1(2.0 * log_a)) * (gate_x * x)
    b = b.at[:, 0].add(a[:, 0] * h0)

    def combine(left, right):
        a_l, b_l = left
        a_r, b_r = right
        return a_l * a_r, a_r * b_l + b_r

    _, h = lax.associative_scan(combine, (a, b), axis=1)
    return h, h[:, -1]


def _sink_attention(q, k, v, q_pos, k_pos, sinks):
    s = jnp.einsum('...qkgd,...skd->...kgqs', q.astype(jnp.float32), k.astype(jnp.float32)) * (HEAD_DIM ** -0.5)
    rel = q_pos[..., :, None] - k_pos[..., None, :]
    mask = (rel >= 0) & (rel <= WINDOW) & (k_pos[..., None, :] >= 0)
    s = jnp.where(mask[..., None, None, :, :], s, NEG_INF)
    sink = sinks.astype(jnp.float32).reshape(N_KV_HEADS, GROUP, 1, 1)
    m = jnp.maximum(jnp.max(s, axis=-1, keepdims=True), sink)
    p = jnp.exp(s - m)
    denom = jnp.sum(p, axis=-1, keepdims=True) + jnp.exp(sink - m)
    out = jnp.einsum('...kgqs,...skd->...qkgd', p / denom, v.astype(jnp.float32))
    return out.astype(v.dtype)


def _attn_prompt(q, k, v, sinks):
    n, t = q.shape[0], q.shape[1]
    nb = t // WINDOW
    qb = q.reshape(n, nb, WINDOW, N_KV_HEADS, GROUP, HEAD_DIM)
    kb = k.reshape(n, nb, WINDOW, N_KV_HEADS, HEAD_DIM)
    vb = v.reshape(n, nb, WINDOW, N_KV_HEADS, HEAD_DIM)
    kk = jnp.concatenate([jnp.concatenate([jnp.zeros_like(kb[:, :1]), kb[:, :-1]], axis=1), kb], axis=2)
    vv = jnp.concatenate([jnp.concatenate([jnp.zeros_like(vb[:, :1]), vb[:, :-1]], axis=1), vb], axis=2)
    pos = jnp.arange(t, dtype=jnp.int32).reshape(nb, WINDOW)
    k_pos = jnp.concatenate([pos - WINDOW, pos], axis=-1)
    o = _sink_attention(qb, kk, vv, pos, k_pos, sinks)
    return o.reshape(n, t, ATTN_WIDTH)


def _attn_sample(q, kk, vv, sinks, wb):
    n, t = q.shape[0], q.shape[1]
    qg = q.reshape(n, t, N_KV_HEADS, GROUP, HEAD_DIM)
    q_pos = PAST_LEN + jnp.arange(t, dtype=jnp.int32)
    k_pos = jnp.concatenate([PAST_LEN - wb + jnp.arange(wb, dtype=jnp.int32), q_pos])
    o = _sink_attention(qg, kk, vv, q_pos, k_pos, sinks)
    return o.reshape(n, t, ATTN_WIDTH)


def _mixer(xn, conv_buf, h0, k_buf, v_buf, w_in, conv_w, conv_b, lru_w_a, lru_b_a, lru_w_x, lru_b_x,
           lru_lambda, attn_sinks, w_branch, w_out, is_prompt):
    n, t, _ = xn.shape
    z = xn @ w_in
    offs = []
    acc = 0
    for wdt in IN_SPLITS[:-1]:
        acc += wdt
        offs.append(acc)
    x_lru, g_lru, q, k, v, g_attn, m_a, m_b = jnp.split(z, offs, axis=-1)
    xp = jnp.concatenate([conv_buf.astype(x_lru.dtype), x_lru], axis=1)
    new_conv = xp[:, -(CONV_WIDTH - 1):]
    xc = _causal_conv(xp, conv_w, conv_b)
    h, h_last = _rg_lru(xc.astype(jnp.float32), h0.astype(jnp.float32), lru_w_a, lru_b_a, lru_w_x, lru_b_x, lru_lambda)
    o_a = h.astype(xn.dtype) * jax.nn.silu(g_lru)
    q = q.reshape(n, t, N_HEADS, HEAD_DIM)
    k = k.reshape(n, t, N_KV_HEADS, HEAD_DIM)
    v = v.reshape(n, t, N_KV_HEADS, HEAD_DIM)
    if is_prompt:
        wb = min(WINDOW, t)
        o = _attn_prompt(q, k, v, attn_sinks)
        new_k, new_v = k[:, -wb:], v[:, -wb:]
    else:
        wb = k_buf.shape[1]
        kk = jnp.concatenate([k_buf.astype(k.dtype), k], axis=1)
        vv = jnp.concatenate([v_buf.astype(v.dtype), v], axis=1)
        o = _attn_sample(q, kk, vv, attn_sinks, wb)
        new_k, new_v = kk[:, -wb:], vv[:, -wb:]
    o_b = o * jax.nn.silu(g_attn)
    merged = (jax.nn.sigmoid(m_a) * (o_a @ w_branch[:LRU_WIDTH])
              + jax.nn.sigmoid(m_b) * (o_b @ w_branch[LRU_WIDTH:]))
    return merged @ w_out, new_conv, h_last, new_k, new_v


def setup_inputs(seed: int = 0) -> dict:
    key = jax.random.key(seed)
    ks = jax.random.split(key, 20)
    wb = min(WINDOW, PAST_LEN)
    f32 = jnp.float32
    u = jax.random.uniform(ks[12], (DEPTH, LRU_WIDTH), f32, 0.9, 0.999)
    s = u ** (1.0 / LRU_C)
    lam = jnp.log(s) - jnp.log1p(-s)
    return {
        "x_prompt": jax.random.normal(ks[0], (BATCH, SEQ, D_MODEL), f32),
        "x_sample": jax.random.normal(ks[1], (DEC_BATCH, DEC_SEQ, D_MODEL), f32),
        "state_conv": jax.random.normal(ks[2], (DEPTH, DEC_BATCH, CONV_WIDTH - 1, LRU_WIDTH), f32),
        "state_h": 0.5 * jax.random.normal(ks[3], (DEPTH, DEC_BATCH, LRU_WIDTH), f32),
        "cache_k_win": jax.random.normal(ks[4], (DEPTH, DEC_BATCH, wb, N_KV_HEADS, HEAD_DIM), f32),
        "cache_v_win": jax.random.normal(ks[5], (DEPTH, DEC_BATCH, wb, N_KV_HEADS, HEAD_DIM), f32),
        "norm_pre": 1.0 + 0.05 * jax.random.normal(ks[6], (DEPTH, D_MODEL), f32),
        "norm_post": 1.0 + 0.05 * jax.random.normal(ks[7], (DEPTH, D_MODEL), f32),
        "w_in": jax.random.normal(ks[8], (DEPTH, D_MODEL, IN_WIDTH), f32) * D_MODEL ** -0.5,
        "conv_w": jax.random.normal(ks[9], (DEPTH, CONV_WIDTH, LRU_WIDTH), f32) * CONV_WIDTH ** -0.5,
        "conv_b": 0.02 * jax.random.normal(ks[10], (DEPTH, LRU_WIDTH), f32),
        "lru_w_a": jax.random.normal(ks[11], (DEPTH, LRU_BLOCKS, LRU_BLOCK, LRU_BLOCK), f32) * LRU_BLOCK ** -0.5,
        "lru_b_a": 0.02 * jax.random.normal(ks[13], (DEPTH, LRU_WIDTH), f32),
        "lru_w_x": jax.random.normal(ks[14], (DEPTH, LRU_BLOCKS, LRU_BLOCK, LRU_BLOCK), f32) * LRU_BLOCK ** -0.5,
        "lru_b_x": 0.02 * jax.random.normal(ks[15], (DEPTH, LRU_WIDTH), f32),
        "lru_lambda": lam,
        "attn_sinks": 0.5 * jax.random.normal(ks[16], (DEPTH, N_HEADS), f32),
        "w_branch": jax.random.normal(ks[17], (DEPTH, LRU_WIDTH + ATTN_WIDTH, D_MODEL), f32) * D_MODEL ** -0.5,
        "w_out": jax.random.normal(ks[18], (DEPTH, D_MODEL, D_MODEL), f32) * D_MODEL ** -0.5,
    }


def reference(x_prompt, x_sample, state_conv, state_h, cache_k_win, cache_v_win, norm_pre, norm_post,
              w_in, conv_w, conv_b, lru_w_a, lru_b_a, lru_w_x, lru_b_x, lru_lambda, attn_sinks,
              w_branch, w_out):
    xp_res, xs_res = x_prompt, x_sample
    pc, ph, pk, pv = [], [], [], []
    sc, sh, sk, sv = [], [], [], []
    for l in range(DEPTH):
        lw = (w_in[l], conv_w[l], conv_b[l], lru_w_a[l], lru_b_a[l], lru_w_x[l], lru_b_x[l],
              lru_lambda[l], attn_sinks[l], w_branch[l], w_out[l])
        xn = _rmsnorm(xp_res, norm_pre[l])
        zero_conv = jnp.zeros((xp_res.shape[0], CONV_WIDTH - 1, LRU_WIDTH), xp_res.dtype)
        zero_h = jnp.zeros((xp_res.shape[0], LRU_WIDTH), jnp.float32)
        o, c_new, h_new, k_new, v_new = _mixer(xn, zero_conv, zero_h, None, None, *lw, True)
        xp_res = xp_res + _rmsnorm(o, norm_post[l])
        pc.append(c_new)
        ph.append(h_new.astype(state_h.dtype))
        pk.append(k_new)
        pv.append(v_new)
        xn = _rmsnorm(xs_res, norm_pre[l])
        o, c_new, h_new, k_new, v_new = _mixer(xn, state_conv[l], state_h[l], cache_k_win[l], cache_v_win[l], *lw, False)
        xs_res = xs_res + _rmsnorm(o, norm_post[l])
        sc.append(c_new)
        sh.append(h_new.astype(state_h.dtype))
        sk.append(k_new)
        sv.append(v_new)
    return (xp_res, xs_res,
            jnp.stack(pc), jnp.stack(ph), jnp.stack(pk), jnp.stack(pv),
            jnp.stack(sc), jnp.stack(sh), jnp.stack(sk), jnp.stack(sv))
```

```python
import functools

import jax
import jax.numpy as jnp
from jax import lax
from jax.experimental import pallas as pl
from jax.experimental.pallas import tpu as pltpu

HEAD_DIM = 64
WINDOW = 128
CONV_WIDTH = 4
LRU_C = 8.0
RMS_EPS = 1e-6
NEG_INF = -1e30

LANES = 128
SUBLANES = 8
BF16_ROWS = 16
VMEM_LIMIT_BYTES = 58 * 1024 * 1024

F32 = jnp.float32
BF16 = jnp.bfloat16


def _largest_divisor(n, candidates):
    for c in candidates:
        if n % c == 0:
            return c
    raise ValueError(f"no tile in {candidates} divides {n}")


def _params(sem):
    return pltpu.CompilerParams(dimension_semantics=sem, vmem_limit_bytes=VMEM_LIMIT_BYTES)


def _dot(a, b):
    return jnp.dot(a, b, preferred_element_type=F32)


def _sigmoid(x):
    return jax.nn.sigmoid(x)


def _silu(x):
    return x * jax.nn.sigmoid(x)


def _rmsnorm_kernel(xp_ref, xs_ref, g_ref, o_ref, *, n_p):
    i = pl.program_id(0)

    def body(x_ref):
        x = x_ref[...]
        ms = jnp.mean(x * x, axis=-1, keepdims=True)
        o_ref[...] = ((x * lax.rsqrt(ms + RMS_EPS)) * g_ref[...]).astype(o_ref.dtype)

    @pl.when(i < n_p)
    def _():
        body(xp_ref)

    @pl.when(i >= n_p)
    def _():
        body(xs_ref)


def _rmsnorm_pre(xp, xs, g, tm):
    s, d = xp.shape
    ms = xs.shape[0]
    n_p, n_s = s // tm, ms // tm
    return pl.pallas_call(
        functools.partial(_rmsnorm_kernel, n_p=n_p),
        out_shape=jax.ShapeDtypeStruct((s + ms, d), BF16),
        grid=(n_p + n_s,),
        in_specs=[
            pl.BlockSpec((tm, d), lambda i: (jnp.minimum(i, n_p - 1), 0)),
            pl.BlockSpec((tm, d), lambda i: (jnp.maximum(i - n_p, 0), 0)),
            pl.BlockSpec((1, d), lambda i: (0, 0)),
        ],
        out_specs=pl.BlockSpec((tm, d), lambda i: (i, 0)),
        compiler_params=_params(("arbitrary",)),
        name="rmsnorm_pre",
    )(xp, xs, g)


def _cast_rows(src_ref, dst_ref, rows_per_step=256):
    n = src_ref.shape[0]
    step = _largest_divisor(n, (rows_per_step, 128, 64, 32, 16))

    def body(r, carry):
        r0 = pl.multiple_of(r * step, step)
        dst_ref[pl.ds(r0, step), :] = src_ref[pl.ds(r0, step), :].astype(dst_ref.dtype)
        return carry

    lax.fori_loop(0, n // step, body, 0)


def _in_proj_kernel(x_ref, w_ref, o_ref, wb_ref):
    @pl.when(pl.program_id(1) == 0)
    def _():
        _cast_rows(w_ref, wb_ref)

    o_ref[...] = _dot(x_ref[...], wb_ref[...]).astype(o_ref.dtype)


def _in_proj(xn, w_in, tm):
    m, d = xn.shape
    n = w_in.shape[1]
    tn = _largest_divisor(n, (1024, 512, 256, 128))
    return pl.pallas_call(
        _in_proj_kernel,
        out_shape=jax.ShapeDtypeStruct((m, n), BF16),
        grid=(n // tn, m // tm),
        in_specs=[
            pl.BlockSpec((tm, d), lambda j, i: (i, 0)),
            pl.BlockSpec((d, tn), lambda j, i: (0, j)),
        ],
        out_specs=pl.BlockSpec((tm, tn), lambda j, i: (i, j)),
        scratch_shapes=[pltpu.VMEM((d, tn), BF16)],
        compiler_params=_params(("arbitrary", "arbitrary")),
        name="in_proj",
    )(xn, w_in)


def _scan_within_tiles(a, b):
    sub = lax.broadcasted_iota(jnp.int32, a.shape, 0) & (SUBLANES - 1)
    shift = 1
    while shift < SUBLANES:
        keep = sub >= shift
        a_sh = jnp.where(keep, pltpu.roll(a, shift, axis=0), 1.0)
        b_sh = jnp.where(keep, pltpu.roll(b, shift, axis=0), 0.0)
        b = b + a * b_sh
        a = a * a_sh
        shift *= 2
    return a, b


def _lru_kernel(x_ref, g_ref, cw_ref, cb_ref, wa_ref, ba_ref, wx_ref, bx_ref, lam_ref,
                cst_ref, h0_ref, o_ref, hp_ref, hs_ref, xtail_ref, hc_ref, *, n_p, chunk):
    i = pl.program_id(1)
    tb, cbw = x_ref.shape
    nb = h0_ref.shape[0]
    ts = tb // nb
    blk = wa_ref.shape[1]
    w = cw_ref[...]
    bias = cb_ref[...]
    lam = lam_ref[...]
    neg_c_softplus = -LRU_C * (jnp.maximum(-lam, 0.0) + jnp.log(1.0 + jnp.exp(-jnp.abs(lam))))
    b_a = ba_ref[...]
    b_x = bx_ref[...]

    def gates(xc):
        xcb = xc.astype(BF16)
        ga, gx = [], []
        for q in range(cbw // blk):
            xq = xcb[:, q * blk:(q + 1) * blk]
            ga.append(_dot(xq, wa_ref[q]))
            gx.append(_dot(xq, wx_ref[q]))
        r_a = _sigmoid(jnp.concatenate(ga, axis=1) + b_a)
        r_x = _sigmoid(jnp.concatenate(gx, axis=1) + b_x)
        log_a = neg_c_softplus * r_a
        a = jnp.exp(log_a)
        b = jnp.sqrt(1.0 - a * a) * (r_x * xc)
        return a, b

    def conv(taps):
        out = bias
        for k in range(CONV_WIDTH):
            out = out + taps[k] * w[k:k + 1, :]
        return out

    @pl.when(i < n_p)
    def _prompt():
        @pl.when(i == 0)
        def _():
            xtail_ref[...] = jnp.zeros_like(xtail_ref)
            hc_ref[...] = jnp.zeros_like(hc_ref)

        def chunk_body(c, carry):
            r0 = pl.multiple_of(c * chunk, chunk)
            xcur = x_ref[pl.ds(r0, chunk), :].astype(F32)
            xx = jnp.concatenate([xtail_ref[...], xcur], axis=0)
            taps = []
            for k in range(CONV_WIDTH):
                back = CONV_WIDTH - 1 - k
                sh = pltpu.roll(xx, back, axis=0) if back else xx
                taps.append(sh[SUBLANES:, :])
            xtail_ref[...] = xcur[chunk - SUBLANES:, :]
            a, b = gates(conv(taps))
            a_in, b_in = _scan_within_tiles(a, b)
            h_in = hc_ref[...]
            hs = []
            for v in range(chunk // SUBLANES):
                rows = slice(v * SUBLANES, (v + 1) * SUBLANES)
                hv = a_in[rows] * h_in + b_in[rows]
                hs.append(hv)
                h_in = jnp.broadcast_to(hv[SUBLANES - 1:SUBLANES, :], hv.shape)
            hc_ref[...] = h_in
            h = jnp.concatenate(hs, axis=0)
            g = g_ref[pl.ds(r0, chunk), :].astype(F32)
            o_ref[pl.ds(r0, chunk), :] = (h * _silu(g)).astype(o_ref.dtype)
            return carry

        lax.fori_loop(0, tb // chunk, chunk_body, 0)

        @pl.when(i == n_p - 1)
        def _():
            hp_ref[...] = hc_ref[0:1, :]

    @pl.when(i >= n_p)
    def _sample():
        xs = [cst_ref[k] for k in range(CONV_WIDTH - 1)]
        xs += [x_ref[t * nb:(t + 1) * nb, :].astype(F32) for t in range(ts)]
        h = h0_ref[...]
        for t in range(ts):
            a, b = gates(conv(xs[t:t + CONV_WIDTH]))
            h = a * h + b
            g = g_ref[t * nb:(t + 1) * nb, :].astype(F32)
            o_ref[t * nb:(t + 1) * nb, :] = (h * _silu(g)).astype(o_ref.dtype)
        hs_ref[...] = h


def _lru(z, conv_w, conv_b, w_a, b_a, w_x, b_x, lam, cst_tm, h0, s, tb):
    m = z.shape[0]
    width = conv_w.shape[1]
    nb = h0.shape[0]
    blk = w_a.shape[1]
    n_p = s // tb
    assert m - s == tb, "the sample rows must form exactly one row block"
    cbw = _largest_divisor(width, (512, 256, 128))
    assert cbw % blk == 0
    chunk = min(128, tb)
    n_c = width // cbw
    row = lambda c, i: (0, c)
    return pl.pallas_call(
        functools.partial(_lru_kernel, n_p=n_p, chunk=chunk),
        out_shape=(
            jax.ShapeDtypeStruct((m, width), BF16),
            jax.ShapeDtypeStruct((1, width), F32),
            jax.ShapeDtypeStruct((nb, width), F32),
        ),
        grid=(n_c, m // tb),
        in_specs=[
            pl.BlockSpec((tb, cbw), lambda c, i: (i, c)),
            pl.BlockSpec((tb, cbw), lambda c, i: (i, n_c + c)),
            pl.BlockSpec((CONV_WIDTH, cbw), row),
            pl.BlockSpec((1, cbw), row),
            pl.BlockSpec((cbw // blk, blk, blk), lambda c, i: (c, 0, 0)),
            pl.BlockSpec((1, cbw), row),
            pl.BlockSpec((cbw // blk, blk, blk), lambda c, i: (c, 0, 0)),
            pl.BlockSpec((1, cbw), row),
            pl.BlockSpec((1, cbw), row),
            pl.BlockSpec((CONV_WIDTH - 1, nb, cbw), lambda c, i: (0, 0, c)),
            pl.BlockSpec((nb, cbw), row),
        ],
        out_specs=(
            pl.BlockSpec((tb, cbw), lambda c, i: (i, c)),
            pl.BlockSpec((1, cbw), row),
            pl.BlockSpec((nb, cbw), row),
        ),
        scratch_shapes=[pltpu.VMEM((SUBLANES, cbw), F32), pltpu.VMEM((SUBLANES, cbw), F32)],
        compiler_params=_params(("arbitrary", "arbitrary")),
        name="lru",
    )(z, z, conv_w, conv_b, w_a, b_a, w_x, b_x, lam, cst_tm, h0)


def _head_rows(tile, half, g, lane):
    if (g % 2) != half:
        tile = pltpu.roll(tile, HEAD_DIM, axis=1)
    keep = (lane >= HEAD_DIM * half) & (lane < HEAD_DIM * (half + 1))
    return jnp.where(keep, tile, 0.0)


def _softmax_pv(s, mask, sink_col, v_pair):
    s = jnp.where(mask, s, NEG_INF)
    m = jnp.maximum(jnp.max(s, axis=-1, keepdims=True), sink_col)
    p = jnp.exp(s - m)
    denom = jnp.sum(p, axis=-1, keepdims=True) + jnp.exp(sink_col - m)
    return _dot(p.astype(BF16), v_pair) / denom


def _merge_halves(o, rows, half, j, lane):
    o_lo = o[(2 * j) * rows:(2 * j + 1) * rows]
    o_hi = o[(2 * j + 1) * rows:(2 * j + 2) * rows]
    if half == 0:
        o_hi = pltpu.roll(o_hi, HEAD_DIM, axis=1)
    else:
        o_lo = pltpu.roll(o_lo, HEAD_DIM, axis=1)
    return jnp.where(lane < HEAD_DIM, o_lo, o_hi)


def _attn_prompt_kernel(sink_ref, q_ref, kp_ref, kc_ref, vp_ref, vc_ref, g0_ref, g1_ref, o_ref,
                        *, n_kv, group):
    i = pl.program_id(0)
    qb = q_ref.shape[0]
    aw_half = g0_ref.shape[1]
    scale = HEAD_DIM ** -0.5
    lane = lax.broadcasted_iota(jnp.int32, (qb, LANES), 1)
    r = lax.broadcasted_iota(jnp.int32, (qb, 2 * qb), 0)
    c = lax.broadcasted_iota(jnp.int32, (qb, 2 * qb), 1)
    rel = qb + r - c
    mask1 = (rel >= 0) & (rel <= WINDOW) & ((c >= qb) | (i > 0))
    mask = jnp.concatenate([mask1] * group, axis=0)
    hw = group * HEAD_DIM
    for p in range(n_kv // 2):
        lanes = slice(p * LANES, (p + 1) * LANES)
        k_pair = jnp.concatenate([kp_ref[:, lanes], kc_ref[:, lanes]], axis=0)
        v_pair = jnp.concatenate([vp_ref[:, lanes], vc_ref[:, lanes]], axis=0)
        for half in range(2):
            kh = 2 * p + half
            pieces, sinks = [], []
            for g in range(group):
                c0 = kh * hw + (g // 2) * LANES
                tile = q_ref[:, c0:c0 + LANES].astype(F32) * scale
                pieces.append(_head_rows(tile, half, g, lane).astype(BF16))
                sinks.append(jnp.full((qb, 1), sink_ref[kh * group + g], F32))
            lhs = jnp.concatenate(pieces, axis=0)
            s = lax.dot_general(lhs, k_pair, (((1,), (1,)), ((), ())), preferred_element_type=F32)
            o = _softmax_pv(s, mask, jnp.concatenate(sinks, axis=0), v_pair)
            for j in range(group // 2):
                c0 = kh * hw + j * LANES
                tile = _merge_halves(o, qb, half, j, lane)
                g_ref, gc = (g0_ref, c0) if c0 < aw_half else (g1_ref, c0 - aw_half)
                gate = g_ref[:, gc:gc + LANES].astype(F32)
                o_ref[:, c0:c0 + LANES] = (tile * _silu(gate)).astype(o_ref.dtype)


def _attn_prompt(z, sinks, s, width, aw, kvw, n_kv, group):
    qb = WINDOW
    assert s % qb == 0 and n_kv % 2 == 0 and group % 2 == 0
    q_blk = (2 * width) // aw
    k_blk = (2 * width + aw) // kvw
    v_blk = k_blk + 1
    g_blk = (2 * width + aw + 2 * kvw) // (aw // 2)
    assert (2 * width) % aw == 0 and (2 * width + aw) % kvw == 0
    assert (2 * width + aw + 2 * kvw) % (aw // 2) == 0
    prev = lambda i: jnp.maximum(i - 1, 0)
    return pl.pallas_call(
        functools.partial(_attn_prompt_kernel, n_kv=n_kv, group=group),
        out_shape=jax.ShapeDtypeStruct((s, aw), BF16),
        grid=(s // qb,),
        in_specs=[
            pl.BlockSpec(memory_space=pltpu.SMEM),
            pl.BlockSpec((qb, aw), lambda i: (i, q_blk)),
            pl.BlockSpec((qb, kvw), lambda i: (prev(i), k_blk)),
            pl.BlockSpec((qb, kvw), lambda i: (i, k_blk)),
            pl.BlockSpec((qb, kvw), lambda i: (prev(i), v_blk)),
            pl.BlockSpec((qb, kvw), lambda i: (i, v_blk)),
            pl.BlockSpec((qb, aw // 2), lambda i: (i, g_blk)),
            pl.BlockSpec((qb, aw // 2), lambda i: (i, g_blk + 1)),
        ],
        out_specs=pl.BlockSpec((qb, aw), lambda i: (i, 0)),
        compiler_params=_params(("arbitrary",)),
        name="attn_prompt",
    )(sinks, z, z, z, z, z, z, z)


def _attn_sample_kernel(sink_ref, q_ref, kn_ref, vn_ref, ck_ref, cv_ref, g_ref, o_ref,
                        *, n_kv, group, ts):
    sb, tq, _ = q_ref.shape
    wb = ck_ref.shape[1]
    tk = kn_ref.shape[1]
    scale = HEAD_DIM ** -0.5
    lane = lax.broadcasted_iota(jnp.int32, (tq, LANES), 1)
    t = lax.broadcasted_iota(jnp.int32, (tq, wb + tk), 0)
    c = lax.broadcasted_iota(jnp.int32, (tq, wb + tk), 1)
    in_cache = (c < wb) & (t + wb - c <= WINDOW)
    in_new = (c >= wb) & (c - wb <= t) & (c - wb < ts)
    mask = jnp.concatenate([in_cache | in_new] * group, axis=0)
    hw = group * HEAD_DIM

    def seq_body(n, carry):
        for p in range(n_kv // 2):
            lanes = slice(p * LANES, (p + 1) * LANES)
            k_pair = jnp.concatenate([ck_ref[n, :, lanes], kn_ref[n, :, lanes]], axis=0).astype(BF16)
            v_pair = jnp.concatenate([cv_ref[n, :, lanes], vn_ref[n, :, lanes]], axis=0).astype(BF16)
            for half in range(2):
                kh = 2 * p + half
                pieces, sinks = [], []
                for g in range(group):
                    c0 = kh * hw + (g // 2) * LANES
                    tile = q_ref[n, :, c0:c0 + LANES] * scale
                    pieces.append(_head_rows(tile, half, g, lane))
                    sinks.append(jnp.full((tq, 1), sink_ref[kh * group + g], F32))
                lhs = jnp.concatenate(pieces, axis=0).astype(BF16)
                s = lax.dot_general(lhs, k_pair, (((1,), (1,)), ((), ())),
                                    preferred_element_type=F32)
                o = _softmax_pv(s, mask, jnp.concatenate(sinks, axis=0), v_pair)
                for j in range(group // 2):
                    c0 = kh * hw + j * LANES
                    tile = _merge_halves(o, tq, half, j, lane)
                    gate = g_ref[n, :, c0:c0 + LANES]
                    o_ref[n, :, c0:c0 + LANES] = tile * _silu(gate)
        return carry

    lax.fori_loop(0, sb, seq_body, 0)


def _attn_sample(q8, kn, vn, cache_k, cache_v, g8, sinks, n_kv, group, ts):
    nb, tq, aw = q8.shape
    tk = kn.shape[1]
    wb, kvw = cache_k.shape[1], cache_k.shape[2]
    sb = _largest_divisor(nb, (8, 4, 2, 1))
    blk3 = lambda rows, width: pl.BlockSpec((sb, rows, width), lambda i: (i, 0, 0))
    return pl.pallas_call(
        functools.partial(_attn_sample_kernel, n_kv=n_kv, group=group, ts=ts),
        out_shape=jax.ShapeDtypeStruct((nb, tq, aw), F32),
        grid=(nb // sb,),
        in_specs=[
            pl.BlockSpec(memory_space=pltpu.SMEM),
            blk3(tq, aw), blk3(tk, kvw), blk3(tk, kvw), blk3(wb, kvw), blk3(wb, kvw), blk3(tq, aw),
        ],
        out_specs=blk3(tq, aw),
        compiler_params=_params(("arbitrary",)),
        name="attn_sample",
    )(sinks, q8, kn, vn, cache_k, cache_v, g8)


def _branch_kernel(oa_ref, ob_ref, w_ref, ma_ref, mb_ref, o_ref, wb_ref):
    @pl.when(pl.program_id(1) == 0)
    def _():
        _cast_rows(w_ref, wb_ref)

    width = oa_ref.shape[1]
    pa = _dot(oa_ref[...], wb_ref[0:width, :])
    pb = _dot(ob_ref[...], wb_ref[width:, :])
    ma = ma_ref[...].astype(F32)
    mb = mb_ref[...].astype(F32)
    o_ref[...] = (_sigmoid(ma) * pa + _sigmoid(mb) * pb).astype(o_ref.dtype)


def _branch_merge(o_a, o_b, w_branch, z, ma_col, tm, tn):
    m, width = o_a.shape
    aw = o_b.shape[1]
    d = w_branch.shape[1]
    ma_blk = ma_col // tn
    mb_blk = (ma_col + d) // tn
    return pl.pallas_call(
        _branch_kernel,
        out_shape=jax.ShapeDtypeStruct((m, d), BF16),
        grid=(d // tn, m // tm),
        in_specs=[
            pl.BlockSpec((tm, width), lambda j, i: (i, 0)),
            pl.BlockSpec((tm, aw), lambda j, i: (i, 0)),
            pl.BlockSpec((width + aw, tn), lambda j, i: (0, j)),
            pl.BlockSpec((tm, tn), lambda j, i: (i, ma_blk + j)),
            pl.BlockSpec((tm, tn), lambda j, i: (i, mb_blk + j)),
        ],
        out_specs=pl.BlockSpec((tm, tn), lambda j, i: (i, j)),
        scratch_shapes=[pltpu.VMEM((width + aw, tn), BF16)],
        compiler_params=_params(("arbitrary", "arbitrary")),
        name="branch_merge",
    )(o_a, o_b, w_branch, z, z)


def _out_kernel(m_ref, w_ref, g_ref, xp_ref, xs_ref, yp_ref, ys_ref, acc_ref, ss_ref,
                *, n_p, n_t, d):
    i = pl.program_id(0)
    j = pl.program_id(1)

    @pl.when(j < n_t)
    def _matmul():
        t = _dot(m_ref[...], w_ref[...].astype(BF16))
        acc_ref[j] = t
        ssq = jnp.sum(t * t, axis=-1, keepdims=True)

        @pl.when(j == 0)
        def _():
            ss_ref[...] = ssq

        @pl.when(j > 0)
        def _():
            ss_ref[...] += ssq

    @pl.when(j >= n_t)
    def _epilogue():
        inv = lax.rsqrt(ss_ref[...] / d + RMS_EPS)
        y = (acc_ref[j - n_t] * inv) * g_ref[...]

        @pl.when(i < n_p)
        def _():
            yp_ref[...] = xp_ref[...] + y

        @pl.when(i >= n_p)
        def _():
            ys_ref[...] = xs_ref[...] + y


def _out_proj(merged, w_out, g, xp, xs, tm, tn):
    m, d = merged.shape
    s = xp.shape[0]
    n_p, n_s = s // tm, (m - s) // tm
    n_t = d // tn
    tile = lambda j: jnp.maximum(j - n_t, 0)
    p_map = lambda i, j: (jnp.minimum(i, n_p - 1), jnp.where(i >= n_p, n_t - 1, tile(j)))
    s_map = lambda i, j: (jnp.maximum(i - n_p, 0), jnp.where(i < n_p, 0, tile(j)))
    return pl.pallas_call(
        functools.partial(_out_kernel, n_p=n_p, n_t=n_t, d=d),
        out_shape=(jax.ShapeDtypeStruct((s, d), F32), jax.ShapeDtypeStruct((m - s, d), F32)),
        grid=(n_p + n_s, 2 * n_t),
        in_specs=[
            pl.BlockSpec((tm, d), lambda i, j: (i, 0)),
            pl.BlockSpec((d, tn), lambda i, j: (0, jnp.minimum(j, n_t - 1))),
            pl.BlockSpec((1, tn), lambda i, j: (0, tile(j))),
            pl.BlockSpec((tm, tn), p_map),
            pl.BlockSpec((tm, tn), s_map),
        ],
        out_specs=(pl.BlockSpec((tm, tn), p_map), pl.BlockSpec((tm, tn), s_map)),
        scratch_shapes=[pltpu.VMEM((n_t, tm, tn), F32), pltpu.VMEM((tm, 1), F32)],
        compiler_params=_params(("arbitrary", "arbitrary")),
        name="out_proj",
    )(merged, w_out, g, xp, xs)


def _pad_rows(x, rows):
    return jnp.pad(x, ((0, 0), (0, rows - x.shape[1]), (0, 0)))


def _layer(xp, xs_tm, conv_state, h_state, cache_k, cache_v, norm_pre, norm_post, w_in, conv_w,
           conv_b, w_a, b_a, w_x, b_x, lam, sinks, w_branch, w_out):
    s, d = xp.shape
    nb, wb, n_kv, _ = cache_k.shape
    ts = xs_tm.shape[0] // nb
    width = conv_w.shape[1]
    n_heads = sinks.shape[0]
    group = n_heads // n_kv
    aw, kvw = n_heads * HEAD_DIM, n_kv * HEAD_DIM
    m = s + nb * ts
    tm = nb * ts
    assert s % tm == 0 and tm % SUBLANES == 0 and ts >= CONV_WIDTH - 1
    q_col = 2 * width
    k_col = q_col + aw
    v_col = k_col + kvw
    g_col = v_col + kvw
    ma_col = g_col + aw
    assert w_in.shape[1] == ma_col + 2 * d

    xn = _rmsnorm_pre(xp, xs_tm, norm_pre.reshape(1, d), _largest_divisor(tm, (256, 128, 64, 8)))
    z = _in_proj(xn, w_in, tm)

    cst_tm = conv_state.transpose(1, 0, 2)
    o_a, h_p, h_s = _lru(z, conv_w, conv_b.reshape(1, width), w_a.astype(BF16),
                         b_a.reshape(1, width), w_x.astype(BF16), b_x.reshape(1, width),
                         lam.reshape(1, width), cst_tm, h_state, s, tm)

    o_b_p = _attn_prompt(z, sinks, s, width, aw, kvw, n_kv, group)

    def sample_cols(c0, c1):
        blk = lax.slice(z, (s, c0), (m, c1)).reshape(ts, nb, c1 - c0)
        return blk.transpose(1, 0, 2).astype(F32)

    q_s, k_s, v_s = sample_cols(q_col, k_col), sample_cols(k_col, v_col), sample_cols(v_col, g_col)
    g_s = sample_cols(g_col, ma_col)
    tq = -(-ts // SUBLANES) * SUBLANES
    tk = -(-ts // BF16_ROWS) * BF16_ROWS
    ck = cache_k.reshape(nb, wb, kvw)
    cv = cache_v.reshape(nb, wb, kvw)
    o_b_s = _attn_sample(_pad_rows(q_s, tq), _pad_rows(k_s, tk), _pad_rows(v_s, tk), ck, cv,
                         _pad_rows(g_s, tq), sinks, n_kv, group, ts)
    o_b_s = o_b_s[:, :ts].transpose(1, 0, 2).reshape(nb * ts, aw).astype(BF16)
    o_b = jnp.concatenate([o_b_p, o_b_s], axis=0)

    tn = _largest_divisor(ma_col, (512, 256, 128))
    assert d % tn == 0
    merged = _branch_merge(o_a, o_b, w_branch, z, ma_col, tm, tn)
    y_p, y_s = _out_proj(merged, w_out, norm_post.reshape(1, d), xp, xs_tm, tm, tn)

    keep = CONV_WIDTH - 1
    new_conv_p = lax.slice(z, (s - keep, 0), (s, width)).astype(F32)[None]
    wbp = min(WINDOW, s)
    new_k_p = lax.slice(z, (s - wbp, k_col), (s, v_col)).astype(F32).reshape(1, wbp, n_kv, HEAD_DIM)
    new_v_p = lax.slice(z, (s - wbp, v_col), (s, g_col)).astype(F32).reshape(1, wbp, n_kv, HEAD_DIM)
    x_lru_s = sample_cols(0, width)
    new_conv_s = jnp.concatenate([conv_state, x_lru_s], axis=1)[:, -keep:]
    new_k_s = jnp.concatenate([ck, k_s], axis=1)[:, -wb:].reshape(nb, wb, n_kv, HEAD_DIM)
    new_v_s = jnp.concatenate([cv, v_s], axis=1)[:, -wb:].reshape(nb, wb, n_kv, HEAD_DIM)
    return y_p, y_s, (new_conv_p, h_p, new_k_p, new_v_p), (new_conv_s, h_s, new_k_s, new_v_s)


def kernel(x_prompt, x_sample, state_conv, state_h, cache_k_win, cache_v_win, norm_pre, norm_post, w_in, conv_w, conv_b, lru_w_a, lru_b_a, lru_w_x, lru_b_x, lru_lambda, attn_sinks, w_branch, w_out):
    batch, s, d = x_prompt.shape
    nb, ts, _ = x_sample.shape
    assert batch == 1, "the prompt group is a single sequence"
    depth = w_in.shape[0]
    xp = x_prompt.reshape(s, d)
    xs_tm = x_sample.transpose(1, 0, 2).reshape(ts * nb, d)
    p_states, s_states = [], []
    for l in range(depth):
        xp, xs_tm, p_new, s_new = _layer(
            xp, xs_tm, state_conv[l], state_h[l], cache_k_win[l], cache_v_win[l], norm_pre[l],
            norm_post[l], w_in[l], conv_w[l], conv_b[l], lru_w_a[l], lru_b_a[l], lru_w_x[l],
            lru_b_x[l], lru_lambda[l], attn_sinks[l], w_branch[l], w_out[l])
        p_states.append(p_new)
        s_states.append(s_new)
    y_prompt = xp.reshape(1, s, d)
    y_sample = xs_tm.reshape(ts, nb, d).transpose(1, 0, 2)
    stack = lambda states, k: jnp.stack([st[k] for st in states])
    return (y_prompt, y_sample,
            stack(p_states, 0), stack(p_states, 1), stack(p_states, 2), stack(p_states, 3),
            stack(s_states, 0), stack(s_states, 1), stack(s_states, 2), stack(s_states, 3))
```

```python
import functools

import jax
import jax.numpy as jnp
from jax import lax
from jax.experimental import pallas as pl
from jax.experimental.pallas import tpu as pltpu

HEAD_DIM = 64
WINDOW = 128
CONV_WIDTH = 4
LRU_C = 8.0
RMS_EPS = 1e-6
NEG_INF = -1e30

LANES = 128
SUBLANES = 8
BF16_ROWS = 16
VMEM_LIMIT_BYTES = 58 * 1024 * 1024

F32 = jnp.float32
BF16 = jnp.bfloat16


def _largest_divisor(n, candidates):
    for c in candidates:
        if n % c == 0:
            return c
    raise ValueError(f"no tile in {candidates} divides {n}")


def _params(sem):
    return pltpu.CompilerParams(dimension_semantics=sem, vmem_limit_bytes=VMEM_LIMIT_BYTES)


def _dot(a, b):
    return jnp.dot(a, b, preferred_element_type=F32)


def _dot_nt(a, b):
    return lax.dot_general(a, b, (((1,), (1,)), ((), ())), preferred_element_type=F32)


def _sigmoid(x):
    return jax.nn.sigmoid(x)


def _silu(x):
    return x * jax.nn.sigmoid(x)


def _rmsnorm_kernel(xp_ref, xs_ref, g_ref, o_ref, *, n_p):
    i = pl.program_id(0)

    def body(x_ref):
        x = x_ref[...]
        ms = jnp.mean(x * x, axis=-1, keepdims=True)
        o_ref[...] = ((x * lax.rsqrt(ms + RMS_EPS)) * g_ref[...]).astype(o_ref.dtype)

    @pl.when(i < n_p)
    def _():
        body(xp_ref)

    @pl.when(i >= n_p)
    def _():
        body(xs_ref)


def _rmsnorm_pre(xp, xs, g, tm):
    s, d = xp.shape
    ms = xs.shape[0]
    n_p, n_s = s // tm, ms // tm
    return pl.pallas_call(
        functools.partial(_rmsnorm_kernel, n_p=n_p),
        out_shape=jax.ShapeDtypeStruct((s + ms, d), BF16),
        grid=(n_p + n_s,),
        in_specs=[
            pl.BlockSpec((tm, d), lambda i: (jnp.minimum(i, n_p - 1), 0)),
            pl.BlockSpec((tm, d), lambda i: (jnp.maximum(i - n_p, 0), 0)),
            pl.BlockSpec((1, d), lambda i: (0, 0)),
        ],
        out_specs=pl.BlockSpec((tm, d), lambda i: (i, 0)),
        compiler_params=_params(("arbitrary",)),
        name="rmsnorm_pre",
    )(xp, xs, g)


def _cast_rows(src_ref, dst_ref, rows_per_step=256):
    n = src_ref.shape[0]
    step = _largest_divisor(n, (rows_per_step, 128, 64, 32, 16))

    def body(r, carry):
        r0 = pl.multiple_of(r * step, step)
        dst_ref[pl.ds(r0, step), :] = src_ref[pl.ds(r0, step), :].astype(dst_ref.dtype)
        return carry

    lax.fori_loop(0, n // step, body, 0)


def _in_proj_kernel(x_ref, w_ref, o_ref, wb_ref):
    @pl.when(pl.program_id(1) == 0)
    def _():
        _cast_rows(w_ref, wb_ref)

    o_ref[...] = _dot(x_ref[...], wb_ref[...]).astype(o_ref.dtype)


def _in_proj(xn, w_in, tm):
    m, d = xn.shape
    n = w_in.shape[1]
    tn = _largest_divisor(n, (1024, 512, 256, 128))
    return pl.pallas_call(
        _in_proj_kernel,
        out_shape=jax.ShapeDtypeStruct((m, n), BF16),
        grid=(n // tn, m // tm),
        in_specs=[
            pl.BlockSpec((tm, d), lambda j, i: (i, 0)),
            pl.BlockSpec((d, tn), lambda j, i: (0, j)),
        ],
        out_specs=pl.BlockSpec((tm, tn), lambda j, i: (i, j)),
        scratch_shapes=[pltpu.VMEM((d, tn), BF16)],
        compiler_params=_params(("arbitrary", "arbitrary")),
        name="in_proj",
    )(xn, w_in)


def _scan_tile(a, b, sub):
    shift = 1
    while shift < SUBLANES:
        keep = sub >= shift
        a_sh = jnp.where(keep, pltpu.roll(a, shift, axis=0), 1.0)
        b_sh = jnp.where(keep, pltpu.roll(b, shift, axis=0), 0.0)
        b = b + a * b_sh
        a = a * a_sh
        shift *= 2
    return a, b


def _lru_kernel(x_ref, g_ref, cw_ref, cb_ref, wa_ref, ba_ref, wx_ref, bx_ref, lam_ref,
                cst_ref, h0_ref, o_ref, hp_ref, hs_ref, xtail_ref, hc_ref, *, n_p, chunk):
    i = pl.program_id(1)
    tb, cbw = x_ref.shape
    nb = h0_ref.shape[0]
    ts = tb // nb
    blk = wa_ref.shape[1]
    w = cw_ref[...]
    bias = cb_ref[...]
    lam = lam_ref[...]
    neg_c_softplus = -LRU_C * (jnp.maximum(-lam, 0.0) + jnp.log(1.0 + jnp.exp(-jnp.abs(lam))))
    b_a = ba_ref[...]
    b_x = bx_ref[...]

    def gates(xc):
        xcb = xc.astype(BF16)
        ga, gx = [], []
        for q in range(cbw // blk):
            xq = xcb[:, q * blk:(q + 1) * blk]
            ga.append(_dot(xq, wa_ref[q]))
            gx.append(_dot(xq, wx_ref[q]))
        r_a = _sigmoid(jnp.concatenate(ga, axis=1) + b_a)
        r_x = _sigmoid(jnp.concatenate(gx, axis=1) + b_x)
        log_a = neg_c_softplus * r_a
        a = jnp.exp(log_a)
        b = jnp.sqrt(1.0 - a * a) * (r_x * xc)
        return a, b

    def conv(taps):
        out = bias
        for k in range(CONV_WIDTH):
            out = out + taps[k] * w[k:k + 1, :]
        return out

    @pl.when(i < n_p)
    def _prompt():
        @pl.when(i == 0)
        def _():
            xtail_ref[...] = jnp.zeros_like(xtail_ref)
            hc_ref[...] = jnp.zeros_like(hc_ref)

        sub = lax.broadcasted_iota(jnp.int32, (SUBLANES, cbw), 0)
        n_tiles = chunk // SUBLANES

        def chunk_body(c, carry):
            r0 = pl.multiple_of(c * chunk, chunk)
            xcur = x_ref[pl.ds(r0, chunk), :].astype(F32)
            tiles = [xtail_ref[...]]
            tiles += [xcur[v * SUBLANES:(v + 1) * SUBLANES] for v in range(n_tiles)]
            xtail_ref[...] = tiles[-1]
            taps = []
            for k in range(CONV_WIDTH):
                back = CONV_WIDTH - 1 - k
                if back == 0:
                    taps.append(xcur)
                    continue
                rolled = [pltpu.roll(t, back, axis=0) for t in tiles]
                taps.append(jnp.concatenate(
                    [jnp.where(sub >= back, rolled[v + 1], rolled[v]) for v in range(n_tiles)],
                    axis=0))
            a, b = gates(conv(taps))
            h_in = hc_ref[...]
            hs = []
            for v in range(n_tiles):
                rows = slice(v * SUBLANES, (v + 1) * SUBLANES)
                a_in, b_in = _scan_tile(a[rows], b[rows], sub)
                hv = a_in * h_in + b_in
                hs.append(hv)
                h_in = jnp.broadcast_to(hv[SUBLANES - 1:SUBLANES, :], hv.shape)
            hc_ref[...] = h_in
            h = jnp.concatenate(hs, axis=0)
            g = g_ref[pl.ds(r0, chunk), :].astype(F32)
            o_ref[pl.ds(r0, chunk), :] = (h * _silu(g)).astype(o_ref.dtype)
            return carry

        lax.fori_loop(0, tb // chunk, chunk_body, 0)

        @pl.when(i == n_p - 1)
        def _():
            hp_ref[...] = hc_ref[0:1, :]

    @pl.when(i >= n_p)
    def _sample():
        xs = [cst_ref[k] for k in range(CONV_WIDTH - 1)]
        xs += [x_ref[t * nb:(t + 1) * nb, :].astype(F32) for t in range(ts)]
        h = h0_ref[...]
        for t in range(ts):
            a, b = gates(conv(xs[t:t + CONV_WIDTH]))
            h = a * h + b
            g = g_ref[t * nb:(t + 1) * nb, :].astype(F32)
            o_ref[t * nb:(t + 1) * nb, :] = (h * _silu(g)).astype(o_ref.dtype)
        hs_ref[...] = h


def _lru(z, conv_w, conv_b, w_a, b_a, w_x, b_x, lam, cst_tm, h0, s, tb):
    m = z.shape[0]
    width = conv_w.shape[1]
    nb = h0.shape[0]
    blk = w_a.shape[1]
    n_p = s // tb
    assert m - s == tb, "the sample rows must form exactly one row block"
    cbw = _largest_divisor(width, (512, 256, 128))
    assert cbw % blk == 0
    chunk = min(128, tb)
    n_c = width // cbw
    row = lambda c, i: (0, c)
    return pl.pallas_call(
        functools.partial(_lru_kernel, n_p=n_p, chunk=chunk),
        out_shape=(
            jax.ShapeDtypeStruct((m, width), BF16),
            jax.ShapeDtypeStruct((1, width), F32),
            jax.ShapeDtypeStruct((nb, width), F32),
        ),
        grid=(n_c, m // tb),
        in_specs=[
            pl.BlockSpec((tb, cbw), lambda c, i: (i, c)),
            pl.BlockSpec((tb, cbw), lambda c, i: (i, n_c + c)),
            pl.BlockSpec((CONV_WIDTH, cbw), row),
            pl.BlockSpec((1, cbw), row),
            pl.BlockSpec((cbw // blk, blk, blk), lambda c, i: (c, 0, 0)),
            pl.BlockSpec((1, cbw), row),
            pl.BlockSpec((cbw // blk, blk, blk), lambda c, i: (c, 0, 0)),
            pl.BlockSpec((1, cbw), row),
            pl.BlockSpec((1, cbw), row),
            pl.BlockSpec((CONV_WIDTH - 1, nb, cbw), lambda c, i: (0, 0, c)),
            pl.BlockSpec((nb, cbw), row),
        ],
        out_specs=(
            pl.BlockSpec((tb, cbw), lambda c, i: (i, c)),
            pl.BlockSpec((1, cbw), row),
            pl.BlockSpec((nb, cbw), row),
        ),
        scratch_shapes=[pltpu.VMEM((SUBLANES, cbw), F32), pltpu.VMEM((SUBLANES, cbw), F32)],
        compiler_params=_params(("arbitrary", "arbitrary")),
        name="lru",
    )(z, z, conv_w, conv_b, w_a, b_a, w_x, b_x, lam, cst_tm, h0)


def _attn_prompt_kernel(sink_ref, q_ref, kp_ref, kc_ref, vp_ref, vc_ref, g0_ref, g1_ref, o_ref,
                        *, n_kv, group):
    i = pl.program_id(0)
    qb = q_ref.shape[0]
    aw_half = g0_ref.shape[1]
    scale = HEAD_DIM ** -0.5
    hw = group * HEAD_DIM
    c = lax.broadcasted_iota(jnp.int32, (2 * qb, qb), 0)
    r = lax.broadcasted_iota(jnp.int32, (2 * qb, qb), 1)
    rel = qb + r - c
    mask = (rel >= 0) & (rel <= WINDOW) & ((c >= qb) | (i > 0))
    lane = lax.broadcasted_iota(jnp.int32, (qb, LANES), 1)
    half_mask = (lane < HEAD_DIM, lane >= HEAD_DIM)
    zero = jnp.zeros((qb, LANES), BF16)
    for p in range(n_kv // 2):
        lanes = slice(p * LANES, (p + 1) * LANES)
        k_f32 = jnp.concatenate([kp_ref[:, lanes], kc_ref[:, lanes]], axis=0).astype(F32) * scale
        k_variants = (k_f32.astype(BF16), pltpu.roll(k_f32, HEAD_DIM, axis=1).astype(BF16))
        v_pair = jnp.concatenate([vp_ref[:, lanes], vc_ref[:, lanes]], axis=0)
        v_t = v_pair.astype(F32).T.astype(BF16)
        heads = ([], [])
        for half in range(2):
            for g in range(group):
                heads[0 if (g % 2) == half else 1].append((2 * p + half, g))
        out_t = {}
        for variant in range(2):
            q_rows = []
            for kh, g in heads[variant]:
                c0 = kh * hw + (g // 2) * LANES
                q_rows.append(jnp.where(half_mask[g % 2], q_ref[:, c0:c0 + LANES], zero))
            s_t = _dot_nt(k_variants[variant], jnp.concatenate(q_rows, axis=0))
            p_blocks, inv_den = [], []
            for b, (kh, g) in enumerate(heads[variant]):
                sink = sink_ref[kh * group + g]
                sb = jnp.where(mask, s_t[:, b * qb:(b + 1) * qb], NEG_INF)
                m = jnp.maximum(jnp.max(sb, axis=0, keepdims=True), sink)
                pe = jnp.exp(sb - m)
                den = jnp.sum(pe, axis=0, keepdims=True) + jnp.exp(sink - m)
                p_blocks.append(pe.astype(BF16))
                inv_den.append(1.0 / den)
            o_t = _dot(v_t, jnp.concatenate(p_blocks, axis=1))
            for b, (kh, g) in enumerate(heads[variant]):
                r0 = (kh % 2) * HEAD_DIM
                out_t[(kh, g)] = o_t[r0:r0 + HEAD_DIM, b * qb:(b + 1) * qb] * inv_den[b]
        for half in range(2):
            kh = 2 * p + half
            for j in range(group // 2):
                c0 = kh * hw + j * LANES
                tile = jnp.concatenate([out_t[(kh, 2 * j)], out_t[(kh, 2 * j + 1)]], axis=0).T
                g_ref, gc = (g0_ref, c0) if c0 < aw_half else (g1_ref, c0 - aw_half)
                gate = g_ref[:, gc:gc + LANES].astype(F32)
                o_ref[:, c0:c0 + LANES] = (tile * _silu(gate)).astype(o_ref.dtype)


def _attn_prompt(z, sinks, s, width, aw, kvw, n_kv, group):
    qb = WINDOW
    assert s % qb == 0 and n_kv % 2 == 0 and group % 2 == 0
    q_blk = (2 * width) // aw
    k_blk = (2 * width + aw) // kvw
    v_blk = k_blk + 1
    g_blk = (2 * width + aw + 2 * kvw) // (aw // 2)
    assert (2 * width) % aw == 0 and (2 * width + aw) % kvw == 0
    assert (2 * width + aw + 2 * kvw) % (aw // 2) == 0
    prev = lambda i: jnp.maximum(i - 1, 0)
    return pl.pallas_call(
        functools.partial(_attn_prompt_kernel, n_kv=n_kv, group=group),
        out_shape=jax.ShapeDtypeStruct((s, aw), BF16),
        grid=(s // qb,),
        in_specs=[
            pl.BlockSpec(memory_space=pltpu.SMEM),
            pl.BlockSpec((qb, aw), lambda i: (i, q_blk)),
            pl.BlockSpec((qb, kvw), lambda i: (prev(i), k_blk)),
            pl.BlockSpec((qb, kvw), lambda i: (i, k_blk)),
            pl.BlockSpec((qb, kvw), lambda i: (prev(i), v_blk)),
            pl.BlockSpec((qb, kvw), lambda i: (i, v_blk)),
            pl.BlockSpec((qb, aw // 2), lambda i: (i, g_blk)),
            pl.BlockSpec((qb, aw // 2), lambda i: (i, g_blk + 1)),
        ],
        out_specs=pl.BlockSpec((qb, aw), lambda i: (i, 0)),
        compiler_params=_params(("arbitrary",)),
        name="attn_prompt",
    )(sinks, z, z, z, z, z, z, z)


def _attn_sample_kernel(q_ref, kn_ref, vn_ref, ck_ref, cv_ref, g_ref, sink_ref, o_ref,
                        s_ref, p_ref, *, n_kv, group, ts):
    sb, tq, _ = q_ref.shape
    wb = ck_ref.shape[1]
    tk = kn_ref.shape[1]
    n_keys = s_ref.shape[1]
    n_pairs = n_kv // 2
    pair_rows = 2 * group * tq
    seq_rows = n_pairs * pair_rows
    scale = HEAD_DIM ** -0.5
    hw = group * HEAD_DIM
    lane = lax.broadcasted_iota(jnp.int32, (tq, LANES), 1)
    key_pad = jnp.zeros((n_keys - wb - tk, LANES), F32)

    def keys_of(c_ref, n_ref, n, lanes):
        return jnp.concatenate([c_ref[n, :, lanes], n_ref[n, :, lanes], key_pad], axis=0).astype(BF16)

    def score_body(n, carry):
        for p in range(n_pairs):
            lanes = slice(p * LANES, (p + 1) * LANES)
            pieces = []
            for half in range(2):
                for g in range(group):
                    c0 = (2 * p + half) * hw + (g // 2) * LANES
                    tile = q_ref[n, :, c0:c0 + LANES] * scale
                    if (g % 2) != half:
                        tile = pltpu.roll(tile, HEAD_DIM, axis=1)
                    keep = (lane >= HEAD_DIM * half) & (lane < HEAD_DIM * (half + 1))
                    pieces.append(jnp.where(keep, tile, 0.0))
            lhs = jnp.concatenate(pieces, axis=0).astype(BF16)
            r0 = pl.multiple_of(n * seq_rows + p * pair_rows, pair_rows)
            s_ref[pl.ds(r0, pair_rows), :] = _dot_nt(lhs, keys_of(ck_ref, kn_ref, n, lanes))
        return carry

    lax.fori_loop(0, sb, score_body, 0)

    t = lax.broadcasted_iota(jnp.int32, (tq, n_keys), 0)
    c = lax.broadcasted_iota(jnp.int32, (tq, n_keys), 1)
    mask8 = ((c < wb) & (t + wb - c <= WINDOW)) | ((c >= wb) & (c - wb <= t) & (c - wb < ts))
    reps = sb * seq_rows // tq
    mask = jnp.concatenate([mask8] * reps, axis=0)
    sink = jnp.concatenate([sink_ref[:, 0:1]] * sb, axis=0)
    s = jnp.where(mask, s_ref[...], NEG_INF)
    m = jnp.maximum(jnp.max(s, axis=-1, keepdims=True), sink)
    pe = jnp.exp(s - m)
    den = jnp.sum(pe, axis=-1, keepdims=True) + jnp.exp(sink - m)
    p_ref[...] = (pe / den).astype(p_ref.dtype)

    def value_body(n, carry):
        for p in range(n_pairs):
            lanes = slice(p * LANES, (p + 1) * LANES)
            r0 = pl.multiple_of(n * seq_rows + p * pair_rows, pair_rows)
            o = _dot(p_ref[pl.ds(r0, pair_rows), :], keys_of(cv_ref, vn_ref, n, lanes))
            for half in range(2):
                kh = 2 * p + half
                for j in range(group // 2):
                    rows = (half * group + 2 * j) * tq
                    o_lo, o_hi = o[rows:rows + tq], o[rows + tq:rows + 2 * tq]
                    if half == 0:
                        o_hi = pltpu.roll(o_hi, HEAD_DIM, axis=1)
                    else:
                        o_lo = pltpu.roll(o_lo, HEAD_DIM, axis=1)
                    c0 = kh * hw + j * LANES
                    gate = g_ref[n, :, c0:c0 + LANES]
                    o_ref[n, :, c0:c0 + LANES] = jnp.where(lane < HEAD_DIM, o_lo, o_hi) * _silu(gate)
        return carry

    lax.fori_loop(0, sb, value_body, 0)


def _attn_sample(q8, kn, vn, cache_k, cache_v, g8, sink_rows, n_kv, group, ts):
    nb, tq, aw = q8.shape
    tk = kn.shape[1]
    wb, kvw = cache_k.shape[1], cache_k.shape[2]
    sb = _largest_divisor(nb, (8, 4, 2, 1))
    n_keys = -(-(wb + tk) // (2 * LANES)) * (2 * LANES)
    rows = sb * (n_kv // 2) * 2 * group * tq
    assert sink_rows.shape[0] * sb == rows
    blk3 = lambda r, width: pl.BlockSpec((sb, r, width), lambda i: (i, 0, 0))
    return pl.pallas_call(
        functools.partial(_attn_sample_kernel, n_kv=n_kv, group=group, ts=ts),
        out_shape=jax.ShapeDtypeStruct((nb, tq, aw), F32),
        grid=(nb // sb,),
        in_specs=[
            blk3(tq, aw), blk3(tk, kvw), blk3(tk, kvw), blk3(wb, kvw), blk3(wb, kvw), blk3(tq, aw),
            pl.BlockSpec(sink_rows.shape, lambda i: (0, 0)),
        ],
        out_specs=blk3(tq, aw),
        scratch_shapes=[pltpu.VMEM((rows, n_keys), F32), pltpu.VMEM((rows, n_keys), BF16)],
        compiler_params=_params(("arbitrary",)),
        name="attn_sample",
    )(q8, kn, vn, cache_k, cache_v, g8, sink_rows)


def _branch_kernel(oa_ref, obp_ref, obs_ref, w_ref, ma_ref, mb_ref, o_ref, wb_ref, *, n_p):
    i = pl.program_id(1)

    @pl.when(i == 0)
    def _():
        _cast_rows(w_ref, wb_ref)

    width = oa_ref.shape[1]
    ob = jnp.where(i < n_p, obp_ref[...], obs_ref[...])
    pa = _dot(oa_ref[...], wb_ref[0:width, :])
    pb = _dot(ob, wb_ref[width:, :])
    ma = ma_ref[...].astype(F32)
    mb = mb_ref[...].astype(F32)
    o_ref[...] = (_sigmoid(ma) * pa + _sigmoid(mb) * pb).astype(o_ref.dtype)


def _branch_merge(o_a, o_b_p, o_b_s, w_branch, z, ma_col, tm, tn):
    m, width = o_a.shape
    aw = o_b_p.shape[1]
    d = w_branch.shape[1]
    n_p = o_b_p.shape[0] // tm
    ma_blk = ma_col // tn
    mb_blk = (ma_col + d) // tn
    return pl.pallas_call(
        functools.partial(_branch_kernel, n_p=n_p),
        out_shape=jax.ShapeDtypeStruct((m, d), BF16),
        grid=(d // tn, m // tm),
        in_specs=[
            pl.BlockSpec((tm, width), lambda j, i: (i, 0)),
            pl.BlockSpec((tm, aw), lambda j, i: (jnp.minimum(i, n_p - 1), 0)),
            pl.BlockSpec((tm, aw), lambda j, i: (jnp.maximum(i - n_p, 0), 0)),
            pl.BlockSpec((width + aw, tn), lambda j, i: (0, j)),
            pl.BlockSpec((tm, tn), lambda j, i: (i, ma_blk + j)),
            pl.BlockSpec((tm, tn), lambda j, i: (i, mb_blk + j)),
        ],
        out_specs=pl.BlockSpec((tm, tn), lambda j, i: (i, j)),
        scratch_shapes=[pltpu.VMEM((width + aw, tn), BF16)],
        compiler_params=_params(("arbitrary", "arbitrary")),
        name="branch_merge",
    )(o_a, o_b_p, o_b_s, w_branch, z, z)


def _out_kernel(m_ref, w_ref, g_ref, x_ref, y_ref, acc_ref, ss_ref, *, d):
    i = pl.program_id(0)
    j = pl.program_id(1)
    slot = lax.rem(i, 2)
    prev = 1 - slot

    @pl.when((i == 0) & (j == 0))
    def _():
        acc_ref[1] = jnp.zeros(acc_ref.shape[1:], F32)
        ss_ref[...] = jnp.zeros_like(ss_ref)

    t = _dot(m_ref[...], w_ref[...])
    acc_ref[slot, j] = t
    ssq = jnp.sum(t * t, axis=-1, keepdims=True)
    ss_ref[slot] = jnp.where(j == 0, ssq, ss_ref[slot] + ssq)

    inv = lax.rsqrt(ss_ref[prev] / d + RMS_EPS)
    y_ref[...] = x_ref[...] + (acc_ref[prev, j] * inv) * g_ref[...]


def _out_proj(merged, w_out_bf16, g, x, row0, tm, tn):
    rows, d = x.shape
    n_r = rows // tm
    n_t = d // tn
    blk0 = row0 // tm
    xy_map = lambda i, j: (jnp.maximum(i - 1, 0), jnp.where(i == 0, 0, j))
    return pl.pallas_call(
        functools.partial(_out_kernel, d=d),
        out_shape=jax.ShapeDtypeStruct((rows, d), F32),
        grid=(n_r + 1, n_t),
        in_specs=[
            pl.BlockSpec((tm, d), lambda i, j: (blk0 + jnp.minimum(i, n_r - 1), 0)),
            pl.BlockSpec((d, tn), lambda i, j: (0, jnp.where(i == n_r, n_t - 1, j))),
            pl.BlockSpec((1, tn), lambda i, j: (0, j)),
            pl.BlockSpec((tm, tn), xy_map),
        ],
        out_specs=pl.BlockSpec((tm, tn), xy_map),
        scratch_shapes=[pltpu.VMEM((2, n_t, tm, tn), F32), pltpu.VMEM((2, tm, 1), F32)],
        compiler_params=_params(("arbitrary", "arbitrary")),
        name="out_proj",
    )(merged, w_out_bf16, g, x)


def _pad_rows(x, rows):
    return jnp.pad(x, ((0, 0), (0, rows - x.shape[1]), (0, 0)))


def _layer(xp, xs_tm, conv_state, h_state, cache_k, cache_v, norm_pre, norm_post, w_in, conv_w,
           conv_b, w_a, b_a, w_x, b_x, lam, sinks, w_branch, w_out):
    s, d = xp.shape
    nb, wb, n_kv, _ = cache_k.shape
    ts = xs_tm.shape[0] // nb
    width = conv_w.shape[1]
    n_heads = sinks.shape[0]
    group = n_heads // n_kv
    aw, kvw = n_heads * HEAD_DIM, n_kv * HEAD_DIM
    m = s + nb * ts
    tm = nb * ts
    assert s % tm == 0 and tm % SUBLANES == 0 and ts >= CONV_WIDTH - 1
    q_col = 2 * width
    k_col = q_col + aw
    v_col = k_col + kvw
    g_col = v_col + kvw
    ma_col = g_col + aw
    assert w_in.shape[1] == ma_col + 2 * d

    xn = _rmsnorm_pre(xp, xs_tm, norm_pre.reshape(1, d), _largest_divisor(tm, (256, 128, 64, 8)))
    z = _in_proj(xn, w_in, tm)

    cst_tm = conv_state.transpose(1, 0, 2)
    o_a, h_p, h_s = _lru(z, conv_w, conv_b.reshape(1, width), w_a.astype(BF16),
                         b_a.reshape(1, width), w_x.astype(BF16), b_x.reshape(1, width),
                         lam.reshape(1, width), cst_tm, h_state, s, tm)

    o_b_p = _attn_prompt(z, sinks, s, width, aw, kvw, n_kv, group)

    def sample_cols(c0, c1):
        blk = lax.slice(z, (s, c0), (m, c1)).reshape(ts, nb, c1 - c0)
        return blk.transpose(1, 0, 2).astype(F32)

    q_s, k_s, v_s = sample_cols(q_col, k_col), sample_cols(k_col, v_col), sample_cols(v_col, g_col)
    g_s = sample_cols(g_col, ma_col)
    tq = -(-ts // SUBLANES) * SUBLANES
    tk = -(-ts // BF16_ROWS) * BF16_ROWS
    ck = cache_k.reshape(nb, wb, kvw)
    cv = cache_v.reshape(nb, wb, kvw)
    sink_rows = jnp.broadcast_to(sinks.reshape(n_heads, 1, 1), (n_heads, tq, LANES))
    sink_rows = sink_rows.reshape(n_heads * tq, LANES)
    o_b_s = _attn_sample(_pad_rows(q_s, tq), _pad_rows(k_s, tk), _pad_rows(v_s, tk), ck, cv,
                         _pad_rows(g_s, tq), sink_rows, n_kv, group, ts)
    o_b_s = o_b_s[:, :ts].transpose(1, 0, 2).reshape(nb * ts, aw).astype(BF16)

    tn = _largest_divisor(ma_col, (512, 256, 128))
    assert d % tn == 0
    merged = _branch_merge(o_a, o_b_p, o_b_s, w_branch, z, ma_col, tm, tn)
    w_out_bf16 = w_out.astype(BF16)
    g_post = norm_post.reshape(1, d)
    y_p = _out_proj(merged, w_out_bf16, g_post, xp, 0, tm, tn)
    y_s = _out_proj(merged, w_out_bf16, g_post, xs_tm, s, tm, tn)

    keep = CONV_WIDTH - 1
    new_conv_p = lax.slice(z, (s - keep, 0), (s, width)).astype(F32)[None]
    wbp = min(WINDOW, s)
    new_k_p = lax.slice(z, (s - wbp, k_col), (s, v_col)).astype(F32).reshape(1, wbp, n_kv, HEAD_DIM)
    new_v_p = lax.slice(z, (s - wbp, v_col), (s, g_col)).astype(F32).reshape(1, wbp, n_kv, HEAD_DIM)
    x_lru_s = sample_cols(0, width)
    new_conv_s = jnp.concatenate([conv_state, x_lru_s], axis=1)[:, -keep:]
    new_k_s = jnp.concatenate([ck, k_s], axis=1)[:, -wb:].reshape(nb, wb, n_kv, HEAD_DIM)
    new_v_s = jnp.concatenate([cv, v_s], axis=1)[:, -wb:].reshape(nb, wb, n_kv, HEAD_DIM)
    return y_p, y_s, (new_conv_p, h_p, new_k_p, new_v_p), (new_conv_s, h_s, new_k_s, new_v_s)


def kernel(x_prompt, x_sample, state_conv, state_h, cache_k_win, cache_v_win, norm_pre, norm_post, w_in, conv_w, conv_b, lru_w_a, lru_b_a, lru_w_x, lru_b_x, lru_lambda, attn_sinks, w_branch, w_out):
    batch, s, d = x_prompt.shape
    nb, ts, _ = x_sample.shape
    assert batch == 1, "the prompt group is a single sequence"
    depth = w_in.shape[0]
    xp = x_prompt.reshape(s, d)
    xs_tm = x_sample.transpose(1, 0, 2).reshape(ts * nb, d)
    p_states, s_states = [], []
    for l in range(depth):
        xp, xs_tm, p_new, s_new = _layer(
            xp, xs_tm, state_conv[l], state_h[l], cache_k_win[l], cache_v_win[l], norm_pre[l],
            norm_post[l], w_in[l], conv_w[l], conv_b[l], lru_w_a[l], lru_b_a[l], lru_w_x[l],
            lru_b_x[l], lru_lambda[l], attn_sinks[l], w_branch[l], w_out[l])
        p_states.append(p_new)
        s_states.append(s_new)
    y_prompt = xp.reshape(1, s, d)
    y_sample = xs_tm.reshape(ts, nb, d).transpose(1, 0, 2)
    stack = lambda states, k: jnp.stack([st[k] for st in states])
    return (y_prompt, y_sample,
            stack(p_states, 0), stack(p_states, 1), stack(p_states, 2), stack(p_states, 3),
            stack(s_states, 0), stack(s_states, 1), stack(s_states, 2), stack(s_states, 3))
```

```python
import functools

import jax
import jax.numpy as jnp
from jax import lax
from jax.experimental import pallas as pl
from jax.experimental.pallas import tpu as pltpu

HEAD_DIM = 64
WINDOW = 128
CONV_WIDTH = 4
LRU_C = 8.0
RMS_EPS = 1e-6
NEG_INF = -1e30

LANES = 128
SUBLANES = 8
BF16_ROWS = 16
VMEM_LIMIT_BYTES = 58 * 1024 * 1024

F32 = jnp.float32
BF16 = jnp.bfloat16


def _largest_divisor(n, candidates):
    for c in candidates:
        if n % c == 0:
            return c
    raise ValueError(f"no tile in {candidates} divides {n}")


def _params(sem):
    return pltpu.CompilerParams(dimension_semantics=sem, vmem_limit_bytes=VMEM_LIMIT_BYTES)


def _dot(a, b):
    return jnp.dot(a, b, preferred_element_type=F32)


def _dot_nt(a, b):
    return lax.dot_general(a, b, (((1,), (1,)), ((), ())), preferred_element_type=F32)


def _sigmoid(x):
    return jax.nn.sigmoid(x)


def _silu(x):
    return x * jax.nn.sigmoid(x)


def _rmsnorm_kernel(xp_ref, xs_ref, g_ref, o_ref, *, n_p):
    i = pl.program_id(0)

    def body(x_ref):
        x = x_ref[...]
        ms = jnp.mean(x * x, axis=-1, keepdims=True)
        o_ref[...] = ((x * lax.rsqrt(ms + RMS_EPS)) * g_ref[...]).astype(o_ref.dtype)

    @pl.when(i < n_p)
    def _():
        body(xp_ref)

    @pl.when(i >= n_p)
    def _():
        body(xs_ref)


def _rmsnorm_pre(xp, xs, g, tm):
    s, d = xp.shape
    ms = xs.shape[0]
    n_p, n_s = s // tm, ms // tm
    return pl.pallas_call(
        functools.partial(_rmsnorm_kernel, n_p=n_p),
        out_shape=jax.ShapeDtypeStruct((s + ms, d), BF16),
        grid=(n_p + n_s,),
        in_specs=[
            pl.BlockSpec((tm, d), lambda i: (jnp.minimum(i, n_p - 1), 0)),
            pl.BlockSpec((tm, d), lambda i: (jnp.maximum(i - n_p, 0), 0)),
            pl.BlockSpec((1, d), lambda i: (0, 0)),
        ],
        out_specs=pl.BlockSpec((tm, d), lambda i: (i, 0)),
        compiler_params=_params(("arbitrary",)),
        name="rmsnorm_pre",
    )(xp, xs, g)


def _cast_rows(src_ref, dst_ref, rows_per_step=256):
    n = src_ref.shape[0]
    step = _largest_divisor(n, (rows_per_step, 128, 64, 32, 16))

    def body(r, carry):
        r0 = pl.multiple_of(r * step, step)
        dst_ref[pl.ds(r0, step), :] = src_ref[pl.ds(r0, step), :].astype(dst_ref.dtype)
        return carry

    lax.fori_loop(0, n // step, body, 0)


def _in_proj_kernel(x_ref, w_ref, o_ref, wb_ref):
    @pl.when(pl.program_id(1) == 0)
    def _():
        _cast_rows(w_ref, wb_ref)

    o_ref[...] = _dot(x_ref[...], wb_ref[...]).astype(o_ref.dtype)


def _in_proj(xn, w_in, tm):
    m, d = xn.shape
    n = w_in.shape[1]
    tn = _largest_divisor(n, (1024, 512, 256, 128))
    return pl.pallas_call(
        _in_proj_kernel,
        out_shape=jax.ShapeDtypeStruct((m, n), BF16),
        grid=(n // tn, m // tm),
        in_specs=[
            pl.BlockSpec((tm, d), lambda j, i: (i, 0)),
            pl.BlockSpec((d, tn), lambda j, i: (0, j)),
        ],
        out_specs=pl.BlockSpec((tm, tn), lambda j, i: (i, j)),
        scratch_shapes=[pltpu.VMEM((d, tn), BF16)],
        compiler_params=_params(("arbitrary", "arbitrary")),
        name="in_proj",
    )(xn, w_in)


def _neg_c_softplus(lam):
    return -LRU_C * (jnp.maximum(-lam, 0.0) + jnp.log(1.0 + jnp.exp(-jnp.abs(lam))))


def _lru_conv(taps, w, bias):
    out = bias
    for k in range(CONV_WIDTH):
        out = out + taps[k] * w[k:k + 1, :]
    return out


def _lru_gate_dots(xc, wa_ref, wx_ref):
    blk = wa_ref.shape[1]
    xcb = xc.astype(BF16)
    ga, gx = [], []
    for q in range(xc.shape[1] // blk):
        xq = xcb[:, q * blk:(q + 1) * blk]
        ga.append(_dot(xq, wa_ref[q]))
        gx.append(_dot(xq, wx_ref[q]))
    return jnp.concatenate(ga, axis=1), jnp.concatenate(gx, axis=1)


def _lru_coeffs(xc, ga, gx, b_a, b_x, ncs):
    r_a = _sigmoid(ga + b_a)
    r_x = _sigmoid(gx + b_x)
    a = jnp.exp(ncs * r_a)
    b = jnp.sqrt(1.0 - a * a) * (r_x * xc)
    return a, b


def _scan_tile(a, b, sub):
    shift = 1
    while shift < SUBLANES:
        keep = sub >= shift
        a_sh = jnp.where(keep, pltpu.roll(a, shift, axis=0), 1.0)
        b_sh = jnp.where(keep, pltpu.roll(b, shift, axis=0), 0.0)
        b = b + a * b_sh
        a = a * a_sh
        shift *= 2
    return a, b


def _lru_prompt_conv(xcur, x_tail, sub, w, bias):
    n_tiles = xcur.shape[0] // SUBLANES
    tiles = [x_tail] + [xcur[v * SUBLANES:(v + 1) * SUBLANES] for v in range(n_tiles)]
    taps = []
    for k in range(CONV_WIDTH):
        back = CONV_WIDTH - 1 - k
        if back == 0:
            taps.append(xcur)
            continue
        rolled = [pltpu.roll(t, back, axis=0) for t in tiles]
        taps.append(jnp.concatenate(
            [jnp.where(sub >= back, rolled[v + 1], rolled[v]) for v in range(n_tiles)], axis=0))
    return _lru_conv(taps, w, bias), tiles[-1]


def _lru_prompt_scan(a, b, h_in, sub):
    hs = []
    for v in range(a.shape[0] // SUBLANES):
        rows = slice(v * SUBLANES, (v + 1) * SUBLANES)
        a_in, b_in = _scan_tile(a[rows], b[rows], sub)
        hv = a_in * h_in + b_in
        hs.append(hv)
        h_in = jnp.broadcast_to(hv[SUBLANES - 1:SUBLANES, :], hv.shape)
    return jnp.concatenate(hs, axis=0), h_in


def _lru_sample_kernel(x_ref, g_ref, cw_ref, cb_ref, wa_ref, ba_ref, wx_ref, bx_ref, lam_ref,
                       cst_ref, h0_ref, o_ref, hs_ref):
    nb = h0_ref.shape[0]
    ts = x_ref.shape[0] // nb
    w, bias = cw_ref[...], cb_ref[...]
    ncs = _neg_c_softplus(lam_ref[...])
    xs = [cst_ref[k] for k in range(CONV_WIDTH - 1)]
    xs += [x_ref[t * nb:(t + 1) * nb, :].astype(F32) for t in range(ts)]
    h = h0_ref[...]
    for t in range(ts):
        xc = _lru_conv(xs[t:t + CONV_WIDTH], w, bias)
        ga, gx = _lru_gate_dots(xc, wa_ref, wx_ref)
        a, b = _lru_coeffs(xc, ga, gx, ba_ref[...], bx_ref[...], ncs)
        h = a * h + b
        g = g_ref[t * nb:(t + 1) * nb, :].astype(F32)
        o_ref[t * nb:(t + 1) * nb, :] = (h * _silu(g)).astype(o_ref.dtype)
    hs_ref[...] = h


def _lru_sample(z, lru_params, cst_tm, h0, s, tb, cbw):
    conv_w, conv_b, w_a, b_a, w_x, b_x, lam = lru_params
    width = conv_w.shape[1]
    nb = h0.shape[0]
    blk = w_a.shape[1]
    n_c = width // cbw
    blk_s = s // tb
    row = lambda c: (0, c)
    gate_w = pl.BlockSpec((cbw // blk, blk, blk), lambda c: (c, 0, 0))
    return pl.pallas_call(
        _lru_sample_kernel,
        out_shape=(jax.ShapeDtypeStruct((tb, width), BF16), jax.ShapeDtypeStruct((nb, width), F32)),
        grid=(n_c,),
        in_specs=[
            pl.BlockSpec((tb, cbw), lambda c: (blk_s, c)),
            pl.BlockSpec((tb, cbw), lambda c: (blk_s, n_c + c)),
            pl.BlockSpec((CONV_WIDTH, cbw), row),
            pl.BlockSpec((1, cbw), row),
            gate_w,
            pl.BlockSpec((1, cbw), row),
            gate_w,
            pl.BlockSpec((1, cbw), row),
            pl.BlockSpec((1, cbw), row),
            pl.BlockSpec((CONV_WIDTH - 1, nb, cbw), lambda c: (0, 0, c)),
            pl.BlockSpec((nb, cbw), row),
        ],
        out_specs=(pl.BlockSpec((tb, cbw), row), pl.BlockSpec((nb, cbw), row)),
        compiler_params=_params(("arbitrary",)),
        name="lru_sample",
    )(z, z, conv_w, conv_b, w_a, b_a, w_x, b_x, lam, cst_tm, h0)


def _attn_prompt_kernel(sink_ref, q_ref, kp_ref, kc_ref, vp_ref, vc_ref, g0_ref, g1_ref, os_ref,
                        o_ref, *, n_q, n_kv, group):
    i = pl.program_id(0)

    @pl.when(i < n_q)
    def _():
        _attn_prompt_block(i, sink_ref, q_ref, kp_ref, kc_ref, vp_ref, vc_ref, g0_ref, g1_ref,
                           o_ref, n_kv, group)

    @pl.when(i >= n_q)
    def _():
        o_ref[...] = os_ref[...]


def _attn_prompt_block(i, sink_ref, q_ref, kp_ref, kc_ref, vp_ref, vc_ref, g0_ref, g1_ref, o_ref,
                       n_kv, group):
    qb = q_ref.shape[0]
    aw_half = g0_ref.shape[1]
    scale = HEAD_DIM ** -0.5
    hw = group * HEAD_DIM
    c = lax.broadcasted_iota(jnp.int32, (2 * qb, qb), 0)
    r = lax.broadcasted_iota(jnp.int32, (2 * qb, qb), 1)
    rel = qb + r - c
    mask = (rel >= 0) & (rel <= WINDOW) & ((c >= qb) | (i > 0))
    lane = lax.broadcasted_iota(jnp.int32, (qb, LANES), 1)
    half_mask = (lane < HEAD_DIM, lane >= HEAD_DIM)
    zero = jnp.zeros((qb, LANES), BF16)
    for p in range(n_kv // 2):
        lanes = slice(p * LANES, (p + 1) * LANES)
        k_f32 = jnp.concatenate([kp_ref[:, lanes], kc_ref[:, lanes]], axis=0).astype(F32) * scale
        k_variants = (k_f32.astype(BF16), pltpu.roll(k_f32, HEAD_DIM, axis=1).astype(BF16))
        v_pair = jnp.concatenate([vp_ref[:, lanes], vc_ref[:, lanes]], axis=0)
        v_t = v_pair.astype(F32).T.astype(BF16)
        heads = ([], [])
        for half in range(2):
            for g in range(group):
                heads[0 if (g % 2) == half else 1].append((2 * p + half, g))
        out_t = {}
        for variant in range(2):
            q_rows = []
            for kh, g in heads[variant]:
                c0 = kh * hw + (g // 2) * LANES
                q_rows.append(jnp.where(half_mask[g % 2], q_ref[:, c0:c0 + LANES], zero))
            s_t = _dot_nt(k_variants[variant], jnp.concatenate(q_rows, axis=0))
            p_blocks, inv_den = [], []
            for b, (kh, g) in enumerate(heads[variant]):
                sink = sink_ref[kh * group + g]
                sb = jnp.where(mask, s_t[:, b * qb:(b + 1) * qb], NEG_INF)
                m = jnp.maximum(jnp.max(sb, axis=0, keepdims=True), sink)
                pe = jnp.exp(sb - m)
                den = jnp.sum(pe, axis=0, keepdims=True) + jnp.exp(sink - m)
                p_blocks.append(pe.astype(BF16))
                inv_den.append(1.0 / den)
            o_t = _dot(v_t, jnp.concatenate(p_blocks, axis=1))
            for b, (kh, g) in enumerate(heads[variant]):
                r0 = (kh % 2) * HEAD_DIM
                out_t[(kh, g)] = o_t[r0:r0 + HEAD_DIM, b * qb:(b + 1) * qb] * inv_den[b]
        for half in range(2):
            kh = 2 * p + half
            for j in range(group // 2):
                c0 = kh * hw + j * LANES
                tile = jnp.concatenate([out_t[(kh, 2 * j)], out_t[(kh, 2 * j + 1)]], axis=0).T
                g_ref, gc = (g0_ref, c0) if c0 < aw_half else (g1_ref, c0 - aw_half)
                gate = g_ref[:, gc:gc + LANES].astype(F32)
                o_ref[:, c0:c0 + LANES] = (tile * _silu(gate)).astype(o_ref.dtype)


def _attn_prompt(z, sinks, o_b_s, s, width, aw, kvw, n_kv, group):
    qb = WINDOW
    n_q = s // qb
    assert s % qb == 0 and o_b_s.shape[0] % qb == 0 and n_kv % 2 == 0 and group % 2 == 0
    q_blk = (2 * width) // aw
    k_blk = (2 * width + aw) // kvw
    v_blk = k_blk + 1
    g_blk = (2 * width + aw + 2 * kvw) // (aw // 2)
    assert (2 * width) % aw == 0 and (2 * width + aw) % kvw == 0
    assert (2 * width + aw + 2 * kvw) % (aw // 2) == 0
    cur = lambda i: jnp.minimum(i, n_q - 1)
    prev = lambda i: jnp.maximum(cur(i) - 1, 0)
    return pl.pallas_call(
        functools.partial(_attn_prompt_kernel, n_q=n_q, n_kv=n_kv, group=group),
        out_shape=jax.ShapeDtypeStruct((s + o_b_s.shape[0], aw), BF16),
        grid=(n_q + o_b_s.shape[0] // qb,),
        in_specs=[
            pl.BlockSpec(memory_space=pltpu.SMEM),
            pl.BlockSpec((qb, aw), lambda i: (cur(i), q_blk)),
            pl.BlockSpec((qb, kvw), lambda i: (prev(i), k_blk)),
            pl.BlockSpec((qb, kvw), lambda i: (cur(i), k_blk)),
            pl.BlockSpec((qb, kvw), lambda i: (prev(i), v_blk)),
            pl.BlockSpec((qb, kvw), lambda i: (cur(i), v_blk)),
            pl.BlockSpec((qb, aw // 2), lambda i: (cur(i), g_blk)),
            pl.BlockSpec((qb, aw // 2), lambda i: (cur(i), g_blk + 1)),
            pl.BlockSpec((qb, aw), lambda i: (jnp.maximum(i - n_q, 0), 0)),
        ],
        out_specs=pl.BlockSpec((qb, aw), lambda i: (i, 0)),
        compiler_params=_params(("arbitrary",)),
        name="attn_prompt",
    )(sinks, z, z, z, z, z, z, z, o_b_s)


def _attn_sample_kernel(q_ref, kn_ref, vn_ref, ck_ref, cv_ref, g_ref, sink_ref, o_ref,
                        s_ref, p_ref, *, n_kv, group, ts):
    sb, tq, _ = q_ref.shape
    wb = ck_ref.shape[1]
    tk = kn_ref.shape[1]
    n_keys = s_ref.shape[1]
    n_pairs = n_kv // 2
    pair_rows = 2 * group * tq
    seq_rows = n_pairs * pair_rows
    scale = HEAD_DIM ** -0.5
    hw = group * HEAD_DIM
    lane = lax.broadcasted_iota(jnp.int32, (tq, LANES), 1)
    key_pad = jnp.zeros((n_keys - wb - tk, LANES), F32)

    def keys_of(c_ref, n_ref, n, lanes):
        return jnp.concatenate([c_ref[n, :, lanes], n_ref[n, :, lanes], key_pad], axis=0).astype(BF16)

    def score_body(n, carry):
        for p in range(n_pairs):
            lanes = slice(p * LANES, (p + 1) * LANES)
            pieces = []
            for half in range(2):
                for g in range(group):
                    c0 = (2 * p + half) * hw + (g // 2) * LANES
                    tile = q_ref[n, :, c0:c0 + LANES] * scale
                    if (g % 2) != half:
                        tile = pltpu.roll(tile, HEAD_DIM, axis=1)
                    keep = (lane >= HEAD_DIM * half) & (lane < HEAD_DIM * (half + 1))
                    pieces.append(jnp.where(keep, tile, 0.0))
            lhs = jnp.concatenate(pieces, axis=0).astype(BF16)
            r0 = pl.multiple_of(n * seq_rows + p * pair_rows, pair_rows)
            s_ref[pl.ds(r0, pair_rows), :] = _dot_nt(lhs, keys_of(ck_ref, kn_ref, n, lanes))
        return carry

    lax.fori_loop(0, sb, score_body, 0)

    t = lax.broadcasted_iota(jnp.int32, (tq, n_keys), 0)
    c = lax.broadcasted_iota(jnp.int32, (tq, n_keys), 1)
    mask8 = ((c < wb) & (t + wb - c <= WINDOW)) | ((c >= wb) & (c - wb <= t) & (c - wb < ts))
    reps = sb * seq_rows // tq
    mask = jnp.concatenate([mask8] * reps, axis=0)
    sink = jnp.concatenate([sink_ref[:, 0:1]] * sb, axis=0)
    s = jnp.where(mask, s_ref[...], NEG_INF)
    m = jnp.maximum(jnp.max(s, axis=-1, keepdims=True), sink)
    pe = jnp.exp(s - m)
    den = jnp.sum(pe, axis=-1, keepdims=True) + jnp.exp(sink - m)
    p_ref[...] = (pe / den).astype(p_ref.dtype)

    def value_body(n, carry):
        for p in range(n_pairs):
            lanes = slice(p * LANES, (p + 1) * LANES)
            r0 = pl.multiple_of(n * seq_rows + p * pair_rows, pair_rows)
            o = _dot(p_ref[pl.ds(r0, pair_rows), :], keys_of(cv_ref, vn_ref, n, lanes))
            for half in range(2):
                kh = 2 * p + half
                for j in range(group // 2):
                    rows = (half * group + 2 * j) * tq
                    o_lo, o_hi = o[rows:rows + tq], o[rows + tq:rows + 2 * tq]
                    if half == 0:
                        o_hi = pltpu.roll(o_hi, HEAD_DIM, axis=1)
                    else:
                        o_lo = pltpu.roll(o_lo, HEAD_DIM, axis=1)
                    c0 = kh * hw + j * LANES
                    gate = g_ref[n, :, c0:c0 + LANES]
                    o_ref[n, :, c0:c0 + LANES] = jnp.where(lane < HEAD_DIM, o_lo, o_hi) * _silu(gate)
        return carry

    lax.fori_loop(0, sb, value_body, 0)


def _attn_sample(q8, kn, vn, cache_k, cache_v, g8, sink_rows, n_kv, group, ts):
    nb, tq, aw = q8.shape
    tk = kn.shape[1]
    wb, kvw = cache_k.shape[1], cache_k.shape[2]
    sb = _largest_divisor(nb, (8, 4, 2, 1))
    n_keys = -(-(wb + tk) // (2 * LANES)) * (2 * LANES)
    rows = sb * (n_kv // 2) * 2 * group * tq
    assert sink_rows.shape[0] * sb == rows
    blk3 = lambda r, width: pl.BlockSpec((sb, r, width), lambda i: (i, 0, 0))
    return pl.pallas_call(
        functools.partial(_attn_sample_kernel, n_kv=n_kv, group=group, ts=ts),
        out_shape=jax.ShapeDtypeStruct((nb, tq, aw), F32),
        grid=(nb // sb,),
        in_specs=[
            blk3(tq, aw), blk3(tk, kvw), blk3(tk, kvw), blk3(wb, kvw), blk3(wb, kvw), blk3(tq, aw),
            pl.BlockSpec(sink_rows.shape, lambda i: (0, 0)),
        ],
        out_specs=blk3(tq, aw),
        scratch_shapes=[pltpu.VMEM((rows, n_keys), F32), pltpu.VMEM((rows, n_keys), BF16)],
        compiler_params=_params(("arbitrary",)),
        name="attn_sample",
    )(q8, kn, vn, cache_k, cache_v, g8, sink_rows)


def _branch_lru_kernel(x_ref, g_ref, cw_ref, cb_ref, wa_ref, wx_ref, ba_ref, bx_ref, lam_ref,
                       oas_ref, ob_ref, w_ref, ma_ref, mb_ref,
                       o_ref, hp_ref, oa_ref, xt_ref, hc_ref, xc_ref, ga_ref, gx_ref,
                       *, n_p, n_t, chunk):
    u = pl.program_id(0)
    tm, cbw = x_ref.shape

    @pl.when(u == 0)
    def _():
        for ref in (oa_ref, xt_ref, hc_ref, xc_ref, ga_ref, gx_ref):
            ref[...] = jnp.zeros_like(ref)

    unit_t = jnp.maximum(u - 1, 0)
    row_t, ch_t = lax.div(unit_t, n_t), lax.rem(unit_t, n_t)
    live_t = (u >= 1) & (unit_t < n_p * n_t)
    row_d = lax.div(jnp.maximum(u - 1 - n_t, 0), n_t)
    ch_c = lax.rem(u, n_t)
    live_c = u < n_p * n_t

    slot_d = lax.rem(row_d, 2)
    width = n_t * cbw
    tn = o_ref.shape[1]
    n_chunks = tm // chunk
    pieces = [(rh, ch) for rh in range(2) for ch in range(2)]
    chunks_per_piece = n_chunks // len(pieces)
    assert chunks_per_piece * len(pieces) == n_chunks

    def product_piece(rh, ch):
        rows = slice(rh * tm // 2, (rh + 1) * tm // 2)
        cols = slice(ch * tn // 2, (ch + 1) * tn // 2)
        o_a = jnp.concatenate([oa_ref[slot_d, q, rows, :] for q in range(n_t)], axis=1)
        pa = _dot(o_a, w_ref[0:width, cols])
        pb = _dot(ob_ref[rows, :], w_ref[width:, cols])
        ma = ma_ref[rows, cols].astype(F32)
        mb = mb_ref[rows, cols].astype(F32)
        o_ref[rows, cols] = (_sigmoid(ma) * pa + _sigmoid(mb) * pb).astype(o_ref.dtype)

    sub = lax.broadcasted_iota(jnp.int32, (SUBLANES, cbw), 0)
    stage_w = lax.rem(u, 2)
    stage_r = 1 - stage_w

    slot_t = lax.rem(row_t, 2)
    b_a, b_x = ba_ref[...], bx_ref[...]
    ncs = _neg_c_softplus(lam_ref[...])
    h_in = hc_ref[ch_t]
    for c in range(n_chunks):
        if c % chunks_per_piece == 0:
            product_piece(*pieces[c // chunks_per_piece])
        rows = slice(c * chunk, (c + 1) * chunk)
        a, b = _lru_coeffs(xc_ref[stage_r, rows, :], ga_ref[stage_r, rows, :],
                           gx_ref[stage_r, rows, :], b_a, b_x, ncs)
        h, h_in = _lru_prompt_scan(a, b, h_in, sub)
        o_a_new = (h * _silu(g_ref[rows, :].astype(F32))).astype(oa_ref.dtype)
        oa_ref[slot_t, ch_t, rows, :] = jnp.where(live_t, o_a_new, oas_ref[rows, :])
    h_in = jnp.where(live_t, h_in, hc_ref[ch_t])
    hc_ref[ch_t] = h_in
    hp_ref[ch_t] = h_in[0:1, :]

    w, bias = cw_ref[...], cb_ref[...]
    x_tail = xt_ref[ch_c]
    xcs = []
    for c in range(n_chunks):
        rows = slice(c * chunk, (c + 1) * chunk)
        xc, x_tail = _lru_prompt_conv(x_ref[rows, :].astype(F32), x_tail, sub, w, bias)
        xcs.append(xc)
    xt_ref[ch_c] = jnp.where(live_c, x_tail, xt_ref[ch_c])
    xc_all = jnp.concatenate(xcs, axis=0)
    ga, gx = _lru_gate_dots(xc_all, wa_ref, wx_ref)
    xc_ref[stage_w] = xc_all
    ga_ref[stage_w] = ga
    gx_ref[stage_w] = gx


def _branch_lru(z, lru_params, o_a_s, o_b, w_branch_bf16, ma_col, s, tm, tn):
    conv_w, conv_b, w_a, b_a, w_x, b_x, lam = lru_params
    m = z.shape[0]
    width = conv_w.shape[1]
    blk = w_a.shape[1]
    aw = o_b.shape[1]
    d = w_branch_bf16.shape[1]
    n_p = s // tm
    n_r = m // tm
    n_t = d // tn
    cbw = width // n_t
    assert cbw % blk == 0 and cbw % LANES == 0
    chunk = tm // 8
    assert chunk % SUBLANES == 0
    ma_blk = ma_col // tn
    mb_blk = (ma_col + d) // tn
    row_c = lambda u: jnp.minimum(u // n_t, n_p - 1)
    ch_c = lambda u: u % n_t
    unit_t = lambda u: jnp.maximum(u - 1, 0)
    row_t = lambda u: jnp.minimum(unit_t(u) // n_t, n_p - 1)
    ch_t = lambda u: unit_t(u) % n_t
    unit_d = lambda u: jnp.maximum(u - 1 - n_t, 0)
    row_d = lambda u: unit_d(u) // n_t
    tile_d = lambda u: unit_d(u) % n_t
    gate_w = pl.BlockSpec((cbw // blk, blk, blk), lambda u: (ch_c(u), 0, 0))
    chan_t = lambda rows: pl.BlockSpec((rows, cbw), lambda u: (0, ch_t(u)))
    return pl.pallas_call(
        functools.partial(_branch_lru_kernel, n_p=n_p, n_t=n_t, chunk=chunk),
        out_shape=(jax.ShapeDtypeStruct((m, d), BF16), jax.ShapeDtypeStruct((n_t, 1, cbw), F32)),
        grid=((n_r + 1) * n_t + 1,),
        in_specs=[
            pl.BlockSpec((tm, cbw), lambda u: (row_c(u), ch_c(u))),
            pl.BlockSpec((tm, cbw), lambda u: (row_t(u), n_t + ch_t(u))),
            pl.BlockSpec((CONV_WIDTH, cbw), lambda u: (0, ch_c(u))),
            pl.BlockSpec((1, cbw), lambda u: (0, ch_c(u))),
            gate_w,
            gate_w,
            chan_t(1),
            chan_t(1),
            chan_t(1),
            chan_t(tm),
            pl.BlockSpec((tm, aw), lambda u: (row_d(u), 0)),
            pl.BlockSpec((width + aw, tn), lambda u: (0, tile_d(u))),
            pl.BlockSpec((tm, tn), lambda u: (row_d(u), ma_blk + tile_d(u))),
            pl.BlockSpec((tm, tn), lambda u: (row_d(u), mb_blk + tile_d(u))),
        ],
        out_specs=(pl.BlockSpec((tm, tn), lambda u: (row_d(u), tile_d(u))),
                   pl.BlockSpec((n_t, 1, cbw), lambda u: (0, 0, 0))),
        scratch_shapes=[
            pltpu.VMEM((2, n_t, tm, cbw), BF16),
            pltpu.VMEM((n_t, SUBLANES, cbw), F32),
            pltpu.VMEM((n_t, SUBLANES, cbw), F32),
            pltpu.VMEM((2, tm, cbw), F32),
            pltpu.VMEM((2, tm, cbw), F32),
            pltpu.VMEM((2, tm, cbw), F32),
        ],
        compiler_params=_params(("arbitrary",)),
        name="branch_lru",
    )(z, z, conv_w, conv_b, w_a, w_x, b_a, b_x, lam, o_a_s, o_b, w_branch_bf16, z, z)


def _out_kernel(m_ref, w_ref, g_ref, x_ref, y_ref, acc_ref, ss_ref, *, d):
    i = pl.program_id(0)
    j = pl.program_id(1)
    slot = lax.rem(i, 2)
    prev = 1 - slot

    @pl.when((i == 0) & (j == 0))
    def _():
        acc_ref[1] = jnp.zeros(acc_ref.shape[1:], F32)
        ss_ref[...] = jnp.zeros_like(ss_ref)

    t = _dot(m_ref[...], w_ref[...])
    acc_ref[slot, j] = t
    ssq = jnp.sum(t * t, axis=-1, keepdims=True)
    ss_ref[slot] = jnp.where(j == 0, ssq, ss_ref[slot] + ssq)

    inv = lax.rsqrt(ss_ref[prev] / d + RMS_EPS)
    y_ref[...] = x_ref[...] + (acc_ref[prev, j] * inv) * g_ref[...]


def _out_proj(merged, w_out_bf16, g, x, row0, tm, tn):
    rows, d = x.shape
    n_r = rows // tm
    n_t = d // tn
    blk0 = row0 // tm
    xy_map = lambda i, j: (jnp.maximum(i - 1, 0), jnp.where(i == 0, 0, j))
    return pl.pallas_call(
        functools.partial(_out_kernel, d=d),
        out_shape=jax.ShapeDtypeStruct((rows, d), F32),
        grid=(n_r + 1, n_t),
        in_specs=[
            pl.BlockSpec((tm, d), lambda i, j: (blk0 + jnp.minimum(i, n_r - 1), 0)),
            pl.BlockSpec((d, tn), lambda i, j: (0, jnp.where(i == n_r, n_t - 1, j))),
            pl.BlockSpec((1, tn), lambda i, j: (0, j)),
            pl.BlockSpec((tm, tn), xy_map),
        ],
        out_specs=pl.BlockSpec((tm, tn), xy_map),
        scratch_shapes=[pltpu.VMEM((2, n_t, tm, tn), F32), pltpu.VMEM((2, tm, 1), F32)],
        compiler_params=_params(("arbitrary", "arbitrary")),
        name="out_proj",
    )(merged, w_out_bf16, g, x)


def _pad_rows(x, rows):
    return jnp.pad(x, ((0, 0), (0, rows - x.shape[1]), (0, 0)))


def _layer(xp, xs_tm, conv_state, h_state, cache_k, cache_v, norm_pre, norm_post, w_in, conv_w,
           conv_b, w_a, b_a, w_x, b_x, lam, sinks, w_branch, w_out):
    s, d = xp.shape
    nb, wb, n_kv, _ = cache_k.shape
    ts = xs_tm.shape[0] // nb
    width = conv_w.shape[1]
    n_heads = sinks.shape[0]
    group = n_heads // n_kv
    aw, kvw = n_heads * HEAD_DIM, n_kv * HEAD_DIM
    m = s + nb * ts
    tm = nb * ts
    assert s % tm == 0 and tm % SUBLANES == 0 and ts >= CONV_WIDTH - 1
    q_col = 2 * width
    k_col = q_col + aw
    v_col = k_col + kvw
    g_col = v_col + kvw
    ma_col = g_col + aw
    assert w_in.shape[1] == ma_col + 2 * d

    xn = _rmsnorm_pre(xp, xs_tm, norm_pre.reshape(1, d), _largest_divisor(tm, (256, 128, 64, 8)))
    z = _in_proj(xn, w_in, tm)

    tn = _largest_divisor(ma_col, (512, 256, 128))
    assert d % tn == 0 and width % (d // tn) == 0
    lru_params = (conv_w, conv_b.reshape(1, width), w_a.astype(BF16), b_a.reshape(1, width),
                  w_x.astype(BF16), b_x.reshape(1, width), lam.reshape(1, width))
    o_a_s, h_s = _lru_sample(z, lru_params, conv_state.transpose(1, 0, 2), h_state, s, tm,
                             width // (d // tn))

    def sample_cols(c0, c1):
        blk = lax.slice(z, (s, c0), (m, c1)).reshape(ts, nb, c1 - c0)
        return blk.transpose(1, 0, 2).astype(F32)

    q_s, k_s, v_s = sample_cols(q_col, k_col), sample_cols(k_col, v_col), sample_cols(v_col, g_col)
    g_s = sample_cols(g_col, ma_col)
    tq = -(-ts // SUBLANES) * SUBLANES
    tk = -(-ts // BF16_ROWS) * BF16_ROWS
    ck = cache_k.reshape(nb, wb, kvw)
    cv = cache_v.reshape(nb, wb, kvw)
    sink_rows = jnp.broadcast_to(sinks.reshape(n_heads, 1, 1), (n_heads, tq, LANES))
    sink_rows = sink_rows.reshape(n_heads * tq, LANES)
    o_b_s = _attn_sample(_pad_rows(q_s, tq), _pad_rows(k_s, tk), _pad_rows(v_s, tk), ck, cv,
                         _pad_rows(g_s, tq), sink_rows, n_kv, group, ts)
    o_b_s = o_b_s[:, :ts].transpose(1, 0, 2).reshape(nb * ts, aw).astype(BF16)
    o_b = _attn_prompt(z, sinks, o_b_s, s, width, aw, kvw, n_kv, group)

    merged, h_p = _branch_lru(z, lru_params, o_a_s, o_b, w_branch.astype(BF16), ma_col, s, tm, tn)
    h_p = h_p.reshape(1, width)
    w_out_bf16 = w_out.astype(BF16)
    g_post = norm_post.reshape(1, d)
    y_p = _out_proj(merged, w_out_bf16, g_post, xp, 0, tm, tn)
    y_s = _out_proj(merged, w_out_bf16, g_post, xs_tm, s, tm, tn)

    keep = CONV_WIDTH - 1
    new_conv_p = lax.slice(z, (s - keep, 0), (s, width)).astype(F32)[None]
    wbp = min(WINDOW, s)
    new_k_p = lax.slice(z, (s - wbp, k_col), (s, v_col)).astype(F32).reshape(1, wbp, n_kv, HEAD_DIM)
    new_v_p = lax.slice(z, (s - wbp, v_col), (s, g_col)).astype(F32).reshape(1, wbp, n_kv, HEAD_DIM)
    x_lru_s = sample_cols(0, width)
    new_conv_s = jnp.concatenate([conv_state, x_lru_s], axis=1)[:, -keep:]
    new_k_s = jnp.concatenate([ck, k_s], axis=1)[:, -wb:].reshape(nb, wb, n_kv, HEAD_DIM)
    new_v_s = jnp.concatenate([cv, v_s], axis=1)[:, -wb:].reshape(nb, wb, n_kv, HEAD_DIM)
    return y_p, y_s, (new_conv_p, h_p, new_k_p, new_v_p), (new_conv_s, h_s, new_k_s, new_v_s)


def kernel(x_prompt, x_sample, state_conv, state_h, cache_k_win, cache_v_win, norm_pre, norm_post, w_in, conv_w, conv_b, lru_w_a, lru_b_a, lru_w_x, lru_b_x, lru_lambda, attn_sinks, w_branch, w_out):
    batch, s, d = x_prompt.shape
    nb, ts, _ = x_sample.shape
    assert batch == 1, "the prompt group is a single sequence"
    depth = w_in.shape[0]
    xp = x_prompt.reshape(s, d)
    xs_tm = x_sample.transpose(1, 0, 2).reshape(ts * nb, d)
    p_states, s_states = [], []
    for l in range(depth):
        xp, xs_tm, p_new, s_new = _layer(
            xp, xs_tm, state_conv[l], state_h[l], cache_k_win[l], cache_v_win[l], norm_pre[l],
            norm_post[l], w_in[l], conv_w[l], conv_b[l], lru_w_a[l], lru_b_a[l], lru_w_x[l],
            lru_b_x[l], lru_lambda[l], attn_sinks[l], w_branch[l], w_out[l])
        p_states.append(p_new)
        s_states.append(s_new)
    y_prompt = xp.reshape(1, s, d)
    y_sample = xs_tm.reshape(ts, nb, d).transpose(1, 0, 2)
    stack = lambda states, k: jnp.stack([st[k] for st in states])
    return (y_prompt, y_sample,
            stack(p_states, 0), stack(p_states, 1), stack(p_states, 2), stack(p_states, 3),
            stack(s_states, 0), stack(s_states, 1), stack(s_states, 2), stack(s_states, 3))
```

```python
import functools

import jax
import jax.numpy as jnp
from jax import lax
from jax.experimental import pallas as pl
from jax.experimental.pallas import tpu as pltpu

HEAD_DIM = 64
WINDOW = 128
CONV_WIDTH = 4
LRU_C = 8.0
RMS_EPS = 1e-6
NEG_INF = -1e30

LANES = 128
SUBLANES = 8
BF16_ROWS = 16
VMEM_LIMIT_BYTES = 58 * 1024 * 1024
IN_PROJ_VMEM_BUDGET = 52 * 1024 * 1024
IN_PROJ_VMEM_LIMIT_BYTES = 62 * 1024 * 1024

F32 = jnp.float32
BF16 = jnp.bfloat16


def _largest_divisor(n, candidates):
    for c in candidates:
        if n % c == 0:
            return c
    raise ValueError(f"no tile in {candidates} divides {n}")


def _params(sem, vmem_limit_bytes=VMEM_LIMIT_BYTES):
    return pltpu.CompilerParams(dimension_semantics=sem, vmem_limit_bytes=vmem_limit_bytes)


def _dot(a, b):
    return jnp.dot(a, b, preferred_element_type=F32)


def _dot_nt(a, b):
    return lax.dot_general(a, b, (((1,), (1,)), ((), ())), preferred_element_type=F32)


def _sigmoid(x):
    return jax.nn.sigmoid(x)


def _silu(x):
    return x * jax.nn.sigmoid(x)


def _rmsnorm_kernel(xp_ref, xs_ref, g_ref, o_ref, *, n_p):
    i = pl.program_id(0)

    def body(x_ref):
        x = x_ref[...]
        ms = jnp.mean(x * x, axis=-1, keepdims=True)
        o_ref[...] = ((x * lax.rsqrt(ms + RMS_EPS)) * g_ref[...]).astype(o_ref.dtype)

    @pl.when(i < n_p)
    def _():
        body(xp_ref)

    @pl.when(i >= n_p)
    def _():
        body(xs_ref)


def _rmsnorm_pre(xp, xs, g, tm):
    s, d = xp.shape
    ms = xs.shape[0]
    n_p, n_s = s // tm, ms // tm
    return pl.pallas_call(
        functools.partial(_rmsnorm_kernel, n_p=n_p),
        out_shape=jax.ShapeDtypeStruct((s + ms, d), BF16),
        grid=(n_p + n_s,),
        in_specs=[
            pl.BlockSpec((tm, d), lambda i: (jnp.minimum(i, n_p - 1), 0)),
            pl.BlockSpec((tm, d), lambda i: (jnp.maximum(i - n_p, 0), 0)),
            pl.BlockSpec((1, d), lambda i: (0, 0)),
        ],
        out_specs=pl.BlockSpec((tm, d), lambda i: (i, 0)),
        compiler_params=_params(("arbitrary",)),
        name="rmsnorm_pre",
    )(xp, xs, g)


def _cast_rows(src_ref, dst_ref, rows_per_step=256):
    n = src_ref.shape[0]
    step = _largest_divisor(n, (rows_per_step, 128, 64, 32, 16))

    def body(r, carry):
        r0 = pl.multiple_of(r * step, step)
        dst_ref[pl.ds(r0, step), :] = src_ref[pl.ds(r0, step), :].astype(dst_ref.dtype)
        return carry

    lax.fori_loop(0, n // step, body, 0)


def _in_proj_kernel(x_ref, w_ref, o_ref, wb_ref):
    @pl.when(pl.program_id(1) == 0)
    def _():
        _cast_rows(w_ref, wb_ref)

    o_ref[...] = _dot(x_ref[...], wb_ref[...]).astype(o_ref.dtype)


def _in_proj_tiles(m, d, n):
    best = None
    for tm in range(BF16_ROWS, m + 1, BF16_ROWS):
        if m % tm:
            continue
        for tn in range(2 * LANES, n + 1, 2 * LANES):
            if n % tn:
                continue
            vmem = d * tn * (2 * 4 + 2) + 2 * tm * d * 2 + 2 * tm * tn * 2
            if vmem <= IN_PROJ_VMEM_BUDGET and (best is None or tm * tn > best[0] * best[1]):
                best = (tm, tn)
    assert best is not None
    return best


def _in_proj(xn, w_in):
    m, d = xn.shape
    n = w_in.shape[1]
    tm, tn = _in_proj_tiles(m, d, n)
    return pl.pallas_call(
        _in_proj_kernel,
        out_shape=jax.ShapeDtypeStruct((m, n), BF16),
        grid=(n // tn, m // tm),
        in_specs=[
            pl.BlockSpec((tm, d), lambda j, i: (i, 0)),
            pl.BlockSpec((d, tn), lambda j, i: (0, j)),
        ],
        out_specs=pl.BlockSpec((tm, tn), lambda j, i: (i, j)),
        scratch_shapes=[pltpu.VMEM((d, tn), BF16)],
        compiler_params=_params(("arbitrary", "arbitrary"), IN_PROJ_VMEM_LIMIT_BYTES),
        name="in_proj",
    )(xn, w_in)


def _neg_c_softplus(lam):
    return -LRU_C * (jnp.maximum(-lam, 0.0) + jnp.log(1.0 + jnp.exp(-jnp.abs(lam))))


def _lru_conv(taps, w, bias):
    out = bias
    for k in range(CONV_WIDTH):
        out = out + taps[k] * w[k:k + 1, :]
    return out


def _lru_gate_dots(xc, wa_ref, wx_ref):
    blk = wa_ref.shape[1]
    xcb = xc.astype(BF16)
    ga, gx = [], []
    for q in range(xc.shape[1] // blk):
        xq = xcb[:, q * blk:(q + 1) * blk]
        ga.append(_dot(xq, wa_ref[q]))
        gx.append(_dot(xq, wx_ref[q]))
    return jnp.concatenate(ga, axis=1), jnp.concatenate(gx, axis=1)


def _lru_coeffs(xc, ga, gx, b_a, b_x, ncs):
    r_a = _sigmoid(ga + b_a)
    r_x = _sigmoid(gx + b_x)
    a = jnp.exp(ncs * r_a)
    b = jnp.sqrt(1.0 - a * a) * (r_x * xc)
    return a, b


def _scan_tile(a, b, sub):
    shift = 1
    while shift < SUBLANES:
        keep = sub >= shift
        a_sh = jnp.where(keep, pltpu.roll(a, shift, axis=0), 1.0)
        b_sh = jnp.where(keep, pltpu.roll(b, shift, axis=0), 0.0)
        b = b + a * b_sh
        a = a * a_sh
        shift *= 2
    return a, b


def _lru_prompt_conv(xcur, x_tail, sub, w, bias):
    n_tiles = xcur.shape[0] // SUBLANES
    tiles = [x_tail] + [xcur[v * SUBLANES:(v + 1) * SUBLANES] for v in range(n_tiles)]
    taps = []
    for k in range(CONV_WIDTH):
        back = CONV_WIDTH - 1 - k
        if back == 0:
            taps.append(xcur)
            continue
        rolled = [pltpu.roll(t, back, axis=0) for t in tiles]
        taps.append(jnp.concatenate(
            [jnp.where(sub >= back, rolled[v + 1], rolled[v]) for v in range(n_tiles)], axis=0))
    return _lru_conv(taps, w, bias), tiles[-1]


def _lru_prompt_scan(a, b, h_in, sub):
    hs = []
    for v in range(a.shape[0] // SUBLANES):
        rows = slice(v * SUBLANES, (v + 1) * SUBLANES)
        a_in, b_in = _scan_tile(a[rows], b[rows], sub)
        hv = a_in * h_in + b_in
        hs.append(hv)
        h_in = jnp.broadcast_to(hv[SUBLANES - 1:SUBLANES, :], hv.shape)
    return jnp.concatenate(hs, axis=0), h_in


def _lru_sample_kernel(x_ref, g_ref, cw_ref, cb_ref, wa_ref, ba_ref, wx_ref, bx_ref, lam_ref,
                       cst_ref, h0_ref, o_ref, hs_ref):
    nb = h0_ref.shape[0]
    ts = x_ref.shape[0] // nb
    w, bias = cw_ref[...], cb_ref[...]
    ncs = _neg_c_softplus(lam_ref[...])
    xs = [cst_ref[k] for k in range(CONV_WIDTH - 1)]
    xs += [x_ref[t * nb:(t + 1) * nb, :].astype(F32) for t in range(ts)]
    h = h0_ref[...]
    for t in range(ts):
        xc = _lru_conv(xs[t:t + CONV_WIDTH], w, bias)
        ga, gx = _lru_gate_dots(xc, wa_ref, wx_ref)
        a, b = _lru_coeffs(xc, ga, gx, ba_ref[...], bx_ref[...], ncs)
        h = a * h + b
        g = g_ref[t * nb:(t + 1) * nb, :].astype(F32)
        o_ref[t * nb:(t + 1) * nb, :] = (h * _silu(g)).astype(o_ref.dtype)
    hs_ref[...] = h


def _lru_sample(z, lru_params, cst_tm, h0, s, tb, cbw):
    conv_w, conv_b, w_a, b_a, w_x, b_x, lam = lru_params
    width = conv_w.shape[1]
    nb = h0.shape[0]
    blk = w_a.shape[1]
    n_c = width // cbw
    blk_s = s // tb
    row = lambda c: (0, c)
    gate_w = pl.BlockSpec((cbw // blk, blk, blk), lambda c: (c, 0, 0))
    return pl.pallas_call(
        _lru_sample_kernel,
        out_shape=(jax.ShapeDtypeStruct((tb, width), BF16), jax.ShapeDtypeStruct((nb, width), F32)),
        grid=(n_c,),
        in_specs=[
            pl.BlockSpec((tb, cbw), lambda c: (blk_s, c)),
            pl.BlockSpec((tb, cbw), lambda c: (blk_s, n_c + c)),
            pl.BlockSpec((CONV_WIDTH, cbw), row),
            pl.BlockSpec((1, cbw), row),
            gate_w,
            pl.BlockSpec((1, cbw), row),
            gate_w,
            pl.BlockSpec((1, cbw), row),
            pl.BlockSpec((1, cbw), row),
            pl.BlockSpec((CONV_WIDTH - 1, nb, cbw), lambda c: (0, 0, c)),
            pl.BlockSpec((nb, cbw), row),
        ],
        out_specs=(pl.BlockSpec((tb, cbw), row), pl.BlockSpec((nb, cbw), row)),
        compiler_params=_params(("arbitrary",)),
        name="lru_sample",
    )(z, z, conv_w, conv_b, w_a, b_a, w_x, b_x, lam, cst_tm, h0)


def _attn_prompt_kernel(sink_ref, q_ref, kp_ref, kc_ref, vp_ref, vc_ref, g0_ref, g1_ref, os_ref,
                        o_ref, *, n_q, n_kv, group):
    i = pl.program_id(0)

    @pl.when(i < n_q)
    def _():
        _attn_prompt_block(i, sink_ref, q_ref, kp_ref, kc_ref, vp_ref, vc_ref, g0_ref, g1_ref,
                           o_ref, n_kv, group)

    @pl.when(i >= n_q)
    def _():
        o_ref[...] = os_ref[...]


def _attn_prompt_block(i, sink_ref, q_ref, kp_ref, kc_ref, vp_ref, vc_ref, g0_ref, g1_ref, o_ref,
                       n_kv, group):
    qb = q_ref.shape[0]
    aw_half = g0_ref.shape[1]
    scale = HEAD_DIM ** -0.5
    hw = group * HEAD_DIM
    c = lax.broadcasted_iota(jnp.int32, (2 * qb, qb), 0)
    r = lax.broadcasted_iota(jnp.int32, (2 * qb, qb), 1)
    rel = qb + r - c
    mask = (rel >= 0) & (rel <= WINDOW) & ((c >= qb) | (i > 0))
    lane = lax.broadcasted_iota(jnp.int32, (qb, LANES), 1)
    half_mask = (lane < HEAD_DIM, lane >= HEAD_DIM)
    zero = jnp.zeros((qb, LANES), BF16)
    for p in range(n_kv // 2):
        lanes = slice(p * LANES, (p + 1) * LANES)
        k_f32 = jnp.concatenate([kp_ref[:, lanes], kc_ref[:, lanes]], axis=0).astype(F32) * scale
        k_variants = (k_f32.astype(BF16), pltpu.roll(k_f32, HEAD_DIM, axis=1).astype(BF16))
        v_pair = jnp.concatenate([vp_ref[:, lanes], vc_ref[:, lanes]], axis=0)
        v_t = v_pair.astype(F32).T.astype(BF16)
        heads = ([], [])
        for half in range(2):
            for g in range(group):
                heads[0 if (g % 2) == half else 1].append((2 * p + half, g))
        out_t = {}
        for variant in range(2):
            q_rows = []
            for kh, g in heads[variant]:
                c0 = kh * hw + (g // 2) * LANES
                q_rows.append(jnp.where(half_mask[g % 2], q_ref[:, c0:c0 + LANES], zero))
            s_t = _dot_nt(k_variants[variant], jnp.concatenate(q_rows, axis=0))
            p_blocks, inv_den = [], []
            for b, (kh, g) in enumerate(heads[variant]):
                sink = sink_ref[kh * group + g]
                sb = jnp.where(mask, s_t[:, b * qb:(b + 1) * qb], NEG_INF)
                m = jnp.maximum(jnp.max(sb, axis=0, keepdims=True), sink)
                pe = jnp.exp(sb - m)
                den = jnp.sum(pe, axis=0, keepdims=True) + jnp.exp(sink - m)
                p_blocks.append(pe.astype(BF16))
                inv_den.append(1.0 / den)
            o_t = _dot(v_t, jnp.concatenate(p_blocks, axis=1))
            for b, (kh, g) in enumerate(heads[variant]):
                r0 = (kh % 2) * HEAD_DIM
                out_t[(kh, g)] = o_t[r0:r0 + HEAD_DIM, b * qb:(b + 1) * qb] * inv_den[b]
        for half in range(2):
            kh = 2 * p + half
            for j in range(group // 2):
                c0 = kh * hw + j * LANES
                tile = jnp.concatenate([out_t[(kh, 2 * j)], out_t[(kh, 2 * j + 1)]], axis=0).T
                g_ref, gc = (g0_ref, c0) if c0 < aw_half else (g1_ref, c0 - aw_half)
                gate = g_ref[:, gc:gc + LANES].astype(F32)
                o_ref[:, c0:c0 + LANES] = (tile * _silu(gate)).astype(o_ref.dtype)


def _attn_prompt(z, sinks, o_b_s, s, width, aw, kvw, n_kv, group):
    qb = WINDOW
    n_q = s // qb
    assert s % qb == 0 and o_b_s.shape[0] % qb == 0 and n_kv % 2 == 0 and group % 2 == 0
    q_blk = (2 * width) // aw
    k_blk = (2 * width + aw) // kvw
    v_blk = k_blk + 1
    g_blk = (2 * width + aw + 2 * kvw) // (aw // 2)
    assert (2 * width) % aw == 0 and (2 * width + aw) % kvw == 0
    assert (2 * width + aw + 2 * kvw) % (aw // 2) == 0
    cur = lambda i: jnp.minimum(i, n_q - 1)
    prev = lambda i: jnp.maximum(cur(i) - 1, 0)
    return pl.pallas_call(
        functools.partial(_attn_prompt_kernel, n_q=n_q, n_kv=n_kv, group=group),
        out_shape=jax.ShapeDtypeStruct((s + o_b_s.shape[0], aw), BF16),
        grid=(n_q + o_b_s.shape[0] // qb,),
        in_specs=[
            pl.BlockSpec(memory_space=pltpu.SMEM),
            pl.BlockSpec((qb, aw), lambda i: (cur(i), q_blk)),
            pl.BlockSpec((qb, kvw), lambda i: (prev(i), k_blk)),
            pl.BlockSpec((qb, kvw), lambda i: (cur(i), k_blk)),
            pl.BlockSpec((qb, kvw), lambda i: (prev(i), v_blk)),
            pl.BlockSpec((qb, kvw), lambda i: (cur(i), v_blk)),
            pl.BlockSpec((qb, aw // 2), lambda i: (cur(i), g_blk)),
            pl.BlockSpec((qb, aw // 2), lambda i: (cur(i), g_blk + 1)),
            pl.BlockSpec((qb, aw), lambda i: (jnp.maximum(i - n_q, 0), 0)),
        ],
        out_specs=pl.BlockSpec((qb, aw), lambda i: (i, 0)),
        compiler_params=_params(("arbitrary",)),
        name="attn_prompt",
    )(sinks, z, z, z, z, z, z, z, o_b_s)


def _attn_sample_kernel(q_ref, kn_ref, vn_ref, ck_ref, cv_ref, g_ref, sink_ref, o_ref,
                        s_ref, p_ref, *, n_kv, group, ts):
    sb, tq, _ = q_ref.shape
    wb = ck_ref.shape[1]
    tk = kn_ref.shape[1]
    n_keys = s_ref.shape[1]
    n_pairs = n_kv // 2
    pair_rows = 2 * group * tq
    seq_rows = n_pairs * pair_rows
    scale = HEAD_DIM ** -0.5
    hw = group * HEAD_DIM
    lane = lax.broadcasted_iota(jnp.int32, (tq, LANES), 1)
    key_pad = jnp.zeros((n_keys - wb - tk, LANES), F32)

    def keys_of(c_ref, n_ref, n, lanes):
        return jnp.concatenate([c_ref[n, :, lanes], n_ref[n, :, lanes], key_pad], axis=0).astype(BF16)

    def score_body(n, carry):
        for p in range(n_pairs):
            lanes = slice(p * LANES, (p + 1) * LANES)
            pieces = []
            for half in range(2):
                for g in range(group):
                    c0 = (2 * p + half) * hw + (g // 2) * LANES
                    tile = q_ref[n, :, c0:c0 + LANES] * scale
                    if (g % 2) != half:
                        tile = pltpu.roll(tile, HEAD_DIM, axis=1)
                    keep = (lane >= HEAD_DIM * half) & (lane < HEAD_DIM * (half + 1))
                    pieces.append(jnp.where(keep, tile, 0.0))
            lhs = jnp.concatenate(pieces, axis=0).astype(BF16)
            r0 = pl.multiple_of(n * seq_rows + p * pair_rows, pair_rows)
            s_ref[pl.ds(r0, pair_rows), :] = _dot_nt(lhs, keys_of(ck_ref, kn_ref, n, lanes))
        return carry

    lax.fori_loop(0, sb, score_body, 0)

    t = lax.broadcasted_iota(jnp.int32, (tq, n_keys), 0)
    c = lax.broadcasted_iota(jnp.int32, (tq, n_keys), 1)
    mask8 = ((c < wb) & (t + wb - c <= WINDOW)) | ((c >= wb) & (c - wb <= t) & (c - wb < ts))
    reps = sb * seq_rows // tq
    mask = jnp.concatenate([mask8] * reps, axis=0)
    sink = jnp.concatenate([sink_ref[:, 0:1]] * sb, axis=0)
    s = jnp.where(mask, s_ref[...], NEG_INF)
    m = jnp.maximum(jnp.max(s, axis=-1, keepdims=True), sink)
    pe = jnp.exp(s - m)
    den = jnp.sum(pe, axis=-1, keepdims=True) + jnp.exp(sink - m)
    p_ref[...] = (pe / den).astype(p_ref.dtype)

    def value_body(n, carry):
        for p in range(n_pairs):
            lanes = slice(p * LANES, (p + 1) * LANES)
            r0 = pl.multiple_of(n * seq_rows + p * pair_rows, pair_rows)
            o = _dot(p_ref[pl.ds(r0, pair_rows), :], keys_of(cv_ref, vn_ref, n, lanes))
            for half in range(2):
                kh = 2 * p + half
                for j in range(group // 2):
                    rows = (half * group + 2 * j) * tq
                    o_lo, o_hi = o[rows:rows + tq], o[rows + tq:rows + 2 * tq]
                    if half == 0:
                        o_hi = pltpu.roll(o_hi, HEAD_DIM, axis=1)
                    else:
                        o_lo = pltpu.roll(o_lo, HEAD_DIM, axis=1)
                    c0 = kh * hw + j * LANES
                    gate = g_ref[n, :, c0:c0 + LANES]
                    o_ref[n, :, c0:c0 + LANES] = jnp.where(lane < HEAD_DIM, o_lo, o_hi) * _silu(gate)
        return carry

    lax.fori_loop(0, sb, value_body, 0)


def _attn_sample(q8, kn, vn, cache_k, cache_v, g8, sink_rows, n_kv, group, ts):
    nb, tq, aw = q8.shape
    tk = kn.shape[1]
    wb, kvw = cache_k.shape[1], cache_k.shape[2]
    sb = _largest_divisor(nb, (8, 4, 2, 1))
    n_keys = -(-(wb + tk) // (2 * LANES)) * (2 * LANES)
    rows = sb * (n_kv // 2) * 2 * group * tq
    assert sink_rows.shape[0] * sb == rows
    blk3 = lambda r, width: pl.BlockSpec((sb, r, width), lambda i: (i, 0, 0))
    return pl.pallas_call(
        functools.partial(_attn_sample_kernel, n_kv=n_kv, group=group, ts=ts),
        out_shape=jax.ShapeDtypeStruct((nb, tq, aw), F32),
        grid=(nb // sb,),
        in_specs=[
            blk3(tq, aw), blk3(tk, kvw), blk3(tk, kvw), blk3(wb, kvw), blk3(wb, kvw), blk3(tq, aw),
            pl.BlockSpec(sink_rows.shape, lambda i: (0, 0)),
        ],
        out_specs=blk3(tq, aw),
        scratch_shapes=[pltpu.VMEM((rows, n_keys), F32), pltpu.VMEM((rows, n_keys), BF16)],
        compiler_params=_params(("arbitrary",)),
        name="attn_sample",
    )(q8, kn, vn, cache_k, cache_v, g8, sink_rows)


def _branch_lru_kernel(x_ref, g_ref, cw_ref, cb_ref, wa_ref, wx_ref, ba_ref, bx_ref, lam_ref,
                       oas_ref, ob_ref, w_ref, ma_ref, mb_ref,
                       o_ref, hp_ref, oa_ref, xt_ref, hc_ref, xc_ref, ga_ref, gx_ref,
                       *, n_p, n_t, chunk):
    u = pl.program_id(0)
    tm, cbw = x_ref.shape

    @pl.when(u == 0)
    def _():
        for ref in (oa_ref, xt_ref, hc_ref, xc_ref, ga_ref, gx_ref):
            ref[...] = jnp.zeros_like(ref)

    unit_t = jnp.maximum(u - 1, 0)
    row_t, ch_t = lax.div(unit_t, n_t), lax.rem(unit_t, n_t)
    live_t = (u >= 1) & (unit_t < n_p * n_t)
    row_d = lax.div(jnp.maximum(u - 1 - n_t, 0), n_t)
    ch_c = lax.rem(u, n_t)
    live_c = u < n_p * n_t

    slot_d = lax.rem(row_d, 2)
    width = n_t * cbw
    tn = o_ref.shape[1]
    n_chunks = tm // chunk
    pieces = [(rh, ch) for rh in range(2) for ch in range(2)]
    chunks_per_piece = n_chunks // len(pieces)
    assert chunks_per_piece * len(pieces) == n_chunks

    def product_piece(rh, ch):
        rows = slice(rh * tm // 2, (rh + 1) * tm // 2)
        cols = slice(ch * tn // 2, (ch + 1) * tn // 2)
        o_a = jnp.concatenate([oa_ref[slot_d, q, rows, :] for q in range(n_t)], axis=1)
        pa = _dot(o_a, w_ref[0:width, cols])
        pb = _dot(ob_ref[rows, :], w_ref[width:, cols])
        ma = ma_ref[rows, cols].astype(F32)
        mb = mb_ref[rows, cols].astype(F32)
        o_ref[rows, cols] = (_sigmoid(ma) * pa + _sigmoid(mb) * pb).astype(o_ref.dtype)

    sub = lax.broadcasted_iota(jnp.int32, (SUBLANES, cbw), 0)
    stage_w = lax.rem(u, 2)
    stage_r = 1 - stage_w

    slot_t = lax.rem(row_t, 2)
    cg = min(cbw, 2 * LANES)
    groups = [slice(k * cg, (k + 1) * cg) for k in range(cbw // cg)]
    sub_g = lax.broadcasted_iota(jnp.int32, (SUBLANES, cg), 0)
    coef = [(ba_ref[:, gs], bx_ref[:, gs], _neg_c_softplus(lam_ref[:, gs])) for gs in groups]
    h_ins = [hc_ref[ch_t, :, gs] for gs in groups]
    for c in range(n_chunks):
        if c % chunks_per_piece == 0:
            product_piece(*pieces[c // chunks_per_piece])
        rows = slice(c * chunk, (c + 1) * chunk)
        for k, gs in enumerate(groups):
            a, b = _lru_coeffs(xc_ref[stage_r, rows, gs], ga_ref[stage_r, rows, gs],
                               gx_ref[stage_r, rows, gs], *coef[k])
            h, h_ins[k] = _lru_prompt_scan(a, b, h_ins[k], sub_g)
            o_a_new = (h * _silu(g_ref[rows, gs].astype(F32))).astype(oa_ref.dtype)
            oa_ref[slot_t, ch_t, rows, gs] = jnp.where(live_t, o_a_new, oas_ref[rows, gs])
    for k, gs in enumerate(groups):
        h_in = jnp.where(live_t, h_ins[k], hc_ref[ch_t, :, gs])
        hc_ref[ch_t, :, gs] = h_in
        hp_ref[ch_t, :, gs] = h_in[0:1, :]

    w, bias = cw_ref[...], cb_ref[...]
    x_tail = xt_ref[ch_c]
    xcs = []
    for c in range(n_chunks):
        rows = slice(c * chunk, (c + 1) * chunk)
        xc, x_tail = _lru_prompt_conv(x_ref[rows, :].astype(F32), x_tail, sub, w, bias)
        xcs.append(xc)
    xt_ref[ch_c] = jnp.where(live_c, x_tail, xt_ref[ch_c])
    xc_all = jnp.concatenate(xcs, axis=0)
    ga, gx = _lru_gate_dots(xc_all, wa_ref, wx_ref)
    xc_ref[stage_w] = xc_all
    ga_ref[stage_w] = ga
    gx_ref[stage_w] = gx


def _branch_lru(z, lru_params, o_a_s, o_b, w_branch_bf16, ma_col, s, tm, tn):
    conv_w, conv_b, w_a, b_a, w_x, b_x, lam = lru_params
    m = z.shape[0]
    width = conv_w.shape[1]
    blk = w_a.shape[1]
    aw = o_b.shape[1]
    d = w_branch_bf16.shape[1]
    n_p = s // tm
    n_r = m // tm
    n_t = d // tn
    cbw = width // n_t
    assert cbw % blk == 0 and cbw % LANES == 0
    chunk = tm // 8
    assert chunk % SUBLANES == 0
    ma_blk = ma_col // tn
    mb_blk = (ma_col + d) // tn
    row_c = lambda u: jnp.minimum(u // n_t, n_p - 1)
    ch_c = lambda u: u % n_t
    unit_t = lambda u: jnp.maximum(u - 1, 0)
    row_t = lambda u: jnp.minimum(unit_t(u) // n_t, n_p - 1)
    ch_t = lambda u: unit_t(u) % n_t
    unit_d = lambda u: jnp.maximum(u - 1 - n_t, 0)
    row_d = lambda u: unit_d(u) // n_t
    tile_d = lambda u: unit_d(u) % n_t
    gate_w = pl.BlockSpec((cbw // blk, blk, blk), lambda u: (ch_c(u), 0, 0))
    chan_t = lambda rows: pl.BlockSpec((rows, cbw), lambda u: (0, ch_t(u)))
    return pl.pallas_call(
        functools.partial(_branch_lru_kernel, n_p=n_p, n_t=n_t, chunk=chunk),
        out_shape=(jax.ShapeDtypeStruct((m, d), BF16), jax.ShapeDtypeStruct((n_t, 1, cbw), F32)),
        grid=((n_r + 1) * n_t + 1,),
        in_specs=[
            pl.BlockSpec((tm, cbw), lambda u: (row_c(u), ch_c(u))),
            pl.BlockSpec((tm, cbw), lambda u: (row_t(u), n_t + ch_t(u))),
            pl.BlockSpec((CONV_WIDTH, cbw), lambda u: (0, ch_c(u))),
            pl.BlockSpec((1, cbw), lambda u: (0, ch_c(u))),
            gate_w,
            gate_w,
            chan_t(1),
            chan_t(1),
            chan_t(1),
            chan_t(tm),
            pl.BlockSpec((tm, aw), lambda u: (row_d(u), 0)),
            pl.BlockSpec((width + aw, tn), lambda u: (0, tile_d(u))),
            pl.BlockSpec((tm, tn), lambda u: (row_d(u), ma_blk + tile_d(u))),
            pl.BlockSpec((tm, tn), lambda u: (row_d(u), mb_blk + tile_d(u))),
        ],
        out_specs=(pl.BlockSpec((tm, tn), lambda u: (row_d(u), tile_d(u))),
                   pl.BlockSpec((n_t, 1, cbw), lambda u: (0, 0, 0))),
        scratch_shapes=[
            pltpu.VMEM((2, n_t, tm, cbw), BF16),
            pltpu.VMEM((n_t, SUBLANES, cbw), F32),
            pltpu.VMEM((n_t, SUBLANES, cbw), F32),
            pltpu.VMEM((2, tm, cbw), F32),
            pltpu.VMEM((2, tm, cbw), F32),
            pltpu.VMEM((2, tm, cbw), F32),
        ],
        compiler_params=_params(("arbitrary",)),
        name="branch_lru",
    )(z, z, conv_w, conv_b, w_a, w_x, b_a, b_x, lam, o_a_s, o_b, w_branch_bf16, z, z)


def _out_kernel(m_ref, w_ref, g_ref, x_ref, y_ref, acc_ref, ss_ref, *, d):
    i = pl.program_id(0)
    j = pl.program_id(1)
    slot = lax.rem(i, 2)
    prev = 1 - slot

    @pl.when((i == 0) & (j == 0))
    def _():
        acc_ref[1] = jnp.zeros(acc_ref.shape[1:], F32)
        ss_ref[...] = jnp.zeros_like(ss_ref)

    t = _dot(m_ref[...], w_ref[...])
    acc_ref[slot, j] = t
    ssq = jnp.sum(t * t, axis=-1, keepdims=True)
    ss_ref[slot] = jnp.where(j == 0, ssq, ss_ref[slot] + ssq)

    inv = lax.rsqrt(ss_ref[prev] / d + RMS_EPS)
    y_ref[...] = x_ref[...] + (acc_ref[prev, j] * inv) * g_ref[...]


def _out_proj(merged, w_out_bf16, g, x, row0, tm, tn):
    rows, d = x.shape
    n_r = rows // tm
    n_t = d // tn
    blk0 = row0 // tm
    xy_map = lambda i, j: (jnp.maximum(i - 1, 0), jnp.where(i == 0, 0, j))
    return pl.pallas_call(
        functools.partial(_out_kernel, d=d),
        out_shape=jax.ShapeDtypeStruct((rows, d), F32),
        grid=(n_r + 1, n_t),
        in_specs=[
            pl.BlockSpec((tm, d), lambda i, j: (blk0 + jnp.minimum(i, n_r - 1), 0)),
            pl.BlockSpec((d, tn), lambda i, j: (0, jnp.where(i == n_r, n_t - 1, j))),
            pl.BlockSpec((1, tn), lambda i, j: (0, j)),
            pl.BlockSpec((tm, tn), xy_map),
        ],
        out_specs=pl.BlockSpec((tm, tn), xy_map),
        scratch_shapes=[pltpu.VMEM((2, n_t, tm, tn), F32), pltpu.VMEM((2, tm, 1), F32)],
        compiler_params=_params(("arbitrary", "arbitrary")),
        name="out_proj",
    )(merged, w_out_bf16, g, x)


def _pad_rows(x, rows):
    return jnp.pad(x, ((0, 0), (0, rows - x.shape[1]), (0, 0)))


def _layer(xp, xs_tm, conv_state, h_state, cache_k, cache_v, norm_pre, norm_post, w_in, conv_w,
           conv_b, w_a, b_a, w_x, b_x, lam, sinks, w_branch, w_out):
    s, d = xp.shape
    nb, wb, n_kv, _ = cache_k.shape
    ts = xs_tm.shape[0] // nb
    width = conv_w.shape[1]
    n_heads = sinks.shape[0]
    group = n_heads // n_kv
    aw, kvw = n_heads * HEAD_DIM, n_kv * HEAD_DIM
    m = s + nb * ts
    tm = nb * ts
    assert s % tm == 0 and tm % SUBLANES == 0 and ts >= CONV_WIDTH - 1
    q_col = 2 * width
    k_col = q_col + aw
    v_col = k_col + kvw
    g_col = v_col + kvw
    ma_col = g_col + aw
    assert w_in.shape[1] == ma_col + 2 * d

    xn = _rmsnorm_pre(xp, xs_tm, norm_pre.reshape(1, d), _largest_divisor(tm, (256, 128, 64, 8)))
    z = _in_proj(xn, w_in)

    tn = _largest_divisor(ma_col, (512, 256, 128))
    assert d % tn == 0 and width % (d // tn) == 0
    lru_params = (conv_w, conv_b.reshape(1, width), w_a.astype(BF16), b_a.reshape(1, width),
                  w_x.astype(BF16), b_x.reshape(1, width), lam.reshape(1, width))
    o_a_s, h_s = _lru_sample(z, lru_params, conv_state.transpose(1, 0, 2), h_state, s, tm,
                             width // (d // tn))

    def sample_cols(c0, c1):
        blk = lax.slice(z, (s, c0), (m, c1)).reshape(ts, nb, c1 - c0)
        return blk.transpose(1, 0, 2).astype(F32)

    q_s, k_s, v_s = sample_cols(q_col, k_col), sample_cols(k_col, v_col), sample_cols(v_col, g_col)
    g_s = sample_cols(g_col, ma_col)
    tq = -(-ts // SUBLANES) * SUBLANES
    tk = -(-ts // BF16_ROWS) * BF16_ROWS
    ck = cache_k.reshape(nb, wb, kvw)
    cv = cache_v.reshape(nb, wb, kvw)
    sink_rows = jnp.broadcast_to(sinks.reshape(n_heads, 1, 1), (n_heads, tq, LANES))
    sink_rows = sink_rows.reshape(n_heads * tq, LANES)
    o_b_s = _attn_sample(_pad_rows(q_s, tq), _pad_rows(k_s, tk), _pad_rows(v_s, tk), ck, cv,
                         _pad_rows(g_s, tq), sink_rows, n_kv, group, ts)
    o_b_s = o_b_s[:, :ts].transpose(1, 0, 2).reshape(nb * ts, aw).astype(BF16)
    o_b = _attn_prompt(z, sinks, o_b_s, s, width, aw, kvw, n_kv, group)

    merged, h_p = _branch_lru(z, lru_params, o_a_s, o_b, w_branch.astype(BF16), ma_col, s, tm, tn)
    h_p = h_p.reshape(1, width)
    w_out_bf16 = w_out.astype(BF16)
    g_post = norm_post.reshape(1, d)
    tn_out = _largest_divisor(d, (1024, 512, 256, 128))
    y_p = _out_proj(merged, w_out_bf16, g_post, xp, 0, tm, tn_out)
    y_s = _out_proj(merged, w_out_bf16, g_post, xs_tm, s, tm, tn_out)

    keep = CONV_WIDTH - 1
    new_conv_p = lax.slice(z, (s - keep, 0), (s, width)).astype(F32)[None]
    wbp = min(WINDOW, s)
    new_k_p = lax.slice(z, (s - wbp, k_col), (s, v_col)).astype(F32).reshape(1, wbp, n_kv, HEAD_DIM)
    new_v_p = lax.slice(z, (s - wbp, v_col), (s, g_col)).astype(F32).reshape(1, wbp, n_kv, HEAD_DIM)
    x_lru_s = sample_cols(0, width)
    new_conv_s = jnp.concatenate([conv_state, x_lru_s], axis=1)[:, -keep:]
    new_k_s = jnp.concatenate([ck, k_s], axis=1)[:, -wb:].reshape(nb, wb, n_kv, HEAD_DIM)
    new_v_s = jnp.concatenate([cv, v_s], axis=1)[:, -wb:].reshape(nb, wb, n_kv, HEAD_DIM)
    return y_p, y_s, (new_conv_p, h_p, new_k_p, new_v_p), (new_conv_s, h_s, new_k_s, new_v_s)


def kernel(x_prompt, x_sample, state_conv, state_h, cache_k_win, cache_v_win, norm_pre, norm_post, w_in, conv_w, conv_b, lru_w_a, lru_b_a, lru_w_x, lru_b_x, lru_lambda, attn_sinks, w_branch, w_out):
    batch, s, d = x_prompt.shape
    nb, ts, _ = x_sample.shape
    assert batch == 1, "the prompt group is a single sequence"
    depth = w_in.shape[0]
    xp = x_prompt.reshape(s, d)
    xs_tm = x_sample.transpose(1, 0, 2).reshape(ts * nb, d)
    p_states, s_states = [], []
    for l in range(depth):
        xp, xs_tm, p_new, s_new = _layer(
            xp, xs_tm, state_conv[l], state_h[l], cache_k_win[l], cache_v_win[l], norm_pre[l],
            norm_post[l], w_in[l], conv_w[l], conv_b[l], lru_w_a[l], lru_b_a[l], lru_w_x[l],
            lru_b_x[l], lru_lambda[l], attn_sinks[l], w_branch[l], w_out[l])
        p_states.append(p_new)
        s_states.append(s_new)
    y_prompt = xp.reshape(1, s, d)
    y_sample = xs_tm.reshape(ts, nb, d).transpose(1, 0, 2)
    stack = lambda states, k: jnp.stack([st[k] for st in states])
    return (y_prompt, y_sample,
            stack(p_states, 0), stack(p_states, 1), stack(p_states, 2), stack(p_states, 3),
            stack(s_states, 0), stack(s_states, 1), stack(s_states, 2), stack(s_states, 3))
```

```python
import functools

import jax
import jax.numpy as jnp
from jax import lax
from jax.experimental import pallas as pl
from jax.experimental.pallas import tpu as pltpu

HEAD_DIM = 64
WINDOW = 128
CONV_WIDTH = 4
LRU_C = 8.0
RMS_EPS = 1e-6
NEG_INF = -1e30

LANES = 128
SUBLANES = 8
BF16_ROWS = 16
VMEM_LIMIT_BYTES = 58 * 1024 * 1024
IN_PROJ_VMEM_BUDGET = 52 * 1024 * 1024
IN_PROJ_VMEM_LIMIT_BYTES = 63 * 1024 * 1024

F32 = jnp.float32
BF16 = jnp.bfloat16


def _largest_divisor(n, candidates):
    for c in candidates:
        if n % c == 0:
            return c
    raise ValueError(f"no tile in {candidates} divides {n}")


def _params(sem, vmem_limit_bytes=VMEM_LIMIT_BYTES):
    return pltpu.CompilerParams(dimension_semantics=sem, vmem_limit_bytes=vmem_limit_bytes)


def _dot(a, b):
    return jnp.dot(a, b, preferred_element_type=F32)


def _dot_nt(a, b):
    return lax.dot_general(a, b, (((1,), (1,)), ((), ())), preferred_element_type=F32)


def _sigmoid(x):
    return jax.nn.sigmoid(x)


def _silu(x):
    return x * jax.nn.sigmoid(x)


def _rmsnorm_kernel(xp_ref, xs_ref, g_ref, o_ref, *, n_p):
    i = pl.program_id(0)

    def body(x_ref):
        x = x_ref[...]
        ms = jnp.mean(x * x, axis=-1, keepdims=True)
        o_ref[...] = ((x * lax.rsqrt(ms + RMS_EPS)) * g_ref[...]).astype(o_ref.dtype)

    @pl.when(i < n_p)
    def _():
        body(xp_ref)

    @pl.when(i >= n_p)
    def _():
        body(xs_ref)


def _rmsnorm_pre(xp, xs, g, tm):
    s, d = xp.shape
    ms = xs.shape[0]
    n_p, n_s = s // tm, ms // tm
    return pl.pallas_call(
        functools.partial(_rmsnorm_kernel, n_p=n_p),
        out_shape=jax.ShapeDtypeStruct((s + ms, d), BF16),
        grid=(n_p + n_s,),
        in_specs=[
            pl.BlockSpec((tm, d), lambda i: (jnp.minimum(i, n_p - 1), 0)),
            pl.BlockSpec((tm, d), lambda i: (jnp.maximum(i - n_p, 0), 0)),
            pl.BlockSpec((1, d), lambda i: (0, 0)),
        ],
        out_specs=pl.BlockSpec((tm, d), lambda i: (i, 0)),
        compiler_params=_params(("arbitrary",)),
        name="rmsnorm_pre",
    )(xp, xs, g)


def _cast_rows(src_ref, dst_ref, rows_per_step=256):
    n = src_ref.shape[0]
    step = _largest_divisor(n, (rows_per_step, 128, 64, 32, 16))

    def body(r, carry):
        r0 = pl.multiple_of(r * step, step)
        dst_ref[pl.ds(r0, step), :] = src_ref[pl.ds(r0, step), :].astype(dst_ref.dtype)
        return carry

    lax.fori_loop(0, n // step, body, 0)


def _in_proj_kernel(x_ref, w_ref, *rest):
    n_extra = (len(rest) - 2) // 2
    extra_in, o_ref = rest[:n_extra], rest[n_extra]
    extra_out, wb_ref = rest[n_extra + 1:2 * n_extra + 1], rest[-1]

    @pl.when(pl.program_id(1) == 0)
    def _():
        _cast_rows(w_ref, wb_ref)

    o_ref[...] = _dot(x_ref[...], wb_ref[...]).astype(o_ref.dtype)
    for src, dst in zip(extra_in, extra_out):
        dst[...] = src[...].astype(dst.dtype)


def _in_proj_tiles(m, d, n):
    best = None
    for tm in range(BF16_ROWS, m + 1, BF16_ROWS):
        if m % tm:
            continue
        for tn in range(2 * LANES, n + 1, 2 * LANES):
            if n % tn:
                continue
            vmem = d * tn * (2 * 4 + 2) + 2 * tm * d * 2 + 2 * tm * tn * 2
            if vmem <= IN_PROJ_VMEM_BUDGET and (best is None or tm * tn > best[0] * best[1]):
                best = (tm, tn)
    assert best is not None
    return best


def _in_proj(xn, w_in, extra_weights=()):
    m, d = xn.shape
    n = w_in.shape[1]
    tm, tn = _in_proj_tiles(m, d, n)
    n_i = m // tm
    n_steps = (n // tn) * n_i
    extra_specs = []
    for w in extra_weights:
        rows = w.shape[0]
        slab = next(r for r in range(BF16_ROWS, rows + 1, BF16_ROWS)
                    if rows % r == 0 and rows // r <= n_steps)
        last = rows // slab - 1
        extra_specs.append(pl.BlockSpec(
            (slab, w.shape[1]), lambda j, i, last=last: (jnp.minimum(j * n_i + i, last), 0)))
    out = pl.pallas_call(
        _in_proj_kernel,
        out_shape=(jax.ShapeDtypeStruct((m, n), BF16),
                   *[jax.ShapeDtypeStruct(w.shape, BF16) for w in extra_weights]),
        grid=(n // tn, n_i),
        in_specs=[
            pl.BlockSpec((tm, d), lambda j, i: (i, 0)),
            pl.BlockSpec((d, tn), lambda j, i: (0, j)),
            *extra_specs,
        ],
        out_specs=(pl.BlockSpec((tm, tn), lambda j, i: (i, j)), *extra_specs),
        scratch_shapes=[pltpu.VMEM((d, tn), BF16)],
        compiler_params=_params(("arbitrary", "arbitrary"), IN_PROJ_VMEM_LIMIT_BYTES),
        name="in_proj",
    )(xn, w_in, *extra_weights)
    return out[0], out[1:]


def _neg_c_softplus(lam):
    return -LRU_C * (jnp.maximum(-lam, 0.0) + jnp.log(1.0 + jnp.exp(-jnp.abs(lam))))


def _lru_conv(taps, w, bias):
    out = bias
    for k in range(CONV_WIDTH):
        out = out + taps[k] * w[k:k + 1, :]
    return out


def _lru_gate_dots(xc, wa_ref, wx_ref):
    blk = wa_ref.shape[1]
    xcb = xc.astype(BF16)
    ga, gx = [], []
    for q in range(xc.shape[1] // blk):
        xq = xcb[:, q * blk:(q + 1) * blk]
        ga.append(_dot(xq, wa_ref[q]))
        gx.append(_dot(xq, wx_ref[q]))
    return jnp.concatenate(ga, axis=1), jnp.concatenate(gx, axis=1)


def _lru_coeffs(xc, ga, gx, b_a, b_x, ncs):
    r_a = _sigmoid(ga + b_a)
    r_x = _sigmoid(gx + b_x)
    a = jnp.exp(ncs * r_a)
    b = jnp.sqrt(1.0 - a * a) * (r_x * xc)
    return a, b


def _scan_tile(a, b, sub):
    shift = 1
    while shift < SUBLANES:
        keep = sub >= shift
        a_sh = jnp.where(keep, pltpu.roll(a, shift, axis=0), 1.0)
        b_sh = jnp.where(keep, pltpu.roll(b, shift, axis=0), 0.0)
        b = b + a * b_sh
        a = a * a_sh
        shift *= 2
    return a, b


def _lru_prompt_conv(xcur, x_tail, sub, w, bias):
    n_tiles = xcur.shape[0] // SUBLANES
    tiles = [x_tail] + [xcur[v * SUBLANES:(v + 1) * SUBLANES] for v in range(n_tiles)]
    taps = []
    for k in range(CONV_WIDTH):
        back = CONV_WIDTH - 1 - k
        if back == 0:
            taps.append(xcur)
            continue
        rolled = [pltpu.roll(t, back, axis=0) for t in tiles]
        taps.append(jnp.concatenate(
            [jnp.where(sub >= back, rolled[v + 1], rolled[v]) for v in range(n_tiles)], axis=0))
    return _lru_conv(taps, w, bias), tiles[-1]


def _lru_prompt_scan(a, b, h_in, sub):
    hs = []
    for v in range(a.shape[0] // SUBLANES):
        rows = slice(v * SUBLANES, (v + 1) * SUBLANES)
        a_in, b_in = _scan_tile(a[rows], b[rows], sub)
        hv = a_in * h_in + b_in
        hs.append(hv)
        h_in = jnp.broadcast_to(hv[SUBLANES - 1:SUBLANES, :], hv.shape)
    return jnp.concatenate(hs, axis=0), h_in


def _lru_sample_kernel(x_ref, g_ref, cw_ref, cb_ref, wa_ref, ba_ref, wx_ref, bx_ref, lam_ref,
                       cst_ref, h0_ref, o_ref, hs_ref):
    nb = h0_ref.shape[0]
    ts = x_ref.shape[0] // nb
    w, bias = cw_ref[...], cb_ref[...]
    ncs = _neg_c_softplus(lam_ref[...])
    xs = [cst_ref[k] for k in range(CONV_WIDTH - 1)]
    xs += [x_ref[t * nb:(t + 1) * nb, :].astype(F32) for t in range(ts)]
    h = h0_ref[...]
    for t in range(ts):
        xc = _lru_conv(xs[t:t + CONV_WIDTH], w, bias)
        ga, gx = _lru_gate_dots(xc, wa_ref, wx_ref)
        a, b = _lru_coeffs(xc, ga, gx, ba_ref[...], bx_ref[...], ncs)
        h = a * h + b
        g = g_ref[t * nb:(t + 1) * nb, :].astype(F32)
        o_ref[t * nb:(t + 1) * nb, :] = (h * _silu(g)).astype(o_ref.dtype)
    hs_ref[...] = h


def _lru_sample(z, lru_params, cst_tm, h0, s, tb, cbw):
    conv_w, conv_b, w_a, b_a, w_x, b_x, lam = lru_params
    width = conv_w.shape[1]
    nb = h0.shape[0]
    blk = w_a.shape[1]
    n_c = width // cbw
    blk_s = s // tb
    row = lambda c: (0, c)
    gate_w = pl.BlockSpec((cbw // blk, blk, blk), lambda c: (c, 0, 0))
    return pl.pallas_call(
        _lru_sample_kernel,
        out_shape=(jax.ShapeDtypeStruct((tb, width), BF16), jax.ShapeDtypeStruct((nb, width), F32)),
        grid=(n_c,),
        in_specs=[
            pl.BlockSpec((tb, cbw), lambda c: (blk_s, c)),
            pl.BlockSpec((tb, cbw), lambda c: (blk_s, n_c + c)),
            pl.BlockSpec((CONV_WIDTH, cbw), row),
            pl.BlockSpec((1, cbw), row),
            gate_w,
            pl.BlockSpec((1, cbw), row),
            gate_w,
            pl.BlockSpec((1, cbw), row),
            pl.BlockSpec((1, cbw), row),
            pl.BlockSpec((CONV_WIDTH - 1, nb, cbw), lambda c: (0, 0, c)),
            pl.BlockSpec((nb, cbw), row),
        ],
        out_specs=(pl.BlockSpec((tb, cbw), row), pl.BlockSpec((nb, cbw), row)),
        compiler_params=_params(("arbitrary",)),
        name="lru_sample",
    )(z, z, conv_w, conv_b, w_a, b_a, w_x, b_x, lam, cst_tm, h0)


def _attn_prompt_kernel(sink_ref, q_ref, kp_ref, kc_ref, vp_ref, vc_ref, g0_ref, g1_ref, os_ref,
                        o_ref, *, n_q, n_kv, group):
    i = pl.program_id(0)

    @pl.when(i < n_q)
    def _():
        _attn_prompt_block(i, sink_ref, q_ref, kp_ref, kc_ref, vp_ref, vc_ref, g0_ref, g1_ref,
                           o_ref, n_kv, group)

    @pl.when(i >= n_q)
    def _():
        o_ref[...] = os_ref[...]


def _attn_prompt_block(i, sink_ref, q_ref, kp_ref, kc_ref, vp_ref, vc_ref, g0_ref, g1_ref, o_ref,
                       n_kv, group):
    qb = q_ref.shape[0]
    aw_half = g0_ref.shape[1]
    scale = HEAD_DIM ** -0.5
    hw = group * HEAD_DIM
    c = lax.broadcasted_iota(jnp.int32, (2 * qb, qb), 0)
    r = lax.broadcasted_iota(jnp.int32, (2 * qb, qb), 1)
    rel = qb + r - c
    mask = (rel >= 0) & (rel <= WINDOW) & ((c >= qb) | (i > 0))
    lane = lax.broadcasted_iota(jnp.int32, (qb, LANES), 1)
    half_mask = (lane < HEAD_DIM, lane >= HEAD_DIM)
    zero = jnp.zeros((qb, LANES), BF16)
    n_pairs = n_kv // 2
    kv = {}
    out_t = {}

    def heads_of(p, variant):
        return [(2 * p + half, g) for half in range(2) for g in range(group)
                if ((g % 2) == half) == (variant == 0)]

    def load_pair(p):
        lanes = slice(p * LANES, (p + 1) * LANES)
        k_f32 = jnp.concatenate([kp_ref[:, lanes], kc_ref[:, lanes]], axis=0).astype(F32) * scale
        v_pair = jnp.concatenate([vp_ref[:, lanes], vc_ref[:, lanes]], axis=0)
        kv[p] = (k_f32.astype(BF16), pltpu.roll(k_f32, HEAD_DIM, axis=1).astype(BF16),
                 v_pair.astype(F32).T.astype(BF16))

    def scores(p, variant):
        q_rows = []
        for kh, g in heads_of(p, variant):
            c0 = kh * hw + (g // 2) * LANES
            q_rows.append(jnp.where(half_mask[g % 2], q_ref[:, c0:c0 + LANES], zero))
        return _dot_nt(kv[p][variant], jnp.concatenate(q_rows, axis=0))

    def softmax_values(p, variant, s_t):
        heads = heads_of(p, variant)
        p_blocks, inv_den = [], []
        for b, (kh, g) in enumerate(heads):
            sink = sink_ref[kh * group + g]
            sb = jnp.where(mask, s_t[:, b * qb:(b + 1) * qb], NEG_INF)
            m = jnp.maximum(jnp.max(sb, axis=0, keepdims=True), sink)
            pe = jnp.exp(sb - m)
            den = jnp.sum(pe, axis=0, keepdims=True) + jnp.exp(sink - m)
            p_blocks.append(pe.astype(BF16))
            inv_den.append(1.0 / den)
        o_t = _dot(kv[p][2], jnp.concatenate(p_blocks, axis=1))
        for b, (kh, g) in enumerate(heads):
            r0 = (kh % 2) * HEAD_DIM
            out_t[(kh, g)] = o_t[r0:r0 + HEAD_DIM, b * qb:(b + 1) * qb] * inv_den[b]

    def store_pair(p):
        for half in range(2):
            kh = 2 * p + half
            for j in range(group // 2):
                c0 = kh * hw + j * LANES
                tile = jnp.concatenate([out_t[(kh, 2 * j)], out_t[(kh, 2 * j + 1)]], axis=0).T
                g_ref, gc = (g0_ref, c0) if c0 < aw_half else (g1_ref, c0 - aw_half)
                gate = g_ref[:, gc:gc + LANES].astype(F32)
                o_ref[:, c0:c0 + LANES] = (tile * _silu(gate)).astype(o_ref.dtype)

    groups = [(p, variant) for p in range(n_pairs) for variant in range(2)]
    load_pair(0)
    s_next = scores(*groups[0])
    for k, (p, variant) in enumerate(groups):
        s_cur = s_next
        if k + 1 < len(groups):
            if groups[k + 1][0] != p:
                load_pair(groups[k + 1][0])
            s_next = scores(*groups[k + 1])
        softmax_values(p, variant, s_cur)
        if variant == 1:
            store_pair(p)


def _attn_prompt(z, sinks, o_b_s, s, width, aw, kvw, n_kv, group):
    qb = WINDOW
    n_q = s // qb
    assert s % qb == 0 and o_b_s.shape[0] % qb == 0 and n_kv % 2 == 0 and group % 2 == 0
    q_blk = (2 * width) // aw
    k_blk = (2 * width + aw) // kvw
    v_blk = k_blk + 1
    g_blk = (2 * width + aw + 2 * kvw) // (aw // 2)
    assert (2 * width) % aw == 0 and (2 * width + aw) % kvw == 0
    assert (2 * width + aw + 2 * kvw) % (aw // 2) == 0
    cur = lambda i: jnp.minimum(i, n_q - 1)
    prev = lambda i: jnp.maximum(cur(i) - 1, 0)
    return pl.pallas_call(
        functools.partial(_attn_prompt_kernel, n_q=n_q, n_kv=n_kv, group=group),
        out_shape=jax.ShapeDtypeStruct((s + o_b_s.shape[0], aw), BF16),
        grid=(n_q + o_b_s.shape[0] // qb,),
        in_specs=[
            pl.BlockSpec(memory_space=pltpu.SMEM),
            pl.BlockSpec((qb, aw), lambda i: (cur(i), q_blk)),
            pl.BlockSpec((qb, kvw), lambda i: (prev(i), k_blk)),
            pl.BlockSpec((qb, kvw), lambda i: (cur(i), k_blk)),
            pl.BlockSpec((qb, kvw), lambda i: (prev(i), v_blk)),
            pl.BlockSpec((qb, kvw), lambda i: (cur(i), v_blk)),
            pl.BlockSpec((qb, aw // 2), lambda i: (cur(i), g_blk)),
            pl.BlockSpec((qb, aw // 2), lambda i: (cur(i), g_blk + 1)),
            pl.BlockSpec((qb, aw), lambda i: (jnp.maximum(i - n_q, 0), 0)),
        ],
        out_specs=pl.BlockSpec((qb, aw), lambda i: (i, 0)),
        compiler_params=_params(("arbitrary",)),
        name="attn_prompt",
    )(sinks, z, z, z, z, z, z, z, o_b_s)


def _attn_sample_kernel(q_ref, kn_ref, vn_ref, ck_ref, cv_ref, g_ref, sink_ref, o_ref,
                        s_ref, p_ref, *, n_kv, group, ts):
    sb, tq, _ = q_ref.shape
    wb = ck_ref.shape[1]
    tk = kn_ref.shape[1]
    n_keys = s_ref.shape[1]
    n_pairs = n_kv // 2
    pair_rows = 2 * group * tq
    seq_rows = n_pairs * pair_rows
    scale = HEAD_DIM ** -0.5
    hw = group * HEAD_DIM
    lane = lax.broadcasted_iota(jnp.int32, (tq, LANES), 1)
    key_pad = jnp.zeros((n_keys - wb - tk, LANES), F32)

    def keys_of(c_ref, n_ref, n, lanes):
        return jnp.concatenate([c_ref[n, :, lanes], n_ref[n, :, lanes], key_pad], axis=0).astype(BF16)

    def score_body(n, carry):
        operands = []
        for p in range(n_pairs):
            lanes = slice(p * LANES, (p + 1) * LANES)
            pieces = []
            for half in range(2):
                for g in range(group):
                    c0 = (2 * p + half) * hw + (g // 2) * LANES
                    tile = q_ref[n, :, c0:c0 + LANES] * scale
                    if (g % 2) != half:
                        tile = pltpu.roll(tile, HEAD_DIM, axis=1)
                    keep = (lane >= HEAD_DIM * half) & (lane < HEAD_DIM * (half + 1))
                    pieces.append(jnp.where(keep, tile, 0.0))
            operands.append((jnp.concatenate(pieces, axis=0).astype(BF16),
                             keys_of(ck_ref, kn_ref, n, lanes)))
        products = [_dot_nt(lhs, keys) for lhs, keys in operands]
        for p in range(n_pairs):
            r0 = pl.multiple_of(n * seq_rows + p * pair_rows, pair_rows)
            s_ref[pl.ds(r0, pair_rows), :] = products[p]
        return carry

    lax.fori_loop(0, sb, score_body, 0)

    t = lax.broadcasted_iota(jnp.int32, (tq, n_keys), 0)
    c = lax.broadcasted_iota(jnp.int32, (tq, n_keys), 1)
    mask8 = ((c < wb) & (t + wb - c <= WINDOW)) | ((c >= wb) & (c - wb <= t) & (c - wb < ts))
    reps = sb * seq_rows // tq
    mask = jnp.concatenate([mask8] * reps, axis=0)
    sink = jnp.concatenate([sink_ref[:, 0:1]] * sb, axis=0)
    s = jnp.where(mask, s_ref[...], NEG_INF)
    m = jnp.maximum(jnp.max(s, axis=-1, keepdims=True), sink)
    pe = jnp.exp(s - m)
    den = jnp.sum(pe, axis=-1, keepdims=True) + jnp.exp(sink - m)
    p_ref[...] = (pe / den).astype(p_ref.dtype)

    def value_body(n, carry):
        operands = []
        for p in range(n_pairs):
            lanes = slice(p * LANES, (p + 1) * LANES)
            r0 = pl.multiple_of(n * seq_rows + p * pair_rows, pair_rows)
            operands.append((p_ref[pl.ds(r0, pair_rows), :], keys_of(cv_ref, vn_ref, n, lanes)))
        products = [_dot(probs, values) for probs, values in operands]
        for p in range(n_pairs):
            o = products[p]
            for half in range(2):
                kh = 2 * p + half
                for j in range(group // 2):
                    rows = (half * group + 2 * j) * tq
                    o_lo, o_hi = o[rows:rows + tq], o[rows + tq:rows + 2 * tq]
                    if half == 0:
                        o_hi = pltpu.roll(o_hi, HEAD_DIM, axis=1)
                    else:
                        o_lo = pltpu.roll(o_lo, HEAD_DIM, axis=1)
                    c0 = kh * hw + j * LANES
                    gate = g_ref[n, :, c0:c0 + LANES]
                    o_ref[n, :, c0:c0 + LANES] = jnp.where(lane < HEAD_DIM, o_lo, o_hi) * _silu(gate)
        return carry

    lax.fori_loop(0, sb, value_body, 0)


def _attn_sample(q8, kn, vn, cache_k, cache_v, g8, sink_rows, n_kv, group, ts):
    nb, tq, aw = q8.shape
    tk = kn.shape[1]
    wb, kvw = cache_k.shape[1], cache_k.shape[2]
    sb = _largest_divisor(nb, (8, 4, 2, 1))
    n_keys = -(-(wb + tk) // (2 * LANES)) * (2 * LANES)
    rows = sb * (n_kv // 2) * 2 * group * tq
    assert sink_rows.shape[0] * sb == rows
    blk3 = lambda r, width: pl.BlockSpec((sb, r, width), lambda i: (i, 0, 0))
    return pl.pallas_call(
        functools.partial(_attn_sample_kernel, n_kv=n_kv, group=group, ts=ts),
        out_shape=jax.ShapeDtypeStruct((nb, tq, aw), F32),
        grid=(nb // sb,),
        in_specs=[
            blk3(tq, aw), blk3(tk, kvw), blk3(tk, kvw), blk3(wb, kvw), blk3(wb, kvw), blk3(tq, aw),
            pl.BlockSpec(sink_rows.shape, lambda i: (0, 0)),
        ],
        out_specs=blk3(tq, aw),
        scratch_shapes=[pltpu.VMEM((rows, n_keys), F32), pltpu.VMEM((rows, n_keys), BF16)],
        compiler_params=_params(("arbitrary",)),
        name="attn_sample",
    )(q8, kn, vn, cache_k, cache_v, g8, sink_rows)


def _branch_lru_kernel(x_ref, g_ref, cw_ref, cb_ref, wa_ref, wx_ref, ba_ref, bx_ref, lam_ref,
                       oas_ref, ob_ref, w_ref, ma_ref, mb_ref,
                       o_ref, hp_ref, oa_ref, xt_ref, hc_ref, xc_ref, ga_ref, gx_ref,
                       *, n_p, n_t, chunk):
    u = pl.program_id(0)
    tm, cbw = x_ref.shape

    @pl.when(u == 0)
    def _():
        for ref in (oa_ref, xt_ref, hc_ref, xc_ref, ga_ref, gx_ref):
            ref[...] = jnp.zeros_like(ref)

    unit_t = jnp.maximum(u - 1, 0)
    row_t, ch_t = lax.div(unit_t, n_t), lax.rem(unit_t, n_t)
    live_t = (u >= 1) & (unit_t < n_p * n_t)
    row_d = lax.div(jnp.maximum(u - 1 - n_t, 0), n_t)
    ch_c = lax.rem(u, n_t)
    live_c = u < n_p * n_t

    slot_d = lax.rem(row_d, 2)
    width = n_t * cbw
    tn = o_ref.shape[1]
    n_chunks = tm // chunk
    pieces = [(rh, ch) for rh in range(2) for ch in range(2)]
    chunks_per_piece = n_chunks // len(pieces)
    assert chunks_per_piece * len(pieces) == n_chunks

    def product_piece(rh, ch):
        rows = slice(rh * tm // 2, (rh + 1) * tm // 2)
        cols = slice(ch * tn // 2, (ch + 1) * tn // 2)
        o_a = jnp.concatenate([oa_ref[slot_d, q, rows, :] for q in range(n_t)], axis=1)
        pa = _dot(o_a, w_ref[0:width, cols])
        pb = _dot(ob_ref[rows, :], w_ref[width:, cols])
        ma = ma_ref[rows, cols].astype(F32)
        mb = mb_ref[rows, cols].astype(F32)
        o_ref[rows, cols] = (_sigmoid(ma) * pa + _sigmoid(mb) * pb).astype(o_ref.dtype)

    sub = lax.broadcasted_iota(jnp.int32, (SUBLANES, cbw), 0)
    stage_w = lax.rem(u, 2)
    stage_r = 1 - stage_w

    slot_t = lax.rem(row_t, 2)
    cg = min(cbw, 2 * LANES)
    groups = [slice(k * cg, (k + 1) * cg) for k in range(cbw // cg)]
    sub_g = lax.broadcasted_iota(jnp.int32, (SUBLANES, cg), 0)
    coef = [(ba_ref[:, gs], bx_ref[:, gs], _neg_c_softplus(lam_ref[:, gs])) for gs in groups]
    h_ins = [hc_ref[ch_t, :, gs] for gs in groups]
    for c in range(n_chunks):
        if c % chunks_per_piece == 0:
            product_piece(*pieces[c // chunks_per_piece])
        rows = slice(c * chunk, (c + 1) * chunk)
        for k, gs in enumerate(groups):
            a, b = _lru_coeffs(xc_ref[stage_r, rows, gs], ga_ref[stage_r, rows, gs],
                               gx_ref[stage_r, rows, gs], *coef[k])
            h, h_ins[k] = _lru_prompt_scan(a, b, h_ins[k], sub_g)
            o_a_new = (h * _silu(g_ref[rows, gs].astype(F32))).astype(oa_ref.dtype)
            oa_ref[slot_t, ch_t, rows, gs] = jnp.where(live_t, o_a_new, oas_ref[rows, gs])
    for k, gs in enumerate(groups):
        h_in = jnp.where(live_t, h_ins[k], hc_ref[ch_t, :, gs])
        hc_ref[ch_t, :, gs] = h_in
        hp_ref[ch_t, :, gs] = h_in[0:1, :]

    w, bias = cw_ref[...], cb_ref[...]
    x_tail = xt_ref[ch_c]
    xcs = []
    for c in range(n_chunks):
        rows = slice(c * chunk, (c + 1) * chunk)
        xc, x_tail = _lru_prompt_conv(x_ref[rows, :].astype(F32), x_tail, sub, w, bias)
        xcs.append(xc)
    xt_ref[ch_c] = jnp.where(live_c, x_tail, xt_ref[ch_c])
    xc_all = jnp.concatenate(xcs, axis=0)
    ga, gx = _lru_gate_dots(xc_all, wa_ref, wx_ref)
    xc_ref[stage_w] = xc_all
    ga_ref[stage_w] = ga
    gx_ref[stage_w] = gx


def _branch_lru(z, lru_params, o_a_s, o_b, w_branch_bf16, ma_col, s, tm, tn):
    conv_w, conv_b, w_a, b_a, w_x, b_x, lam = lru_params
    m = z.shape[0]
    width = conv_w.shape[1]
    blk = w_a.shape[1]
    aw = o_b.shape[1]
    d = w_branch_bf16.shape[1]
    n_p = s // tm
    n_r = m // tm
    n_t = d // tn
    cbw = width // n_t
    assert cbw % blk == 0 and cbw % LANES == 0
    chunk = tm // 8
    assert chunk % SUBLANES == 0
    ma_blk = ma_col // tn
    mb_blk = (ma_col + d) // tn
    row_c = lambda u: jnp.minimum(u // n_t, n_p - 1)
    ch_c = lambda u: u % n_t
    unit_t = lambda u: jnp.maximum(u - 1, 0)
    row_t = lambda u: jnp.minimum(unit_t(u) // n_t, n_p - 1)
    ch_t = lambda u: unit_t(u) % n_t
    unit_d = lambda u: jnp.maximum(u - 1 - n_t, 0)
    row_d = lambda u: unit_d(u) // n_t
    tile_d = lambda u: unit_d(u) % n_t
    gate_w = pl.BlockSpec((cbw // blk, blk, blk), lambda u: (ch_c(u), 0, 0))
    chan_t = lambda rows: pl.BlockSpec((rows, cbw), lambda u: (0, ch_t(u)))
    return pl.pallas_call(
        functools.partial(_branch_lru_kernel, n_p=n_p, n_t=n_t, chunk=chunk),
        out_shape=(jax.ShapeDtypeStruct((m, d), BF16), jax.ShapeDtypeStruct((n_t, 1, cbw), F32)),
        grid=((n_r + 1) * n_t + 1,),
        in_specs=[
            pl.BlockSpec((tm, cbw), lambda u: (row_c(u), ch_c(u))),
            pl.BlockSpec((tm, cbw), lambda u: (row_t(u), n_t + ch_t(u))),
            pl.BlockSpec((CONV_WIDTH, cbw), lambda u: (0, ch_c(u))),
            pl.BlockSpec((1, cbw), lambda u: (0, ch_c(u))),
            gate_w,
            gate_w,
            chan_t(1),
            chan_t(1),
            chan_t(1),
            chan_t(tm),
            pl.BlockSpec((tm, aw), lambda u: (row_d(u), 0)),
            pl.BlockSpec((width + aw, tn), lambda u: (0, tile_d(u))),
            pl.BlockSpec((tm, tn), lambda u: (row_d(u), ma_blk + tile_d(u))),
            pl.BlockSpec((tm, tn), lambda u: (row_d(u), mb_blk + tile_d(u))),
        ],
        out_specs=(pl.BlockSpec((tm, tn), lambda u: (row_d(u), tile_d(u))),
                   pl.BlockSpec((n_t, 1, cbw), lambda u: (0, 0, 0))),
        scratch_shapes=[
            pltpu.VMEM((2, n_t, tm, cbw), BF16),
            pltpu.VMEM((n_t, SUBLANES, cbw), F32),
            pltpu.VMEM((n_t, SUBLANES, cbw), F32),
            pltpu.VMEM((2, tm, cbw), F32),
            pltpu.VMEM((2, tm, cbw), F32),
            pltpu.VMEM((2, tm, cbw), F32),
        ],
        compiler_params=_params(("arbitrary",)),
        name="branch_lru",
    )(z, z, conv_w, conv_b, w_a, w_x, b_a, b_x, lam, o_a_s, o_b, w_branch_bf16, z, z)


def _out_kernel(m_ref, w_ref, g_ref, x_ref, y_ref, acc_ref, ss_ref, *, d):
    i = pl.program_id(0)
    j = pl.program_id(1)
    slot = lax.rem(i, 2)
    prev = 1 - slot

    @pl.when((i == 0) & (j == 0))
    def _():
        acc_ref[1] = jnp.zeros(acc_ref.shape[1:], F32)
        ss_ref[...] = jnp.zeros_like(ss_ref)

    t = _dot(m_ref[...], w_ref[...])
    acc_ref[slot, j] = t
    ssq = jnp.sum(t * t, axis=-1, keepdims=True)
    ss_ref[slot] = jnp.where(j == 0, ssq, ss_ref[slot] + ssq)

    inv = lax.rsqrt(ss_ref[prev] / d + RMS_EPS)
    y_ref[...] = x_ref[...] + (acc_ref[prev, j] * inv) * g_ref[...]


def _out_proj(merged, w_out_bf16, g, x, row0, tm, tn):
    rows, d = x.shape
    n_r = rows // tm
    n_t = d // tn
    blk0 = row0 // tm
    xy_map = lambda i, j: (jnp.maximum(i - 1, 0), jnp.where(i == 0, 0, j))
    return pl.pallas_call(
        functools.partial(_out_kernel, d=d),
        out_shape=jax.ShapeDtypeStruct((rows, d), F32),
        grid=(n_r + 1, n_t),
        in_specs=[
            pl.BlockSpec((tm, d), lambda i, j: (blk0 + jnp.minimum(i, n_r - 1), 0)),
            pl.BlockSpec((d, tn), lambda i, j: (0, jnp.where(i == n_r, n_t - 1, j))),
            pl.BlockSpec((1, tn), lambda i, j: (0, j)),
            pl.BlockSpec((tm, tn), xy_map),
        ],
        out_specs=pl.BlockSpec((tm, tn), xy_map),
        scratch_shapes=[pltpu.VMEM((2, n_t, tm, tn), F32), pltpu.VMEM((2, tm, 1), F32)],
        compiler_params=_params(("arbitrary", "arbitrary")),
        name="out_proj",
    )(merged, w_out_bf16, g, x)


def _pad_rows(x, rows):
    return jnp.pad(x, ((0, 0), (0, rows - x.shape[1]), (0, 0)))


def _layer(xp, xs_tm, conv_state, h_state, cache_k, cache_v, norm_pre, norm_post, w_in, conv_w,
           conv_b, w_a, b_a, w_x, b_x, lam, sinks, w_branch, w_out):
    s, d = xp.shape
    nb, wb, n_kv, _ = cache_k.shape
    ts = xs_tm.shape[0] // nb
    width = conv_w.shape[1]
    n_heads = sinks.shape[0]
    group = n_heads // n_kv
    aw, kvw = n_heads * HEAD_DIM, n_kv * HEAD_DIM
    m = s + nb * ts
    tm = nb * ts
    assert s % tm == 0 and tm % SUBLANES == 0 and ts >= CONV_WIDTH - 1
    q_col = 2 * width
    k_col = q_col + aw
    v_col = k_col + kvw
    g_col = v_col + kvw
    ma_col = g_col + aw
    assert w_in.shape[1] == ma_col + 2 * d

    xn = _rmsnorm_pre(xp, xs_tm, norm_pre.reshape(1, d), _largest_divisor(tm, (256, 128, 64, 8)))
    z, (w_branch_bf16, w_out_bf16) = _in_proj(xn, w_in, (w_branch, w_out))

    tn = _largest_divisor(ma_col, (512, 256, 128))
    assert d % tn == 0 and width % (d // tn) == 0
    lru_params = (conv_w, conv_b.reshape(1, width), w_a.astype(BF16), b_a.reshape(1, width),
                  w_x.astype(BF16), b_x.reshape(1, width), lam.reshape(1, width))
    o_a_s, h_s = _lru_sample(z, lru_params, conv_state.transpose(1, 0, 2), h_state, s, tm,
                             width // (d // tn))

    def sample_cols(c0, c1):
        blk = lax.slice(z, (s, c0), (m, c1)).reshape(ts, nb, c1 - c0)
        return blk.transpose(1, 0, 2).astype(F32)

    q_s, k_s, v_s = sample_cols(q_col, k_col), sample_cols(k_col, v_col), sample_cols(v_col, g_col)
    g_s = sample_cols(g_col, ma_col)
    tq = -(-ts // SUBLANES) * SUBLANES
    tk = -(-ts // BF16_ROWS) * BF16_ROWS
    ck = cache_k.reshape(nb, wb, kvw)
    cv = cache_v.reshape(nb, wb, kvw)
    sink_rows = jnp.broadcast_to(sinks.reshape(n_heads, 1, 1), (n_heads, tq, LANES))
    sink_rows = sink_rows.reshape(n_heads * tq, LANES)
    o_b_s = _attn_sample(_pad_rows(q_s, tq), _pad_rows(k_s, tk), _pad_rows(v_s, tk), ck, cv,
                         _pad_rows(g_s, tq), sink_rows, n_kv, group, ts)
    o_b_s = o_b_s[:, :ts].transpose(1, 0, 2).reshape(nb * ts, aw).astype(BF16)
    o_b = _attn_prompt(z, sinks, o_b_s, s, width, aw, kvw, n_kv, group)

    merged, h_p = _branch_lru(z, lru_params, o_a_s, o_b, w_branch_bf16, ma_col, s, tm, tn)
    h_p = h_p.reshape(1, width)
    g_post = norm_post.reshape(1, d)
    tn_out = _largest_divisor(d, (1024, 512, 256, 128))
    y_p = _out_proj(merged, w_out_bf16, g_post, xp, 0, tm, tn_out)
    y_s = _out_proj(merged, w_out_bf16, g_post, xs_tm, s, tm, tn_out)

    keep = CONV_WIDTH - 1
    new_conv_p = lax.slice(z, (s - keep, 0), (s, width)).astype(F32)[None]
    wbp = min(WINDOW, s)
    new_k_p = lax.slice(z, (s - wbp, k_col), (s, v_col)).astype(F32).reshape(1, wbp, n_kv, HEAD_DIM)
    new_v_p = lax.slice(z, (s - wbp, v_col), (s, g_col)).astype(F32).reshape(1, wbp, n_kv, HEAD_DIM)
    x_lru_s = sample_cols(0, width)
    new_conv_s = jnp.concatenate([conv_state, x_lru_s], axis=1)[:, -keep:]
    new_k_s = jnp.concatenate([ck, k_s], axis=1)[:, -wb:].reshape(nb, wb, n_kv, HEAD_DIM)
    new_v_s = jnp.concatenate([cv, v_s], axis=1)[:, -wb:].reshape(nb, wb, n_kv, HEAD_DIM)
    return y_p, y_s, (new_conv_p, h_p, new_k_p, new_v_p), (new_conv_s, h_s, new_k_s, new_v_s)


def kernel(x_prompt, x_sample, state_conv, state_h, cache_k_win, cache_v_win, norm_pre, norm_post, w_in, conv_w, conv_b, lru_w_a, lru_b_a, lru_w_x, lru_b_x, lru_lambda, attn_sinks, w_branch, w_out):
    batch, s, d = x_prompt.shape
    nb, ts, _ = x_sample.shape
    assert batch == 1, "the prompt group is a single sequence"
    depth = w_in.shape[0]
    xp = x_prompt.reshape(s, d)
    xs_tm = x_sample.transpose(1, 0, 2).reshape(ts * nb, d)
    p_states, s_states = [], []
    for l in range(depth):
        xp, xs_tm, p_new, s_new = _layer(
            xp, xs_tm, state_conv[l], state_h[l], cache_k_win[l], cache_v_win[l], norm_pre[l],
            norm_post[l], w_in[l], conv_w[l], conv_b[l], lru_w_a[l], lru_b_a[l], lru_w_x[l],
            lru_b_x[l], lru_lambda[l], attn_sinks[l], w_branch[l], w_out[l])
        p_states.append(p_new)
        s_states.append(s_new)
    y_prompt = xp.reshape(1, s, d)
    y_sample = xs_tm.reshape(ts, nb, d).transpose(1, 0, 2)
    stack = lambda states, k: jnp.stack([st[k] for st in states])
    return (y_prompt, y_sample,
            stack(p_states, 0), stack(p_states, 1), stack(p_states, 2), stack(p_states, 3),
            stack(s_states, 0), stack(s_states, 1), stack(s_states, 2), stack(s_states, 3))
```

```python
import functools

import jax
import jax.numpy as jnp
from jax import lax
from jax.experimental import pallas as pl
from jax.experimental.pallas import tpu as pltpu

HEAD_DIM = 64
WINDOW = 128
CONV_WIDTH = 4
LRU_C = 8.0
RMS_EPS = 1e-6
NEG_INF = -1e30

LANES = 128
SUBLANES = 8
BF16_ROWS = 16
VMEM_LIMIT_BYTES = 58 * 1024 * 1024
IN_PROJ_VMEM_BUDGET = 52 * 1024 * 1024
IN_PROJ_VMEM_LIMIT_BYTES = 63 * 1024 * 1024

F32 = jnp.float32
BF16 = jnp.bfloat16


def _largest_divisor(n, candidates):
    for c in candidates:
        if n % c == 0:
            return c
    raise ValueError(f"no tile in {candidates} divides {n}")


def _params(sem, vmem_limit_bytes=VMEM_LIMIT_BYTES):
    return pltpu.CompilerParams(dimension_semantics=sem, vmem_limit_bytes=vmem_limit_bytes)


def _dot(a, b):
    return jnp.dot(a, b, preferred_element_type=F32)


def _dot_nt(a, b):
    return lax.dot_general(a, b, (((1,), (1,)), ((), ())), preferred_element_type=F32)


def _sigmoid(x):
    return jax.nn.sigmoid(x)


def _silu(x):
    return x * jax.nn.sigmoid(x)


def _rmsnorm_kernel(xp_ref, xs_ref, g_ref, o_ref, *, n_p):
    i = pl.program_id(0)

    def body(x_ref):
        x = x_ref[...]
        ms = jnp.mean(x * x, axis=-1, keepdims=True)
        o_ref[...] = ((x * lax.rsqrt(ms + RMS_EPS)) * g_ref[...]).astype(o_ref.dtype)

    @pl.when(i < n_p)
    def _():
        body(xp_ref)

    @pl.when(i >= n_p)
    def _():
        body(xs_ref)


def _rmsnorm_pre(xp, xs, g, tm):
    s, d = xp.shape
    ms = xs.shape[0]
    n_p, n_s = s // tm, ms // tm
    return pl.pallas_call(
        functools.partial(_rmsnorm_kernel, n_p=n_p),
        out_shape=jax.ShapeDtypeStruct((s + ms, d), BF16),
        grid=(n_p + n_s,),
        in_specs=[
            pl.BlockSpec((tm, d), lambda i: (jnp.minimum(i, n_p - 1), 0)),
            pl.BlockSpec((tm, d), lambda i: (jnp.maximum(i - n_p, 0), 0)),
            pl.BlockSpec((1, d), lambda i: (0, 0)),
        ],
        out_specs=pl.BlockSpec((tm, d), lambda i: (i, 0)),
        compiler_params=_params(("arbitrary",)),
        name="rmsnorm_pre",
    )(xp, xs, g)


def _cast_rows(src_ref, dst_ref, rows_per_step=256):
    n = src_ref.shape[0]
    step = _largest_divisor(n, (rows_per_step, 128, 64, 32, 16))

    def body(r, carry):
        r0 = pl.multiple_of(r * step, step)
        dst_ref[pl.ds(r0, step), :] = src_ref[pl.ds(r0, step), :].astype(dst_ref.dtype)
        return carry

    lax.fori_loop(0, n // step, body, 0)


def _in_proj_kernel(x_ref, w_ref, *rest, n_weights, n_caches, shift):
    weights_in, rest = rest[:n_weights], rest[n_weights:]
    caches_in, rest = rest[:n_caches], rest[n_caches:]
    o_ref, rest = rest[0], rest[1:]
    weights_out, rest = rest[:n_weights], rest[n_weights:]
    caches_out, rest = rest[:n_caches], rest[n_caches:]
    wb_ref, sem = rest
    j, i = pl.program_id(0), pl.program_id(1)

    def shift_copies(k):
        keep = caches_in[k].shape[1] - shift
        return (pltpu.make_async_copy(caches_in[k].at[:, pl.ds(shift, keep)],
                                      caches_out[k].at[:, pl.ds(0, keep)], sem.at[k, 0]),
                pltpu.make_async_copy(caches_in[k].at[:, pl.ds(0, shift)],
                                      caches_out[k].at[:, pl.ds(keep, shift)], sem.at[k, 1]))

    @pl.when((j == 0) & (i == 0))
    def _():
        for k in range(n_caches):
            for copy in shift_copies(k):
                copy.start()

    @pl.when(i == 0)
    def _():
        _cast_rows(w_ref, wb_ref)

    o_ref[...] = _dot(x_ref[...], wb_ref[...]).astype(o_ref.dtype)
    for src, dst in zip(weights_in, weights_out):
        dst[...] = src[...].astype(dst.dtype)

    @pl.when((j == pl.num_programs(0) - 1) & (i == pl.num_programs(1) - 1))
    def _():
        for k in range(n_caches):
            for copy in shift_copies(k):
                copy.wait()


def _in_proj_tiles(m, d, n):
    best = None
    for tm in range(BF16_ROWS, m + 1, BF16_ROWS):
        if m % tm:
            continue
        for tn in range(2 * LANES, n + 1, 2 * LANES):
            if n % tn:
                continue
            vmem = d * tn * (2 * 4 + 2) + 2 * tm * d * 2 + 2 * tm * tn * 2
            if vmem <= IN_PROJ_VMEM_BUDGET and (best is None or tm * tn > best[0] * best[1]):
                best = (tm, tn)
    assert best is not None
    return best


def _in_proj(xn, w_in, extra_weights=(), caches=(), shift=0):
    m, d = xn.shape
    n = w_in.shape[1]
    tm, tn = _in_proj_tiles(m, d, n)
    n_i = m // tm
    n_steps = (n // tn) * n_i
    extra_specs = []
    for w in extra_weights:
        rows = w.shape[0]
        slab = next(r for r in range(BF16_ROWS, rows + 1, BF16_ROWS)
                    if rows % r == 0 and rows // r <= n_steps)
        last = rows // slab - 1
        extra_specs.append(pl.BlockSpec(
            (slab, w.shape[1]), lambda j, i, last=last: (jnp.minimum(j * n_i + i, last), 0)))
    hbm = pl.BlockSpec(memory_space=pl.ANY)
    n_w, n_c = len(extra_weights), len(caches)
    out = pl.pallas_call(
        functools.partial(_in_proj_kernel, n_weights=n_w, n_caches=n_c, shift=shift),
        out_shape=(jax.ShapeDtypeStruct((m, n), BF16),
                   *[jax.ShapeDtypeStruct(w.shape, BF16) for w in extra_weights],
                   *[jax.ShapeDtypeStruct(c.shape, c.dtype) for c in caches]),
        grid=(n // tn, n_i),
        in_specs=[
            pl.BlockSpec((tm, d), lambda j, i: (i, 0)),
            pl.BlockSpec((d, tn), lambda j, i: (0, j)),
            *extra_specs,
            *[hbm] * n_c,
        ],
        out_specs=(pl.BlockSpec((tm, tn), lambda j, i: (i, j)), *extra_specs, *[hbm] * n_c),
        scratch_shapes=[pltpu.VMEM((d, tn), BF16), pltpu.SemaphoreType.DMA((max(n_c, 1), 2))],
        compiler_params=_params(("arbitrary", "arbitrary"), IN_PROJ_VMEM_LIMIT_BYTES),
        name="in_proj",
    )(xn, w_in, *extra_weights, *caches)
    return out[0], out[1:1 + n_w], out[1 + n_w:]


def _cache_append_kernel(cache_ref, rows_ref, o_ref):
    del cache_ref
    o_ref[...] = rows_ref[...]


def _cache_append(cache, rows):
    nb, wb = cache.shape[:2]
    ts = rows.shape[1]
    assert wb % ts == 0 and rows.shape[2:] == cache.shape[2:]
    sb = _largest_divisor(nb, (32, 16, 8, 4, 2, 1))
    tail = (sb, ts) + cache.shape[2:]
    zeros = (0,) * (cache.ndim - 2)
    return pl.pallas_call(
        _cache_append_kernel,
        out_shape=jax.ShapeDtypeStruct(cache.shape, cache.dtype),
        grid=(nb // sb,),
        in_specs=[pl.BlockSpec(memory_space=pl.ANY),
                  pl.BlockSpec(tail, lambda i: (i, 0) + zeros)],
        out_specs=pl.BlockSpec(tail, lambda i: (i, wb // ts - 1) + zeros),
        input_output_aliases={0: 0},
        compiler_params=_params(("arbitrary",)),
        name="cache_append",
    )(cache, rows)


LOG2_E = 1.4426950408889634


def _neg_c_softplus(lam):
    return (-LRU_C * LOG2_E) * (jnp.maximum(-lam, 0.0) + jnp.log(1.0 + jnp.exp(-jnp.abs(lam))))


def _lru_conv(taps, w, bias):
    out = bias
    for k in range(CONV_WIDTH):
        out = out + taps[k] * w[k:k + 1, :]
    return out


def _lru_gate_dots(xc, wa_ref, wx_ref):
    blk = wa_ref.shape[1]
    xcb = xc.astype(BF16)
    ga, gx = [], []
    for q in range(xc.shape[1] // blk):
        xq = xcb[:, q * blk:(q + 1) * blk]
        ga.append(_dot(xq, wa_ref[q]))
        gx.append(_dot(xq, wx_ref[q]))
    return jnp.concatenate(ga, axis=1), jnp.concatenate(gx, axis=1)


def _lru_coeffs(xc, ga, gx, b_a, b_x, ncs):
    r_a = _sigmoid(ga + b_a)
    r_x = _sigmoid(gx + b_x)
    a = jnp.exp2(ncs * r_a)
    b = jnp.exp2(0.5 * jnp.log2(1.0 - a * a)) * (r_x * xc)
    return a, b


def _scan_tile(a, b, sub):
    shift = 1
    while shift < SUBLANES:
        keep = sub >= shift
        a_sh = jnp.where(keep, pltpu.roll(a, shift, axis=0), 1.0)
        b_sh = jnp.where(keep, pltpu.roll(b, shift, axis=0), 0.0)
        b = b + a * b_sh
        a = a * a_sh
        shift *= 2
    return a, b


def _lru_prompt_conv(xcur, x_tail, sub, w, bias):
    n_tiles = xcur.shape[0] // SUBLANES
    tiles = [x_tail] + [xcur[v * SUBLANES:(v + 1) * SUBLANES] for v in range(n_tiles)]
    taps = []
    for k in range(CONV_WIDTH):
        back = CONV_WIDTH - 1 - k
        if back == 0:
            taps.append(xcur)
            continue
        rolled = [pltpu.roll(t, back, axis=0) for t in tiles]
        taps.append(jnp.concatenate(
            [jnp.where(sub >= back, rolled[v + 1], rolled[v]) for v in range(n_tiles)], axis=0))
    return _lru_conv(taps, w, bias), tiles[-1]


def _lru_prompt_scan(a, b, h_in, sub):
    hs = []
    for v in range(a.shape[0] // SUBLANES):
        rows = slice(v * SUBLANES, (v + 1) * SUBLANES)
        a_in, b_in = _scan_tile(a[rows], b[rows], sub)
        hv = a_in * h_in + b_in
        hs.append(hv)
        h_in = jnp.broadcast_to(hv[SUBLANES - 1:SUBLANES, :], hv.shape)
    return jnp.concatenate(hs, axis=0), h_in


def _lru_sample_kernel(x_ref, g_ref, cw_ref, cb_ref, wa_ref, ba_ref, wx_ref, bx_ref, lam_ref,
                       cst_ref, h0_ref, o_ref, hs_ref):
    nb = h0_ref.shape[0]
    ts = x_ref.shape[0] // nb
    w, bias = cw_ref[...], cb_ref[...]
    ncs = _neg_c_softplus(lam_ref[...])
    xs = [cst_ref[k] for k in range(CONV_WIDTH - 1)]
    xs += [x_ref[t * nb:(t + 1) * nb, :].astype(F32) for t in range(ts)]
    h = h0_ref[...]
    for t in range(ts):
        xc = _lru_conv(xs[t:t + CONV_WIDTH], w, bias)
        ga, gx = _lru_gate_dots(xc, wa_ref, wx_ref)
        a, b = _lru_coeffs(xc, ga, gx, ba_ref[...], bx_ref[...], ncs)
        h = a * h + b
        g = g_ref[t * nb:(t + 1) * nb, :].astype(F32)
        o_ref[t * nb:(t + 1) * nb, :] = (h * _silu(g)).astype(o_ref.dtype)
    hs_ref[...] = h


def _lru_sample(z, lru_params, cst_tm, h0, s, tb, cbw):
    conv_w, conv_b, w_a, b_a, w_x, b_x, lam = lru_params
    width = conv_w.shape[1]
    nb = h0.shape[0]
    blk = w_a.shape[1]
    n_c = width // cbw
    blk_s = s // tb
    row = lambda c: (0, c)
    gate_w = pl.BlockSpec((cbw // blk, blk, blk), lambda c: (c, 0, 0))
    return pl.pallas_call(
        _lru_sample_kernel,
        out_shape=(jax.ShapeDtypeStruct((tb, width), BF16), jax.ShapeDtypeStruct((nb, width), F32)),
        grid=(n_c,),
        in_specs=[
            pl.BlockSpec((tb, cbw), lambda c: (blk_s, c)),
            pl.BlockSpec((tb, cbw), lambda c: (blk_s, n_c + c)),
            pl.BlockSpec((CONV_WIDTH, cbw), row),
            pl.BlockSpec((1, cbw), row),
            gate_w,
            pl.BlockSpec((1, cbw), row),
            gate_w,
            pl.BlockSpec((1, cbw), row),
            pl.BlockSpec((1, cbw), row),
            pl.BlockSpec((CONV_WIDTH - 1, nb, cbw), lambda c: (0, 0, c)),
            pl.BlockSpec((nb, cbw), row),
        ],
        out_specs=(pl.BlockSpec((tb, cbw), row), pl.BlockSpec((nb, cbw), row)),
        compiler_params=_params(("arbitrary",)),
        name="lru_sample",
    )(z, z, conv_w, conv_b, w_a, b_a, w_x, b_x, lam, cst_tm, h0)


def _attn_prompt_kernel(sink_ref, q_ref, kp_ref, kc_ref, vp_ref, vc_ref, g0_ref, g1_ref, os_ref,
                        o_ref, *, n_q, n_kv, group):
    i = pl.program_id(0)

    @pl.when(i < n_q)
    def _():
        _attn_prompt_block(i, sink_ref, q_ref, kp_ref, kc_ref, vp_ref, vc_ref, g0_ref, g1_ref,
                           o_ref, n_kv, group)

    @pl.when(i >= n_q)
    def _():
        o_ref[...] = os_ref[...]


def _attn_prompt_block(i, sink_ref, q_ref, kp_ref, kc_ref, vp_ref, vc_ref, g0_ref, g1_ref, o_ref,
                       n_kv, group):
    qb = q_ref.shape[0]
    aw_half = g0_ref.shape[1]
    scale = HEAD_DIM ** -0.5
    hw = group * HEAD_DIM
    c = lax.broadcasted_iota(jnp.int32, (2 * qb, qb), 0)
    r = lax.broadcasted_iota(jnp.int32, (2 * qb, qb), 1)
    rel = qb + r - c
    mask = (rel >= 0) & (rel <= WINDOW) & ((c >= qb) | (i > 0))
    lane = lax.broadcasted_iota(jnp.int32, (qb, LANES), 1)
    half_mask = (lane < HEAD_DIM, lane >= HEAD_DIM)
    zero = jnp.zeros((qb, LANES), BF16)
    n_pairs = n_kv // 2
    kv = {}
    out_t = {}

    def heads_of(p, variant):
        return [(2 * p + half, g) for half in range(2) for g in range(group)
                if ((g % 2) == half) == (variant == 0)]

    def load_pair(p):
        lanes = slice(p * LANES, (p + 1) * LANES)
        k_f32 = jnp.concatenate([kp_ref[:, lanes], kc_ref[:, lanes]], axis=0).astype(F32) * scale
        v_pair = jnp.concatenate([vp_ref[:, lanes], vc_ref[:, lanes]], axis=0)
        kv[p] = (k_f32.astype(BF16), pltpu.roll(k_f32, HEAD_DIM, axis=1).astype(BF16),
                 v_pair.astype(F32).T.astype(BF16))

    def scores(p, variant):
        q_rows = []
        for kh, g in heads_of(p, variant):
            c0 = kh * hw + (g // 2) * LANES
            q_rows.append(jnp.where(half_mask[g % 2], q_ref[:, c0:c0 + LANES], zero))
        return _dot_nt(kv[p][variant], jnp.concatenate(q_rows, axis=0))

    def softmax_values(p, variant, s_t):
        heads = heads_of(p, variant)
        p_blocks, inv_den = [], []
        for b, (kh, g) in enumerate(heads):
            sink = sink_ref[kh * group + g]
            sb = jnp.where(mask, s_t[:, b * qb:(b + 1) * qb], NEG_INF)
            m = jnp.maximum(jnp.max(sb, axis=0, keepdims=True), sink)
            pe = jnp.exp(sb - m)
            den = jnp.sum(pe, axis=0, keepdims=True) + jnp.exp(sink - m)
            p_blocks.append(pe.astype(BF16))
            inv_den.append(1.0 / den)
        o_t = _dot(kv[p][2], jnp.concatenate(p_blocks, axis=1))
        for b, (kh, g) in enumerate(heads):
            r0 = (kh % 2) * HEAD_DIM
            out_t[(kh, g)] = o_t[r0:r0 + HEAD_DIM, b * qb:(b + 1) * qb] * inv_den[b]

    def store_pair(p):
        for half in range(2):
            kh = 2 * p + half
            for j in range(group // 2):
                c0 = kh * hw + j * LANES
                tile = jnp.concatenate([out_t[(kh, 2 * j)], out_t[(kh, 2 * j + 1)]], axis=0).T
                g_ref, gc = (g0_ref, c0) if c0 < aw_half else (g1_ref, c0 - aw_half)
                gate = g_ref[:, gc:gc + LANES].astype(F32)
                o_ref[:, c0:c0 + LANES] = (tile * _silu(gate)).astype(o_ref.dtype)

    groups = [(p, variant) for p in range(n_pairs) for variant in range(2)]
    load_pair(0)
    s_next = scores(*groups[0])
    for k, (p, variant) in enumerate(groups):
        s_cur = s_next
        if k + 1 < len(groups):
            if groups[k + 1][0] != p:
                load_pair(groups[k + 1][0])
            s_next = scores(*groups[k + 1])
        softmax_values(p, variant, s_cur)
        if variant == 1:
            store_pair(p)


def _attn_prompt(z, sinks, o_b_s, s, width, aw, kvw, n_kv, group):
    qb = WINDOW
    n_q = s // qb
    assert s % qb == 0 and o_b_s.shape[0] % qb == 0 and n_kv % 2 == 0 and group % 2 == 0
    q_blk = (2 * width) // aw
    k_blk = (2 * width + aw) // kvw
    v_blk = k_blk + 1
    g_blk = (2 * width + aw + 2 * kvw) // (aw // 2)
    assert (2 * width) % aw == 0 and (2 * width + aw) % kvw == 0
    assert (2 * width + aw + 2 * kvw) % (aw // 2) == 0
    cur = lambda i: jnp.minimum(i, n_q - 1)
    prev = lambda i: jnp.maximum(cur(i) - 1, 0)
    return pl.pallas_call(
        functools.partial(_attn_prompt_kernel, n_q=n_q, n_kv=n_kv, group=group),
        out_shape=jax.ShapeDtypeStruct((s + o_b_s.shape[0], aw), BF16),
        grid=(n_q + o_b_s.shape[0] // qb,),
        in_specs=[
            pl.BlockSpec(memory_space=pltpu.SMEM),
            pl.BlockSpec((qb, aw), lambda i: (cur(i), q_blk)),
            pl.BlockSpec((qb, kvw), lambda i: (prev(i), k_blk)),
            pl.BlockSpec((qb, kvw), lambda i: (cur(i), k_blk)),
            pl.BlockSpec((qb, kvw), lambda i: (prev(i), v_blk)),
            pl.BlockSpec((qb, kvw), lambda i: (cur(i), v_blk)),
            pl.BlockSpec((qb, aw // 2), lambda i: (cur(i), g_blk)),
            pl.BlockSpec((qb, aw // 2), lambda i: (cur(i), g_blk + 1)),
            pl.BlockSpec((qb, aw), lambda i: (jnp.maximum(i - n_q, 0), 0)),
        ],
        out_specs=pl.BlockSpec((qb, aw), lambda i: (i, 0)),
        compiler_params=_params(("arbitrary",)),
        name="attn_prompt",
    )(sinks, z, z, z, z, z, z, z, o_b_s)


def _attn_sample_kernel(q_ref, kn_ref, vn_ref, ck_ref, cv_ref, g_ref, sink_ref, o_ref,
                        s_ref, p_ref, *, n_kv, group, ts):
    sb, tq, _ = q_ref.shape
    wb = ck_ref.shape[1]
    tk = kn_ref.shape[1]
    n_keys = s_ref.shape[1]
    n_pairs = n_kv // 2
    pair_rows = 2 * group * tq
    seq_rows = n_pairs * pair_rows
    scale = HEAD_DIM ** -0.5
    hw = group * HEAD_DIM
    lane = lax.broadcasted_iota(jnp.int32, (tq, LANES), 1)
    key_pad = jnp.zeros((n_keys - wb - tk, LANES), F32)

    def keys_of(c_ref, n_ref, n, lanes):
        return jnp.concatenate([c_ref[n, :, lanes], n_ref[n, :, lanes], key_pad], axis=0).astype(BF16)

    def score_body(n, carry):
        operands = []
        for p in range(n_pairs):
            lanes = slice(p * LANES, (p + 1) * LANES)
            pieces = []
            for half in range(2):
                for g in range(group):
                    c0 = (2 * p + half) * hw + (g // 2) * LANES
                    tile = q_ref[n, :, c0:c0 + LANES] * scale
                    if (g % 2) != half:
                        tile = pltpu.roll(tile, HEAD_DIM, axis=1)
                    keep = (lane >= HEAD_DIM * half) & (lane < HEAD_DIM * (half + 1))
                    pieces.append(jnp.where(keep, tile, 0.0))
            operands.append((jnp.concatenate(pieces, axis=0).astype(BF16),
                             keys_of(ck_ref, kn_ref, n, lanes)))
        products = [_dot_nt(lhs, keys) for lhs, keys in operands]
        for p in range(n_pairs):
            r0 = pl.multiple_of(n * seq_rows + p * pair_rows, pair_rows)
            s_ref[pl.ds(r0, pair_rows), :] = products[p]
        return carry

    lax.fori_loop(0, sb, score_body, 0)

    t = lax.broadcasted_iota(jnp.int32, (tq, n_keys), 0)
    c = lax.broadcasted_iota(jnp.int32, (tq, n_keys), 1)
    mask8 = ((c < wb) & (t + wb - c <= WINDOW)) | ((c >= wb) & (c - wb <= t) & (c - wb < ts))
    reps = sb * seq_rows // tq
    mask = jnp.concatenate([mask8] * reps, axis=0)
    sink = jnp.concatenate([sink_ref[:, 0:1]] * sb, axis=0)
    s = jnp.where(mask, s_ref[...], NEG_INF)
    m = jnp.maximum(jnp.max(s, axis=-1, keepdims=True), sink)
    pe = jnp.exp(s - m)
    den = jnp.sum(pe, axis=-1, keepdims=True) + jnp.exp(sink - m)
    p_ref[...] = (pe / den).astype(p_ref.dtype)

    def value_body(n, carry):
        operands = []
        for p in range(n_pairs):
            lanes = slice(p * LANES, (p + 1) * LANES)
            r0 = pl.multiple_of(n * seq_rows + p * pair_rows, pair_rows)
            operands.append((p_ref[pl.ds(r0, pair_rows), :], keys_of(cv_ref, vn_ref, n, lanes)))
        products = [_dot(probs, values) for probs, values in operands]
        for p in range(n_pairs):
            o = products[p]
            for half in range(2):
                kh = 2 * p + half
                for j in range(group // 2):
                    rows = (half * group + 2 * j) * tq
                    o_lo, o_hi = o[rows:rows + tq], o[rows + tq:rows + 2 * tq]
                    if half == 0:
                        o_hi = pltpu.roll(o_hi, HEAD_DIM, axis=1)
                    else:
                        o_lo = pltpu.roll(o_lo, HEAD_DIM, axis=1)
                    c0 = kh * hw + j * LANES
                    gate = g_ref[n, :, c0:c0 + LANES]
                    o_ref[n, :, c0:c0 + LANES] = jnp.where(lane < HEAD_DIM, o_lo, o_hi) * _silu(gate)
        return carry

    lax.fori_loop(0, sb, value_body, 0)


def _attn_sample(q8, kn, vn, cache_k, cache_v, g8, sink_rows, n_kv, group, ts):
    nb, tq, aw = q8.shape
    tk = kn.shape[1]
    wb, kvw = cache_k.shape[1], cache_k.shape[2]
    sb = _largest_divisor(nb, (8, 4, 2, 1))
    n_keys = -(-(wb + tk) // (2 * LANES)) * (2 * LANES)
    rows = sb * (n_kv // 2) * 2 * group * tq
    assert sink_rows.shape[0] * sb == rows
    blk3 = lambda r, width: pl.BlockSpec((sb, r, width), lambda i: (i, 0, 0))
    return pl.pallas_call(
        functools.partial(_attn_sample_kernel, n_kv=n_kv, group=group, ts=ts),
        out_shape=jax.ShapeDtypeStruct((nb, tq, aw), F32),
        grid=(nb // sb,),
        in_specs=[
            blk3(tq, aw), blk3(tk, kvw), blk3(tk, kvw), blk3(wb, kvw), blk3(wb, kvw), blk3(tq, aw),
            pl.BlockSpec(sink_rows.shape, lambda i: (0, 0)),
        ],
        out_specs=blk3(tq, aw),
        scratch_shapes=[pltpu.VMEM((rows, n_keys), F32), pltpu.VMEM((rows, n_keys), BF16)],
        compiler_params=_params(("arbitrary",)),
        name="attn_sample",
    )(q8, kn, vn, cache_k, cache_v, g8, sink_rows)


def _branch_lru_kernel(x_ref, g_ref, cw_ref, cb_ref, wa_ref, wx_ref, ba_ref, bx_ref, lam_ref,
                       oas_ref, ob_ref, w_ref, ma_ref, mb_ref,
                       o_ref, hp_ref, oa_ref, xt_ref, hc_ref, xc_ref, ga_ref, gx_ref,
                       *, n_p, n_t, chunk):
    u = pl.program_id(0)
    tm, cbw = x_ref.shape

    @pl.when(u == 0)
    def _():
        for ref in (oa_ref, xt_ref, hc_ref, xc_ref, ga_ref, gx_ref):
            ref[...] = jnp.zeros_like(ref)

    unit_t = jnp.maximum(u - 1, 0)
    row_t, ch_t = lax.div(unit_t, n_t), lax.rem(unit_t, n_t)
    live_t = (u >= 1) & (unit_t < n_p * n_t)
    row_d = lax.div(jnp.maximum(u - 1 - n_t, 0), n_t)
    ch_c = lax.rem(u, n_t)
    live_c = u < n_p * n_t

    slot_d = lax.rem(row_d, 2)
    width = n_t * cbw
    tn = o_ref.shape[1]
    n_chunks = tm // chunk
    pieces = [(rh, ch) for rh in range(2) for ch in range(2)]
    chunks_per_piece = n_chunks // len(pieces)
    assert chunks_per_piece * len(pieces) == n_chunks

    def product_piece(rh, ch):
        rows = slice(rh * tm // 2, (rh + 1) * tm // 2)
        cols = slice(ch * tn // 2, (ch + 1) * tn // 2)
        o_a = jnp.concatenate([oa_ref[slot_d, q, rows, :] for q in range(n_t)], axis=1)
        pa = _dot(o_a, w_ref[0:width, cols])
        pb = _dot(ob_ref[rows, :], w_ref[width:, cols])
        ma = ma_ref[rows, cols].astype(F32)
        mb = mb_ref[rows, cols].astype(F32)
        o_ref[rows, cols] = (_sigmoid(ma) * pa + _sigmoid(mb) * pb).astype(o_ref.dtype)

    sub = lax.broadcasted_iota(jnp.int32, (SUBLANES, cbw), 0)
    stage_w = lax.rem(u, 2)
    stage_r = 1 - stage_w

    slot_t = lax.rem(row_t, 2)
    cg = min(cbw, 2 * LANES)
    groups = [slice(k * cg, (k + 1) * cg) for k in range(cbw // cg)]
    sub_g = lax.broadcasted_iota(jnp.int32, (SUBLANES, cg), 0)
    coef = [(ba_ref[:, gs], bx_ref[:, gs], _neg_c_softplus(lam_ref[:, gs])) for gs in groups]
    h_ins = [hc_ref[ch_t, :, gs] for gs in groups]
    for c in range(n_chunks):
        if c % chunks_per_piece == 0:
            product_piece(*pieces[c // chunks_per_piece])
        rows = slice(c * chunk, (c + 1) * chunk)
        for k, gs in enumerate(groups):
            a, b = _lru_coeffs(xc_ref[stage_r, rows, gs], ga_ref[stage_r, rows, gs],
                               gx_ref[stage_r, rows, gs], *coef[k])
            h, h_ins[k] = _lru_prompt_scan(a, b, h_ins[k], sub_g)
            o_a_new = (h * _silu(g_ref[rows, gs].astype(F32))).astype(oa_ref.dtype)
            oa_ref[slot_t, ch_t, rows, gs] = jnp.where(live_t, o_a_new, oas_ref[rows, gs])
    for k, gs in enumerate(groups):
        h_in = jnp.where(live_t, h_ins[k], hc_ref[ch_t, :, gs])
        hc_ref[ch_t, :, gs] = h_in
        hp_ref[ch_t, :, gs] = h_in[0:1, :]

    w, bias = cw_ref[...], cb_ref[...]
    x_tail = xt_ref[ch_c]
    xcs = []
    for c in range(n_chunks):
        rows = slice(c * chunk, (c + 1) * chunk)
        xc, x_tail = _lru_prompt_conv(x_ref[rows, :].astype(F32), x_tail, sub, w, bias)
        xcs.append(xc)
    xt_ref[ch_c] = jnp.where(live_c, x_tail, xt_ref[ch_c])
    xc_all = jnp.concatenate(xcs, axis=0)
    ga, gx = _lru_gate_dots(xc_all, wa_ref, wx_ref)
    xc_ref[stage_w] = xc_all
    ga_ref[stage_w] = ga
    gx_ref[stage_w] = gx


def _branch_lru(z, lru_params, o_a_s, o_b, w_branch_bf16, ma_col, s, tm, tn):
    conv_w, conv_b, w_a, b_a, w_x, b_x, lam = lru_params
    m = z.shape[0]
    width = conv_w.shape[1]
    blk = w_a.shape[1]
    aw = o_b.shape[1]
    d = w_branch_bf16.shape[1]
    n_p = s // tm
    n_r = m // tm
    n_t = d // tn
    cbw = width // n_t
    assert cbw % blk == 0 and cbw % LANES == 0
    chunk = tm // 8
    assert chunk % SUBLANES == 0
    ma_blk = ma_col // tn
    mb_blk = (ma_col + d) // tn
    row_c = lambda u: jnp.minimum(u // n_t, n_p - 1)
    ch_c = lambda u: u % n_t
    unit_t = lambda u: jnp.maximum(u - 1, 0)
    row_t = lambda u: jnp.minimum(unit_t(u) // n_t, n_p - 1)
    ch_t = lambda u: unit_t(u) % n_t
    unit_d = lambda u: jnp.maximum(u - 1 - n_t, 0)
    row_d = lambda u: unit_d(u) // n_t
    tile_d = lambda u: unit_d(u) % n_t
    gate_w = pl.BlockSpec((cbw // blk, blk, blk), lambda u: (ch_c(u), 0, 0))
    chan_t = lambda rows: pl.BlockSpec((rows, cbw), lambda u: (0, ch_t(u)))
    return pl.pallas_call(
        functools.partial(_branch_lru_kernel, n_p=n_p, n_t=n_t, chunk=chunk),
        out_shape=(jax.ShapeDtypeStruct((m, d), BF16), jax.ShapeDtypeStruct((n_t, 1, cbw), F32)),
        grid=((n_r + 1) * n_t + 1,),
        in_specs=[
            pl.BlockSpec((tm, cbw), lambda u: (row_c(u), ch_c(u))),
            pl.BlockSpec((tm, cbw), lambda u: (row_t(u), n_t + ch_t(u))),
            pl.BlockSpec((CONV_WIDTH, cbw), lambda u: (0, ch_c(u))),
            pl.BlockSpec((1, cbw), lambda u: (0, ch_c(u))),
            gate_w,
            gate_w,
            chan_t(1),
            chan_t(1),
            chan_t(1),
            chan_t(tm),
            pl.BlockSpec((tm, aw), lambda u: (row_d(u), 0)),
            pl.BlockSpec((width + aw, tn), lambda u: (0, tile_d(u))),
            pl.BlockSpec((tm, tn), lambda u: (row_d(u), ma_blk + tile_d(u))),
            pl.BlockSpec((tm, tn), lambda u: (row_d(u), mb_blk + tile_d(u))),
        ],
        out_specs=(pl.BlockSpec((tm, tn), lambda u: (row_d(u), tile_d(u))),
                   pl.BlockSpec((n_t, 1, cbw), lambda u: (0, 0, 0))),
        scratch_shapes=[
            pltpu.VMEM((2, n_t, tm, cbw), BF16),
            pltpu.VMEM((n_t, SUBLANES, cbw), F32),
            pltpu.VMEM((n_t, SUBLANES, cbw), F32),
            pltpu.VMEM((2, tm, cbw), F32),
            pltpu.VMEM((2, tm, cbw), F32),
            pltpu.VMEM((2, tm, cbw), F32),
        ],
        compiler_params=_params(("arbitrary",)),
        name="branch_lru",
    )(z, z, conv_w, conv_b, w_a, w_x, b_a, b_x, lam, o_a_s, o_b, w_branch_bf16, z, z)


def _out_kernel(m_ref, w_ref, g_ref, x_ref, y_ref, acc_ref, ss_ref, *, d):
    i = pl.program_id(0)
    j = pl.program_id(1)
    slot = lax.rem(i, 2)
    prev = 1 - slot

    @pl.when((i == 0) & (j == 0))
    def _():
        acc_ref[1] = jnp.zeros(acc_ref.shape[1:], F32)
        ss_ref[...] = jnp.zeros_like(ss_ref)

    t = _dot(m_ref[...], w_ref[...])
    acc_ref[slot, j] = t
    ssq = jnp.sum(t * t, axis=-1, keepdims=True)
    ss_ref[slot] = jnp.where(j == 0, ssq, ss_ref[slot] + ssq)

    inv = lax.rsqrt(ss_ref[prev] / d + RMS_EPS)
    y_ref[...] = x_ref[...] + (acc_ref[prev, j] * inv) * g_ref[...]


def _out_proj(merged, w_out_bf16, g, x, row0, tm, tn):
    rows, d = x.shape
    n_r = rows // tm
    n_t = d // tn
    blk0 = row0 // tm
    xy_map = lambda i, j: (jnp.maximum(i - 1, 0), jnp.where(i == 0, 0, j))
    return pl.pallas_call(
        functools.partial(_out_kernel, d=d),
        out_shape=jax.ShapeDtypeStruct((rows, d), F32),
        grid=(n_r + 1, n_t),
        in_specs=[
            pl.BlockSpec((tm, d), lambda i, j: (blk0 + jnp.minimum(i, n_r - 1), 0)),
            pl.BlockSpec((d, tn), lambda i, j: (0, jnp.where(i == n_r, n_t - 1, j))),
            pl.BlockSpec((1, tn), lambda i, j: (0, j)),
            pl.BlockSpec((tm, tn), xy_map),
        ],
        out_specs=pl.BlockSpec((tm, tn), xy_map),
        scratch_shapes=[pltpu.VMEM((2, n_t, tm, tn), F32), pltpu.VMEM((2, tm, 1), F32)],
        compiler_params=_params(("arbitrary", "arbitrary")),
        name="out_proj",
    )(merged, w_out_bf16, g, x)


def _pad_rows(x, rows):
    return jnp.pad(x, ((0, 0), (0, rows - x.shape[1]), (0, 0)))


def _layer(xp, xs_tm, conv_state, h_state, cache_k, cache_v, norm_pre, norm_post, w_in, conv_w,
           conv_b, w_a, b_a, w_x, b_x, lam, sinks, w_branch, w_out):
    s, d = xp.shape
    nb, wb, n_kv, _ = cache_k.shape
    ts = xs_tm.shape[0] // nb
    width = conv_w.shape[1]
    n_heads = sinks.shape[0]
    group = n_heads // n_kv
    aw, kvw = n_heads * HEAD_DIM, n_kv * HEAD_DIM
    m = s + nb * ts
    tm = nb * ts
    assert s % tm == 0 and tm % SUBLANES == 0 and ts >= CONV_WIDTH - 1
    q_col = 2 * width
    k_col = q_col + aw
    v_col = k_col + kvw
    g_col = v_col + kvw
    ma_col = g_col + aw
    assert w_in.shape[1] == ma_col + 2 * d

    xn = _rmsnorm_pre(xp, xs_tm, norm_pre.reshape(1, d), _largest_divisor(tm, (512, 256, 128, 64, 8)))
    assert ts < wb
    z, (w_branch_bf16, w_out_bf16), (new_k_s, new_v_s) = _in_proj(
        xn, w_in, (w_branch, w_out), (cache_k, cache_v), shift=ts)

    tn = _largest_divisor(ma_col, (512, 256, 128))
    assert d % tn == 0 and width % (d // tn) == 0
    lru_params = (conv_w, conv_b.reshape(1, width), w_a.astype(BF16), b_a.reshape(1, width),
                  w_x.astype(BF16), b_x.reshape(1, width), lam.reshape(1, width))
    o_a_s, h_s = _lru_sample(z, lru_params, conv_state.transpose(1, 0, 2), h_state, s, tm,
                             width // (d // tn))

    def sample_cols(c0, c1):
        blk = lax.slice(z, (s, c0), (m, c1)).reshape(ts, nb, c1 - c0)
        return blk.transpose(1, 0, 2).astype(F32)

    q_s, k_s, v_s = sample_cols(q_col, k_col), sample_cols(k_col, v_col), sample_cols(v_col, g_col)
    g_s = sample_cols(g_col, ma_col)
    tq = -(-ts // SUBLANES) * SUBLANES
    tk = -(-ts // BF16_ROWS) * BF16_ROWS
    ck = cache_k.reshape(nb, wb, kvw)
    cv = cache_v.reshape(nb, wb, kvw)
    sink_rows = jnp.broadcast_to(sinks.reshape(n_heads, 1, 1), (n_heads, tq, LANES))
    sink_rows = sink_rows.reshape(n_heads * tq, LANES)
    o_b_s = _attn_sample(_pad_rows(q_s, tq), _pad_rows(k_s, tk), _pad_rows(v_s, tk), ck, cv,
                         _pad_rows(g_s, tq), sink_rows, n_kv, group, ts)
    o_b_s = o_b_s[:, :ts].transpose(1, 0, 2).reshape(nb * ts, aw).astype(BF16)
    o_b = _attn_prompt(z, sinks, o_b_s, s, width, aw, kvw, n_kv, group)

    merged, h_p = _branch_lru(z, lru_params, o_a_s, o_b, w_branch_bf16, ma_col, s, tm, tn)
    h_p = h_p.reshape(1, width)
    g_post = norm_post.reshape(1, d)
    tn_out = _largest_divisor(d, (1024, 512, 256, 128))
    y_p = _out_proj(merged, w_out_bf16, g_post, xp, 0, tm, tn_out)
    y_s = _out_proj(merged, w_out_bf16, g_post, xs_tm, s, tm, tn_out)

    keep = CONV_WIDTH - 1
    new_conv_p = lax.slice(z, (s - keep, 0), (s, width)).astype(F32)[None]
    wbp = min(WINDOW, s)
    new_k_p = lax.slice(z, (s - wbp, k_col), (s, v_col)).astype(F32).reshape(1, wbp, n_kv, HEAD_DIM)
    new_v_p = lax.slice(z, (s - wbp, v_col), (s, g_col)).astype(F32).reshape(1, wbp, n_kv, HEAD_DIM)
    x_lru_s = sample_cols(0, width)
    new_conv_s = jnp.concatenate([conv_state, x_lru_s], axis=1)[:, -keep:]
    new_k_s = _cache_append(new_k_s, k_s.reshape(nb, ts, n_kv, HEAD_DIM))
    new_v_s = _cache_append(new_v_s, v_s.reshape(nb, ts, n_kv, HEAD_DIM))
    return y_p, y_s, (new_conv_p, h_p, new_k_p, new_v_p), (new_conv_s, h_s, new_k_s, new_v_s)


def kernel(x_prompt, x_sample, state_conv, state_h, cache_k_win, cache_v_win, norm_pre, norm_post, w_in, conv_w, conv_b, lru_w_a, lru_b_a, lru_w_x, lru_b_x, lru_lambda, attn_sinks, w_branch, w_out):
    batch, s, d = x_prompt.shape
    nb, ts, _ = x_sample.shape
    assert batch == 1, "the prompt group is a single sequence"
    depth = w_in.shape[0]
    xp = x_prompt.reshape(s, d)
    xs_tm = x_sample.transpose(1, 0, 2).reshape(ts * nb, d)
    p_states, s_states = [], []
    for l in range(depth):
        xp, xs_tm, p_new, s_new = _layer(
            xp, xs_tm, state_conv[l], state_h[l], cache_k_win[l], cache_v_win[l], norm_pre[l],
            norm_post[l], w_in[l], conv_w[l], conv_b[l], lru_w_a[l], lru_b_a[l], lru_w_x[l],
            lru_b_x[l], lru_lambda[l], attn_sinks[l], w_branch[l], w_out[l])
        p_states.append(p_new)
        s_states.append(s_new)
    y_prompt = xp.reshape(1, s, d)
    y_sample = xs_tm.reshape(ts, nb, d).transpose(1, 0, 2)
    stack = lambda states, k: jnp.stack([st[k] for st in states])
    return (y_prompt, y_sample,
            stack(p_states, 0), stack(p_states, 1), stack(p_states, 2), stack(p_states, 3),
            stack(s_states, 0), stack(s_states, 1), stack(s_states, 2), stack(s_states, 3))
```

```python
import functools

import jax
import jax.numpy as jnp
from jax import lax
from jax.experimental import pallas as pl
from jax.experimental.pallas import tpu as pltpu

HEAD_DIM = 64
WINDOW = 128
CONV_WIDTH = 4
LRU_C = 8.0
RMS_EPS = 1e-6
NEG_INF = -1e30

LANES = 128
SUBLANES = 8
BF16_ROWS = 16
VMEM_LIMIT_BYTES = 58 * 1024 * 1024
IN_PROJ_VMEM_BUDGET = 52 * 1024 * 1024
IN_PROJ_VMEM_LIMIT_BYTES = 63 * 1024 * 1024

F32 = jnp.float32
BF16 = jnp.bfloat16


def _largest_divisor(n, candidates):
    for c in candidates:
        if n % c == 0:
            return c
    raise ValueError(f"no tile in {candidates} divides {n}")


def _params(sem, vmem_limit_bytes=VMEM_LIMIT_BYTES):
    return pltpu.CompilerParams(dimension_semantics=sem, vmem_limit_bytes=vmem_limit_bytes)


def _dot(a, b):
    return jnp.dot(a, b, preferred_element_type=F32)


def _dot_nt(a, b):
    return lax.dot_general(a, b, (((1,), (1,)), ((), ())), preferred_element_type=F32)


def _sigmoid(x):
    return jax.nn.sigmoid(x)


def _silu(x):
    return x * jax.nn.sigmoid(x)


def _rmsnorm_kernel(xp_ref, xs_ref, g_ref, o_ref, *, n_p):
    i = pl.program_id(0)

    def body(x_ref):
        x = x_ref[...]
        ms = jnp.mean(x * x, axis=-1, keepdims=True)
        o_ref[...] = ((x * lax.rsqrt(ms + RMS_EPS)) * g_ref[...]).astype(o_ref.dtype)

    @pl.when(i < n_p)
    def _():
        body(xp_ref)

    @pl.when(i >= n_p)
    def _():
        body(xs_ref)


def _rmsnorm_pre(xp, xs, g, tm):
    s, d = xp.shape
    ms = xs.shape[0]
    n_p, n_s = s // tm, ms // tm
    return pl.pallas_call(
        functools.partial(_rmsnorm_kernel, n_p=n_p),
        out_shape=jax.ShapeDtypeStruct((s + ms, d), BF16),
        grid=(n_p + n_s,),
        in_specs=[
            pl.BlockSpec((tm, d), lambda i: (jnp.minimum(i, n_p - 1), 0)),
            pl.BlockSpec((tm, d), lambda i: (jnp.maximum(i - n_p, 0), 0)),
            pl.BlockSpec((1, d), lambda i: (0, 0)),
        ],
        out_specs=pl.BlockSpec((tm, d), lambda i: (i, 0)),
        compiler_params=_params(("arbitrary",)),
        name="rmsnorm_pre",
    )(xp, xs, g)


def _cast_rows(src_ref, dst_ref, rows_per_step=256):
    n = src_ref.shape[0]
    step = _largest_divisor(n, (rows_per_step, 128, 64, 32, 16))

    def body(r, carry):
        r0 = pl.multiple_of(r * step, step)
        dst_ref[pl.ds(r0, step), :] = src_ref[pl.ds(r0, step), :].astype(dst_ref.dtype)
        return carry

    lax.fori_loop(0, n // step, body, 0)


def _in_proj_kernel(x_ref, w_ref, *rest):
    n_extra = (len(rest) - 2) // 2
    extra_in, o_ref = rest[:n_extra], rest[n_extra]
    extra_out, wb_ref = rest[n_extra + 1:2 * n_extra + 1], rest[-1]

    @pl.when(pl.program_id(1) == 0)
    def _():
        _cast_rows(w_ref, wb_ref)

    o_ref[...] = _dot(x_ref[...], wb_ref[...]).astype(o_ref.dtype)
    for src, dst in zip(extra_in, extra_out):
        dst[...] = src[...].astype(dst.dtype)


def _in_proj_tiles(m, d, n):
    best = None
    for tm in range(BF16_ROWS, m + 1, BF16_ROWS):
        if m % tm:
            continue
        for tn in range(2 * LANES, n + 1, 2 * LANES):
            if n % tn:
                continue
            vmem = d * tn * (2 * 4 + 2) + 2 * tm * d * 2 + 2 * tm * tn * 2
            if vmem <= IN_PROJ_VMEM_BUDGET and (best is None or tm * tn > best[0] * best[1]):
                best = (tm, tn)
    assert best is not None
    return best


def _in_proj(xn, w_in, extra_weights=()):
    m, d = xn.shape
    n = w_in.shape[1]
    tm, tn = _in_proj_tiles(m, d, n)
    n_i = m // tm
    n_steps = (n // tn) * n_i
    extra_specs = []
    for w in extra_weights:
        rows = w.shape[0]
        slab = next(r for r in range(BF16_ROWS, rows + 1, BF16_ROWS)
                    if rows % r == 0 and rows // r <= n_steps)
        last = rows // slab - 1
        extra_specs.append(pl.BlockSpec(
            (slab, w.shape[1]), lambda j, i, last=last: (jnp.minimum(j * n_i + i, last), 0)))
    out = pl.pallas_call(
        _in_proj_kernel,
        out_shape=(jax.ShapeDtypeStruct((m, n), BF16),
                   *[jax.ShapeDtypeStruct(w.shape, BF16) for w in extra_weights]),
        grid=(n // tn, n_i),
        in_specs=[
            pl.BlockSpec((tm, d), lambda j, i: (i, 0)),
            pl.BlockSpec((d, tn), lambda j, i: (0, j)),
            *extra_specs,
        ],
        out_specs=(pl.BlockSpec((tm, tn), lambda j, i: (i, j)), *extra_specs),
        scratch_shapes=[pltpu.VMEM((d, tn), BF16)],
        compiler_params=_params(("arbitrary", "arbitrary"), IN_PROJ_VMEM_LIMIT_BYTES),
        name="in_proj",
    )(xn, w_in, *extra_weights)
    return out[0], out[1:]


LOG2_E = 1.4426950408889634


def _neg_c_softplus(lam):
    return (-LRU_C * LOG2_E) * (jnp.maximum(-lam, 0.0) + jnp.log(1.0 + jnp.exp(-jnp.abs(lam))))


def _lru_conv(taps, w, bias):
    out = bias
    for k in range(CONV_WIDTH):
        out = out + taps[k] * w[k:k + 1, :]
    return out


def _lru_gate_dots(xc, wa_ref, wx_ref):
    blk = wa_ref.shape[1]
    xcb = xc.astype(BF16)
    ga, gx = [], []
    for q in range(xc.shape[1] // blk):
        xq = xcb[:, q * blk:(q + 1) * blk]
        ga.append(_dot(xq, wa_ref[q]))
        gx.append(_dot(xq, wx_ref[q]))
    return jnp.concatenate(ga, axis=1), jnp.concatenate(gx, axis=1)


def _lru_coeffs(xc, ga, gx, b_a, b_x, ncs):
    r_a = _sigmoid(ga + b_a)
    r_x = _sigmoid(gx + b_x)
    a = jnp.exp2(ncs * r_a)
    b = jnp.exp2(0.5 * jnp.log2(1.0 - a * a)) * (r_x * xc)
    return a, b


def _scan_tile(a, b, sub):
    shift = 1
    while shift < SUBLANES:
        keep = sub >= shift
        a_sh = jnp.where(keep, pltpu.roll(a, shift, axis=0), 1.0)
        b_sh = jnp.where(keep, pltpu.roll(b, shift, axis=0), 0.0)
        b = b + a * b_sh
        a = a * a_sh
        shift *= 2
    return a, b


def _lru_prompt_conv(xcur, x_tail, sub, w, bias):
    n_tiles = xcur.shape[0] // SUBLANES
    tiles = [x_tail] + [xcur[v * SUBLANES:(v + 1) * SUBLANES] for v in range(n_tiles)]
    taps = []
    for k in range(CONV_WIDTH):
        back = CONV_WIDTH - 1 - k
        if back == 0:
            taps.append(xcur)
            continue
        rolled = [pltpu.roll(t, back, axis=0) for t in tiles]
        taps.append(jnp.concatenate(
            [jnp.where(sub >= back, rolled[v + 1], rolled[v]) for v in range(n_tiles)], axis=0))
    return _lru_conv(taps, w, bias), tiles[-1]


def _lru_prompt_scan(a, b, h_in, sub):
    hs = []
    for v in range(a.shape[0] // SUBLANES):
        rows = slice(v * SUBLANES, (v + 1) * SUBLANES)
        a_in, b_in = _scan_tile(a[rows], b[rows], sub)
        hv = a_in * h_in + b_in
        hs.append(hv)
        h_in = jnp.broadcast_to(hv[SUBLANES - 1:SUBLANES, :], hv.shape)
    return jnp.concatenate(hs, axis=0), h_in


def _lru_sample_kernel(x_ref, g_ref, cw_ref, cb_ref, wa_ref, ba_ref, wx_ref, bx_ref, lam_ref,
                       cst_ref, h0_ref, o_ref, hs_ref):
    nb = h0_ref.shape[0]
    ts = x_ref.shape[0] // nb
    w, bias = cw_ref[...], cb_ref[...]
    ncs = _neg_c_softplus(lam_ref[...])
    xs = [cst_ref[k] for k in range(CONV_WIDTH - 1)]
    xs += [x_ref[t * nb:(t + 1) * nb, :].astype(F32) for t in range(ts)]
    h = h0_ref[...]
    for t in range(ts):
        xc = _lru_conv(xs[t:t + CONV_WIDTH], w, bias)
        ga, gx = _lru_gate_dots(xc, wa_ref, wx_ref)
        a, b = _lru_coeffs(xc, ga, gx, ba_ref[...], bx_ref[...], ncs)
        h = a * h + b
        g = g_ref[t * nb:(t + 1) * nb, :].astype(F32)
        o_ref[t * nb:(t + 1) * nb, :] = (h * _silu(g)).astype(o_ref.dtype)
    hs_ref[...] = h


def _lru_sample(z, lru_params, cst_tm, h0, s, tb, cbw):
    conv_w, conv_b, w_a, b_a, w_x, b_x, lam = lru_params
    width = conv_w.shape[1]
    nb = h0.shape[0]
    blk = w_a.shape[1]
    n_c = width // cbw
    blk_s = s // tb
    row = lambda c: (0, c)
    gate_w = pl.BlockSpec((cbw // blk, blk, blk), lambda c: (c, 0, 0))
    return pl.pallas_call(
        _lru_sample_kernel,
        out_shape=(jax.ShapeDtypeStruct((tb, width), BF16), jax.ShapeDtypeStruct((nb, width), F32)),
        grid=(n_c,),
        in_specs=[
            pl.BlockSpec((tb, cbw), lambda c: (blk_s, c)),
            pl.BlockSpec((tb, cbw), lambda c: (blk_s, n_c + c)),
            pl.BlockSpec((CONV_WIDTH, cbw), row),
            pl.BlockSpec((1, cbw), row),
            gate_w,
            pl.BlockSpec((1, cbw), row),
            gate_w,
            pl.BlockSpec((1, cbw), row),
            pl.BlockSpec((1, cbw), row),
            pl.BlockSpec((CONV_WIDTH - 1, nb, cbw), lambda c: (0, 0, c)),
            pl.BlockSpec((nb, cbw), row),
        ],
        out_specs=(pl.BlockSpec((tb, cbw), row), pl.BlockSpec((nb, cbw), row)),
        compiler_params=_params(("arbitrary",)),
        name="lru_sample",
    )(z, z, conv_w, conv_b, w_a, b_a, w_x, b_x, lam, cst_tm, h0)


def _attn_prompt_kernel(sink_ref, q_ref, kp_ref, kc_ref, vp_ref, vc_ref, g0_ref, g1_ref, os_ref,
                        o_ref, *, n_q, n_kv, group):
    i = pl.program_id(0)

    @pl.when(i < n_q)
    def _():
        _attn_prompt_block(i, sink_ref, q_ref, kp_ref, kc_ref, vp_ref, vc_ref, g0_ref, g1_ref,
                           o_ref, n_kv, group)

    @pl.when(i >= n_q)
    def _():
        o_ref[...] = os_ref[...]


def _attn_prompt_block(i, sink_ref, q_ref, kp_ref, kc_ref, vp_ref, vc_ref, g0_ref, g1_ref, o_ref,
                       n_kv, group):
    qb = q_ref.shape[0]
    aw_half = g0_ref.shape[1]
    scale = HEAD_DIM ** -0.5
    hw = group * HEAD_DIM
    c = lax.broadcasted_iota(jnp.int32, (2 * qb, qb), 0)
    r = lax.broadcasted_iota(jnp.int32, (2 * qb, qb), 1)
    rel = qb + r - c
    mask = (rel >= 0) & (rel <= WINDOW) & ((c >= qb) | (i > 0))
    lane = lax.broadcasted_iota(jnp.int32, (qb, LANES), 1)
    half_mask = (lane < HEAD_DIM, lane >= HEAD_DIM)
    zero = jnp.zeros((qb, LANES), BF16)
    n_pairs = n_kv // 2
    kv = {}
    out_t = {}

    def heads_of(p, variant):
        return [(2 * p + half, g) for half in range(2) for g in range(group)
                if ((g % 2) == half) == (variant == 0)]

    def load_pair(p):
        lanes = slice(p * LANES, (p + 1) * LANES)
        k_f32 = jnp.concatenate([kp_ref[:, lanes], kc_ref[:, lanes]], axis=0).astype(F32) * scale
        v_pair = jnp.concatenate([vp_ref[:, lanes], vc_ref[:, lanes]], axis=0)
        kv[p] = (k_f32.astype(BF16), pltpu.roll(k_f32, HEAD_DIM, axis=1).astype(BF16),
                 v_pair.astype(F32).T.astype(BF16))

    def scores(p, variant):
        q_rows = []
        for kh, g in heads_of(p, variant):
            c0 = kh * hw + (g // 2) * LANES
            q_rows.append(jnp.where(half_mask[g % 2], q_ref[:, c0:c0 + LANES], zero))
        return _dot_nt(kv[p][variant], jnp.concatenate(q_rows, axis=0))

    def softmax_values(p, variant, s_t):
        heads = heads_of(p, variant)
        p_blocks, inv_den = [], []
        for b, (kh, g) in enumerate(heads):
            sink = sink_ref[kh * group + g]
            sb = jnp.where(mask, s_t[:, b * qb:(b + 1) * qb], NEG_INF)
            m = jnp.maximum(jnp.max(sb, axis=0, keepdims=True), sink)
            pe = jnp.exp(sb - m)
            den = jnp.sum(pe, axis=0, keepdims=True) + jnp.exp(sink - m)
            p_blocks.append(pe.astype(BF16))
            inv_den.append(1.0 / den)
        o_t = _dot(kv[p][2], jnp.concatenate(p_blocks, axis=1))
        for b, (kh, g) in enumerate(heads):
            r0 = (kh % 2) * HEAD_DIM
            out_t[(kh, g)] = o_t[r0:r0 + HEAD_DIM, b * qb:(b + 1) * qb] * inv_den[b]

    def store_pair(p):
        for half in range(2):
            kh = 2 * p + half
            for j in range(group // 2):
                c0 = kh * hw + j * LANES
                tile = jnp.concatenate([out_t[(kh, 2 * j)], out_t[(kh, 2 * j + 1)]], axis=0).T
                g_ref, gc = (g0_ref, c0) if c0 < aw_half else (g1_ref, c0 - aw_half)
                gate = g_ref[:, gc:gc + LANES].astype(F32)
                o_ref[:, c0:c0 + LANES] = (tile * _silu(gate)).astype(o_ref.dtype)

    groups = [(p, variant) for p in range(n_pairs) for variant in range(2)]
    load_pair(0)
    s_next = scores(*groups[0])
    for k, (p, variant) in enumerate(groups):
        s_cur = s_next
        if k + 1 < len(groups):
            if groups[k + 1][0] != p:
                load_pair(groups[k + 1][0])
            s_next = scores(*groups[k + 1])
        softmax_values(p, variant, s_cur)
        if variant == 1:
            store_pair(p)


def _attn_prompt(z, sinks, o_b_s, s, width, aw, kvw, n_kv, group):
    qb = WINDOW
    n_q = s // qb
    assert s % qb == 0 and o_b_s.shape[0] % qb == 0 and n_kv % 2 == 0 and group % 2 == 0
    q_blk = (2 * width) // aw
    k_blk = (2 * width + aw) // kvw
    v_blk = k_blk + 1
    g_blk = (2 * width + aw + 2 * kvw) // (aw // 2)
    assert (2 * width) % aw == 0 and (2 * width + aw) % kvw == 0
    assert (2 * width + aw + 2 * kvw) % (aw // 2) == 0
    cur = lambda i: jnp.minimum(i, n_q - 1)
    prev = lambda i: jnp.maximum(cur(i) - 1, 0)
    return pl.pallas_call(
        functools.partial(_attn_prompt_kernel, n_q=n_q, n_kv=n_kv, group=group),
        out_shape=jax.ShapeDtypeStruct((s + o_b_s.shape[0], aw), BF16),
        grid=(n_q + o_b_s.shape[0] // qb,),
        in_specs=[
            pl.BlockSpec(memory_space=pltpu.SMEM),
            pl.BlockSpec((qb, aw), lambda i: (cur(i), q_blk)),
            pl.BlockSpec((qb, kvw), lambda i: (prev(i), k_blk)),
            pl.BlockSpec((qb, kvw), lambda i: (cur(i), k_blk)),
            pl.BlockSpec((qb, kvw), lambda i: (prev(i), v_blk)),
            pl.BlockSpec((qb, kvw), lambda i: (cur(i), v_blk)),
            pl.BlockSpec((qb, aw // 2), lambda i: (cur(i), g_blk)),
            pl.BlockSpec((qb, aw // 2), lambda i: (cur(i), g_blk + 1)),
            pl.BlockSpec((qb, aw), lambda i: (jnp.maximum(i - n_q, 0), 0)),
        ],
        out_specs=pl.BlockSpec((qb, aw), lambda i: (i, 0)),
        compiler_params=_params(("arbitrary",)),
        name="attn_prompt",
    )(sinks, z, z, z, z, z, z, z, o_b_s)


def _attn_sample_kernel(q_ref, kn_ref, vn_ref, ck_ref, cv_ref, g_ref, sink_ref, o_ref,
                        s_ref, p_ref, *, n_kv, group, ts):
    sb, tq, _ = q_ref.shape
    wb = ck_ref.shape[3]
    tk = kn_ref.shape[1]
    n_keys = s_ref.shape[1]
    n_new = n_keys - wb
    n_pairs = n_kv // 2
    pair_rows = 2 * group * tq
    seq_rows = n_pairs * pair_rows
    scale = HEAD_DIM ** -0.5
    hw = group * HEAD_DIM
    lane = lax.broadcasted_iota(jnp.int32, (tq, LANES), 1)
    new_pad = jnp.zeros((n_new - tk, LANES), F32)

    def cached_t(c_ref, n, p):
        return jnp.concatenate([c_ref[n, 2 * p], c_ref[n, 2 * p + 1]], axis=0).astype(BF16)

    def new_rows(n_ref, n, p):
        lanes = slice(p * LANES, (p + 1) * LANES)
        return jnp.concatenate([n_ref[n, :, lanes], new_pad], axis=0).astype(BF16)

    def score_body(n, carry):
        for p in range(n_pairs):
            pieces = []
            for half in range(2):
                for g in range(group):
                    c0 = (2 * p + half) * hw + (g // 2) * LANES
                    tile = q_ref[n, :, c0:c0 + LANES] * scale
                    if (g % 2) != half:
                        tile = pltpu.roll(tile, HEAD_DIM, axis=1)
                    keep = (lane >= HEAD_DIM * half) & (lane < HEAD_DIM * (half + 1))
                    pieces.append(jnp.where(keep, tile, 0.0))
            lhs = jnp.concatenate(pieces, axis=0).astype(BF16)
            r0 = pl.multiple_of(n * seq_rows + p * pair_rows, pair_rows)
            s_ref[pl.ds(r0, pair_rows), 0:wb] = _dot(lhs, cached_t(ck_ref, n, p))
            s_ref[pl.ds(r0, pair_rows), wb:] = _dot_nt(lhs, new_rows(kn_ref, n, p))
        return carry

    lax.fori_loop(0, sb, score_body, 0)

    t = lax.broadcasted_iota(jnp.int32, (tq, n_keys), 0)
    c = lax.broadcasted_iota(jnp.int32, (tq, n_keys), 1)
    mask8 = ((c < wb) & (t + wb - c <= WINDOW)) | ((c >= wb) & (c - wb <= t) & (c - wb < ts))
    reps = sb * seq_rows // tq
    mask = jnp.concatenate([mask8] * reps, axis=0)
    sink = jnp.concatenate([sink_ref[:, 0:1]] * sb, axis=0)
    s = jnp.where(mask, s_ref[...], NEG_INF)
    m = jnp.maximum(jnp.max(s, axis=-1, keepdims=True), sink)
    pe = jnp.exp(s - m)
    den = jnp.sum(pe, axis=-1, keepdims=True) + jnp.exp(sink - m)
    p_ref[...] = (pe / den).astype(p_ref.dtype)

    def value_body(n, carry):
        for p in range(n_pairs):
            r0 = pl.multiple_of(n * seq_rows + p * pair_rows, pair_rows)
            o = (_dot_nt(p_ref[pl.ds(r0, pair_rows), 0:wb], cached_t(cv_ref, n, p))
                 + _dot(p_ref[pl.ds(r0, pair_rows), wb:], new_rows(vn_ref, n, p)))
            for half in range(2):
                kh = 2 * p + half
                for j in range(group // 2):
                    rows = (half * group + 2 * j) * tq
                    o_lo, o_hi = o[rows:rows + tq], o[rows + tq:rows + 2 * tq]
                    if half == 0:
                        o_hi = pltpu.roll(o_hi, HEAD_DIM, axis=1)
                    else:
                        o_lo = pltpu.roll(o_lo, HEAD_DIM, axis=1)
                    c0 = kh * hw + j * LANES
                    gate = g_ref[n, :, c0:c0 + LANES]
                    o_ref[n, :, c0:c0 + LANES] = jnp.where(lane < HEAD_DIM, o_lo, o_hi) * _silu(gate)
        return carry

    lax.fori_loop(0, sb, value_body, 0)


def _attn_sample(q8, kn, vn, cache_k_t, cache_v_t, g8, sink_rows, n_kv, group, ts):
    nb, tq, aw = q8.shape
    tk, kvw = kn.shape[1], kn.shape[2]
    wb = cache_k_t.shape[3]
    sb = _largest_divisor(nb, (8, 4, 2, 1))
    assert wb % LANES == 0
    n_keys = wb + -(-tk // LANES) * LANES
    rows = sb * (n_kv // 2) * 2 * group * tq
    assert sink_rows.shape[0] * sb == rows
    blk3 = lambda r, width: pl.BlockSpec((sb, r, width), lambda i: (i, 0, 0))
    cache_blk = pl.BlockSpec((sb, n_kv, HEAD_DIM, wb), lambda i: (i, 0, 0, 0))
    return pl.pallas_call(
        functools.partial(_attn_sample_kernel, n_kv=n_kv, group=group, ts=ts),
        out_shape=jax.ShapeDtypeStruct((nb, tq, aw), F32),
        grid=(nb // sb,),
        in_specs=[
            blk3(tq, aw), blk3(tk, kvw), blk3(tk, kvw), cache_blk, cache_blk, blk3(tq, aw),
            pl.BlockSpec(sink_rows.shape, lambda i: (0, 0)),
        ],
        out_specs=blk3(tq, aw),
        scratch_shapes=[pltpu.VMEM((rows, n_keys), F32), pltpu.VMEM((rows, n_keys), BF16)],
        compiler_params=_params(("arbitrary",)),
        name="attn_sample",
    )(q8, kn, vn, cache_k_t, cache_v_t, g8, sink_rows)


def _branch_lru_kernel(x_ref, g_ref, cw_ref, cb_ref, wa_ref, wx_ref, ba_ref, bx_ref, lam_ref,
                       oas_ref, ob_ref, w_ref, ma_ref, mb_ref,
                       o_ref, hp_ref, oa_ref, xt_ref, hc_ref, xc_ref, ga_ref, gx_ref,
                       *, n_p, n_t, chunk):
    u = pl.program_id(0)
    tm, cbw = x_ref.shape

    @pl.when(u == 0)
    def _():
        for ref in (oa_ref, xt_ref, hc_ref, xc_ref, ga_ref, gx_ref):
            ref[...] = jnp.zeros_like(ref)

    unit_t = jnp.maximum(u - 1, 0)
    row_t, ch_t = lax.div(unit_t, n_t), lax.rem(unit_t, n_t)
    live_t = (u >= 1) & (unit_t < n_p * n_t)
    row_d = lax.div(jnp.maximum(u - 1 - n_t, 0), n_t)
    ch_c = lax.rem(u, n_t)
    live_c = u < n_p * n_t

    slot_d = lax.rem(row_d, 2)
    width = n_t * cbw
    tn = o_ref.shape[1]
    n_chunks = tm // chunk
    pieces = [(rh, ch) for rh in range(2) for ch in range(2)]
    chunks_per_piece = n_chunks // len(pieces)
    assert chunks_per_piece * len(pieces) == n_chunks

    def product_piece(rh, ch):
        rows = slice(rh * tm // 2, (rh + 1) * tm // 2)
        cols = slice(ch * tn // 2, (ch + 1) * tn // 2)
        o_a = jnp.concatenate([oa_ref[slot_d, q, rows, :] for q in range(n_t)], axis=1)
        pa = _dot(o_a, w_ref[0:width, cols])
        pb = _dot(ob_ref[rows, :], w_ref[width:, cols])
        ma = ma_ref[rows, cols].astype(F32)
        mb = mb_ref[rows, cols].astype(F32)
        o_ref[rows, cols] = (_sigmoid(ma) * pa + _sigmoid(mb) * pb).astype(o_ref.dtype)

    sub = lax.broadcasted_iota(jnp.int32, (SUBLANES, cbw), 0)
    stage_w = lax.rem(u, 2)
    stage_r = 1 - stage_w

    slot_t = lax.rem(row_t, 2)
    cg = min(cbw, 2 * LANES)
    groups = [slice(k * cg, (k + 1) * cg) for k in range(cbw // cg)]
    sub_g = lax.broadcasted_iota(jnp.int32, (SUBLANES, cg), 0)
    coef = [(ba_ref[:, gs], bx_ref[:, gs], _neg_c_softplus(lam_ref[:, gs])) for gs in groups]
    h_ins = [hc_ref[ch_t, :, gs] for gs in groups]
    for c in range(n_chunks):
        if c % chunks_per_piece == 0:
            product_piece(*pieces[c // chunks_per_piece])
        rows = slice(c * chunk, (c + 1) * chunk)
        for k, gs in enumerate(groups):
            a, b = _lru_coeffs(xc_ref[stage_r, rows, gs], ga_ref[stage_r, rows, gs],
                               gx_ref[stage_r, rows, gs], *coef[k])
            h, h_ins[k] = _lru_prompt_scan(a, b, h_ins[k], sub_g)
            o_a_new = (h * _silu(g_ref[rows, gs].astype(F32))).astype(oa_ref.dtype)
            oa_ref[slot_t, ch_t, rows, gs] = jnp.where(live_t, o_a_new, oas_ref[rows, gs])
    for k, gs in enumerate(groups):
        h_in = jnp.where(live_t, h_ins[k], hc_ref[ch_t, :, gs])
        hc_ref[ch_t, :, gs] = h_in
        hp_ref[ch_t, :, gs] = h_in[0:1, :]

    w, bias = cw_ref[...], cb_ref[...]
    x_tail = xt_ref[ch_c]
    xcs = []
    for c in range(n_chunks):
        rows = slice(c * chunk, (c + 1) * chunk)
        xc, x_tail = _lru_prompt_conv(x_ref[rows, :].astype(F32), x_tail, sub, w, bias)
        xcs.append(xc)
    xt_ref[ch_c] = jnp.where(live_c, x_tail, xt_ref[ch_c])
    xc_all = jnp.concatenate(xcs, axis=0)
    ga, gx = _lru_gate_dots(xc_all, wa_ref, wx_ref)
    xc_ref[stage_w] = xc_all
    ga_ref[stage_w] = ga
    gx_ref[stage_w] = gx


def _branch_lru(z, lru_params, o_a_s, o_b, w_branch_bf16, ma_col, s, tm, tn):
    conv_w, conv_b, w_a, b_a, w_x, b_x, lam = lru_params
    m = z.shape[0]
    width = conv_w.shape[1]
    blk = w_a.shape[1]
    aw = o_b.shape[1]
    d = w_branch_bf16.shape[1]
    n_p = s // tm
    n_r = m // tm
    n_t = d // tn
    cbw = width // n_t
    assert cbw % blk == 0 and cbw % LANES == 0
    chunk = tm // 8
    assert chunk % SUBLANES == 0
    ma_blk = ma_col // tn
    mb_blk = (ma_col + d) // tn
    row_c = lambda u: jnp.minimum(u // n_t, n_p - 1)
    ch_c = lambda u: u % n_t
    unit_t = lambda u: jnp.maximum(u - 1, 0)
    row_t = lambda u: jnp.minimum(unit_t(u) // n_t, n_p - 1)
    ch_t = lambda u: unit_t(u) % n_t
    unit_d = lambda u: jnp.maximum(u - 1 - n_t, 0)
    row_d = lambda u: unit_d(u) // n_t
    tile_d = lambda u: unit_d(u) % n_t
    gate_w = pl.BlockSpec((cbw // blk, blk, blk), lambda u: (ch_c(u), 0, 0))
    chan_t = lambda rows: pl.BlockSpec((rows, cbw), lambda u: (0, ch_t(u)))
    return pl.pallas_call(
        functools.partial(_branch_lru_kernel, n_p=n_p, n_t=n_t, chunk=chunk),
        out_shape=(jax.ShapeDtypeStruct((m, d), BF16), jax.ShapeDtypeStruct((n_t, 1, cbw), F32)),
        grid=((n_r + 1) * n_t + 1,),
        in_specs=[
            pl.BlockSpec((tm, cbw), lambda u: (row_c(u), ch_c(u))),
            pl.BlockSpec((tm, cbw), lambda u: (row_t(u), n_t + ch_t(u))),
            pl.BlockSpec((CONV_WIDTH, cbw), lambda u: (0, ch_c(u))),
            pl.BlockSpec((1, cbw), lambda u: (0, ch_c(u))),
            gate_w,
            gate_w,
            chan_t(1),
            chan_t(1),
            chan_t(1),
            chan_t(tm),
            pl.BlockSpec((tm, aw), lambda u: (row_d(u), 0)),
            pl.BlockSpec((width + aw, tn), lambda u: (0, tile_d(u))),
            pl.BlockSpec((tm, tn), lambda u: (row_d(u), ma_blk + tile_d(u))),
            pl.BlockSpec((tm, tn), lambda u: (row_d(u), mb_blk + tile_d(u))),
        ],
        out_specs=(pl.BlockSpec((tm, tn), lambda u: (row_d(u), tile_d(u))),
                   pl.BlockSpec((n_t, 1, cbw), lambda u: (0, 0, 0))),
        scratch_shapes=[
            pltpu.VMEM((2, n_t, tm, cbw), BF16),
            pltpu.VMEM((n_t, SUBLANES, cbw), F32),
            pltpu.VMEM((n_t, SUBLANES, cbw), F32),
            pltpu.VMEM((2, tm, cbw), F32),
            pltpu.VMEM((2, tm, cbw), F32),
            pltpu.VMEM((2, tm, cbw), F32),
        ],
        compiler_params=_params(("arbitrary",)),
        name="branch_lru",
    )(z, z, conv_w, conv_b, w_a, w_x, b_a, b_x, lam, o_a_s, o_b, w_branch_bf16, z, z)


def _out_kernel(m_ref, w_ref, g_ref, x_ref, y_ref, acc_ref, ss_ref, *, d):
    i = pl.program_id(0)
    j = pl.program_id(1)
    slot = lax.rem(i, 2)
    prev = 1 - slot

    @pl.when((i == 0) & (j == 0))
    def _():
        acc_ref[1] = jnp.zeros(acc_ref.shape[1:], F32)
        ss_ref[...] = jnp.zeros_like(ss_ref)

    t = _dot(m_ref[...], w_ref[...])
    acc_ref[slot, j] = t
    ssq = jnp.sum(t * t, axis=-1, keepdims=True)
    ss_ref[slot] = jnp.where(j == 0, ssq, ss_ref[slot] + ssq)

    inv = lax.rsqrt(ss_ref[prev] / d + RMS_EPS)
    y_ref[...] = x_ref[...] + (acc_ref[prev, j] * inv) * g_ref[...]


def _out_proj(merged, w_out_bf16, g, x, row0, tm, tn):
    rows, d = x.shape
    n_r = rows // tm
    n_t = d // tn
    blk0 = row0 // tm
    xy_map = lambda i, j: (jnp.maximum(i - 1, 0), jnp.where(i == 0, 0, j))
    return pl.pallas_call(
        functools.partial(_out_kernel, d=d),
        out_shape=jax.ShapeDtypeStruct((rows, d), F32),
        grid=(n_r + 1, n_t),
        in_specs=[
            pl.BlockSpec((tm, d), lambda i, j: (blk0 + jnp.minimum(i, n_r - 1), 0)),
            pl.BlockSpec((d, tn), lambda i, j: (0, jnp.where(i == n_r, n_t - 1, j))),
            pl.BlockSpec((1, tn), lambda i, j: (0, j)),
            pl.BlockSpec((tm, tn), xy_map),
        ],
        out_specs=pl.BlockSpec((tm, tn), xy_map),
        scratch_shapes=[pltpu.VMEM((2, n_t, tm, tn), F32), pltpu.VMEM((2, tm, 1), F32)],
        compiler_params=_params(("arbitrary", "arbitrary")),
        name="out_proj",
    )(merged, w_out_bf16, g, x)


def _pad_rows(x, rows):
    return jnp.pad(x, ((0, 0), (0, rows - x.shape[1]), (0, 0)))


def _layer(xp, xs_tm, conv_state, h_state, cache_k, cache_v, norm_pre, norm_post, w_in, conv_w,
           conv_b, w_a, b_a, w_x, b_x, lam, sinks, w_branch, w_out):
    s, d = xp.shape
    nb, wb, n_kv, _ = cache_k.shape
    ts = xs_tm.shape[0] // nb
    width = conv_w.shape[1]
    n_heads = sinks.shape[0]
    group = n_heads // n_kv
    aw, kvw = n_heads * HEAD_DIM, n_kv * HEAD_DIM
    m = s + nb * ts
    tm = nb * ts
    assert s % tm == 0 and tm % SUBLANES == 0 and ts >= CONV_WIDTH - 1
    q_col = 2 * width
    k_col = q_col + aw
    v_col = k_col + kvw
    g_col = v_col + kvw
    ma_col = g_col + aw
    assert w_in.shape[1] == ma_col + 2 * d

    xn = _rmsnorm_pre(xp, xs_tm, norm_pre.reshape(1, d), _largest_divisor(tm, (512, 256, 128, 64, 8)))
    z, (w_branch_bf16, w_out_bf16) = _in_proj(xn, w_in, (w_branch, w_out))

    tn = _largest_divisor(ma_col, (512, 256, 128))
    assert d % tn == 0 and width % (d // tn) == 0
    lru_params = (conv_w, conv_b.reshape(1, width), w_a.astype(BF16), b_a.reshape(1, width),
                  w_x.astype(BF16), b_x.reshape(1, width), lam.reshape(1, width))
    o_a_s, h_s = _lru_sample(z, lru_params, conv_state.transpose(1, 0, 2), h_state, s, tm,
                             width // (d // tn))

    def sample_cols(c0, c1):
        blk = lax.slice(z, (s, c0), (m, c1)).reshape(ts, nb, c1 - c0)
        return blk.transpose(1, 0, 2).astype(F32)

    q_s, k_s, v_s = sample_cols(q_col, k_col), sample_cols(k_col, v_col), sample_cols(v_col, g_col)
    g_s = sample_cols(g_col, ma_col)
    tq = -(-ts // SUBLANES) * SUBLANES
    tk = -(-ts // BF16_ROWS) * BF16_ROWS
    ck = cache_k.reshape(nb, wb, kvw)
    cv = cache_v.reshape(nb, wb, kvw)
    sink_rows = jnp.broadcast_to(sinks.reshape(n_heads, 1, 1), (n_heads, tq, LANES))
    sink_rows = sink_rows.reshape(n_heads * tq, LANES)
    heads_t = lambda c: c.transpose(0, 2, 3, 1)
    o_b_s = _attn_sample(_pad_rows(q_s, tq), _pad_rows(k_s, tk), _pad_rows(v_s, tk),
                         heads_t(cache_k), heads_t(cache_v), _pad_rows(g_s, tq), sink_rows,
                         n_kv, group, ts)
    o_b_s = o_b_s[:, :ts].transpose(1, 0, 2).reshape(nb * ts, aw).astype(BF16)
    o_b = _attn_prompt(z, sinks, o_b_s, s, width, aw, kvw, n_kv, group)

    merged, h_p = _branch_lru(z, lru_params, o_a_s, o_b, w_branch_bf16, ma_col, s, tm, tn)
    h_p = h_p.reshape(1, width)
    g_post = norm_post.reshape(1, d)
    tn_out = _largest_divisor(d, (1024, 512, 256, 128))
    y_p = _out_proj(merged, w_out_bf16, g_post, xp, 0, tm, tn_out)
    y_s = _out_proj(merged, w_out_bf16, g_post, xs_tm, s, tm // 2, tn_out)

    keep = CONV_WIDTH - 1
    new_conv_p = lax.slice(z, (s - keep, 0), (s, width)).astype(F32)[None]
    wbp = min(WINDOW, s)
    new_k_p = lax.slice(z, (s - wbp, k_col), (s, v_col)).astype(F32).reshape(1, wbp, n_kv, HEAD_DIM)
    new_v_p = lax.slice(z, (s - wbp, v_col), (s, g_col)).astype(F32).reshape(1, wbp, n_kv, HEAD_DIM)
    x_lru_s = sample_cols(0, width)
    new_conv_s = jnp.concatenate([conv_state, x_lru_s], axis=1)[:, -keep:]
    new_k_s = jnp.concatenate([ck, k_s], axis=1)[:, -wb:].reshape(nb, wb, n_kv, HEAD_DIM)
    new_v_s = jnp.concatenate([cv, v_s], axis=1)[:, -wb:].reshape(nb, wb, n_kv, HEAD_DIM)
    return y_p, y_s, (new_conv_p, h_p, new_k_p, new_v_p), (new_conv_s, h_s, new_k_s, new_v_s)


def kernel(x_prompt, x_sample, state_conv, state_h, cache_k_win, cache_v_win, norm_pre, norm_post, w_in, conv_w, conv_b, lru_w_a, lru_b_a, lru_w_x, lru_b_x, lru_lambda, attn_sinks, w_branch, w_out):
    batch, s, d = x_prompt.shape
    nb, ts, _ = x_sample.shape
    assert batch == 1, "the prompt group is a single sequence"
    depth = w_in.shape[0]
    xp = x_prompt.reshape(s, d)
    xs_tm = x_sample.transpose(1, 0, 2).reshape(ts * nb, d)
    p_states, s_states = [], []
    for l in range(depth):
        xp, xs_tm, p_new, s_new = _layer(
            xp, xs_tm, state_conv[l], state_h[l], cache_k_win[l], cache_v_win[l], norm_pre[l],
            norm_post[l], w_in[l], conv_w[l], conv_b[l], lru_w_a[l], lru_b_a[l], lru_w_x[l],
            lru_b_x[l], lru_lambda[l], attn_sinks[l], w_branch[l], w_out[l])
        p_states.append(p_new)
        s_states.append(s_new)
    y_prompt = xp.reshape(1, s, d)
    y_sample = xs_tm.reshape(ts, nb, d).transpose(1, 0, 2)
    stack = lambda states, k: jnp.stack([st[k] for st in states])
    return (y_prompt, y_sample,
            stack(p_states, 0), stack(p_states, 1), stack(p_states, 2), stack(p_states, 3),
            stack(s_states, 0), stack(s_states, 1), stack(s_states, 2), stack(s_states, 3))
```

```python
import functools

import jax
import jax.numpy as jnp
from jax import lax
from jax.experimental import pallas as pl
from jax.experimental.pallas import tpu as pltpu

HEAD_DIM = 64
WINDOW = 128
CONV_WIDTH = 4
LRU_C = 8.0
RMS_EPS = 1e-6
NEG_INF = -1e30

LANES = 128
SUBLANES = 8
BF16_ROWS = 16
VMEM_LIMIT_BYTES = 58 * 1024 * 1024
IN_PROJ_VMEM_BUDGET = 52 * 1024 * 1024
IN_PROJ_VMEM_LIMIT_BYTES = 63 * 1024 * 1024

F32 = jnp.float32
BF16 = jnp.bfloat16


def _largest_divisor(n, candidates):
    for c in candidates:
        if n % c == 0:
            return c
    raise ValueError(f"no tile in {candidates} divides {n}")


def _params(sem, vmem_limit_bytes=VMEM_LIMIT_BYTES):
    return pltpu.CompilerParams(dimension_semantics=sem, vmem_limit_bytes=vmem_limit_bytes)


def _dot(a, b):
    return jnp.dot(a, b, preferred_element_type=F32)


def _dot_nt(a, b):
    return lax.dot_general(a, b, (((1,), (1,)), ((), ())), preferred_element_type=F32)


def _sigmoid(x):
    return jax.nn.sigmoid(x)


def _silu(x):
    return x * jax.nn.sigmoid(x)


def _rmsnorm_kernel(xp_ref, xs_ref, g_ref, o_ref, *, n_p):
    i = pl.program_id(0)

    def body(x_ref):
        x = x_ref[...]
        ms = jnp.mean(x * x, axis=-1, keepdims=True)
        o_ref[...] = ((x * lax.rsqrt(ms + RMS_EPS)) * g_ref[...]).astype(o_ref.dtype)

    @pl.when(i < n_p)
    def _():
        body(xp_ref)

    @pl.when(i >= n_p)
    def _():
        body(xs_ref)


def _rmsnorm_pre(xp, xs, g, tm):
    s, d = xp.shape
    ms = xs.shape[0]
    n_p, n_s = s // tm, ms // tm
    return pl.pallas_call(
        functools.partial(_rmsnorm_kernel, n_p=n_p),
        out_shape=jax.ShapeDtypeStruct((s + ms, d), BF16),
        grid=(n_p + n_s,),
        in_specs=[
            pl.BlockSpec((tm, d), lambda i: (jnp.minimum(i, n_p - 1), 0)),
            pl.BlockSpec((tm, d), lambda i: (jnp.maximum(i - n_p, 0), 0)),
            pl.BlockSpec((1, d), lambda i: (0, 0)),
        ],
        out_specs=pl.BlockSpec((tm, d), lambda i: (i, 0)),
        compiler_params=_params(("arbitrary",)),
        name="rmsnorm_pre",
    )(xp, xs, g)


def _cast_rows(src_ref, dst_ref, rows_per_step=256):
    n = src_ref.shape[0]
    step = _largest_divisor(n, (rows_per_step, 128, 64, 32, 16))

    def body(r, carry):
        r0 = pl.multiple_of(r * step, step)
        dst_ref[pl.ds(r0, step), :] = src_ref[pl.ds(r0, step), :].astype(dst_ref.dtype)
        return carry

    lax.fori_loop(0, n // step, body, 0)


def _in_proj_kernel(x_ref, w_ref, *rest):
    n_extra = (len(rest) - 2) // 2
    extra_in, o_ref = rest[:n_extra], rest[n_extra]
    extra_out, wb_ref = rest[n_extra + 1:2 * n_extra + 1], rest[-1]

    @pl.when(pl.program_id(1) == 0)
    def _():
        _cast_rows(w_ref, wb_ref)

    o_ref[...] = _dot(x_ref[...], wb_ref[...]).astype(o_ref.dtype)
    for src, dst in zip(extra_in, extra_out):
        dst[...] = src[...].astype(dst.dtype)


def _in_proj_tiles(m, d, n):
    best = None
    for tm in range(BF16_ROWS, m + 1, BF16_ROWS):
        if m % tm:
            continue
        for tn in range(2 * LANES, n + 1, 2 * LANES):
            if n % tn:
                continue
            vmem = d * tn * (2 * 4 + 2) + 2 * tm * d * 2 + 2 * tm * tn * 2
            if vmem <= IN_PROJ_VMEM_BUDGET and (best is None or tm * tn > best[0] * best[1]):
                best = (tm, tn)
    assert best is not None
    return best


def _in_proj(xn, w_in, extra_weights=()):
    m, d = xn.shape
    n = w_in.shape[1]
    tm, tn = _in_proj_tiles(m, d, n)
    n_i = m // tm
    n_steps = (n // tn) * n_i
    extra_specs = []
    for w in extra_weights:
        rows = w.shape[0]
        slab = next(r for r in range(BF16_ROWS, rows + 1, BF16_ROWS)
                    if rows % r == 0 and rows // r <= n_steps)
        last = rows // slab - 1
        extra_specs.append(pl.BlockSpec(
            (slab, w.shape[1]), lambda j, i, last=last: (jnp.minimum(j * n_i + i, last), 0)))
    out = pl.pallas_call(
        _in_proj_kernel,
        out_shape=(jax.ShapeDtypeStruct((m, n), BF16),
                   *[jax.ShapeDtypeStruct(w.shape, BF16) for w in extra_weights]),
        grid=(n // tn, n_i),
        in_specs=[
            pl.BlockSpec((tm, d), lambda j, i: (i, 0)),
            pl.BlockSpec((d, tn), lambda j, i: (0, j)),
            *extra_specs,
        ],
        out_specs=(pl.BlockSpec((tm, tn), lambda j, i: (i, j)), *extra_specs),
        scratch_shapes=[pltpu.VMEM((d, tn), BF16)],
        compiler_params=_params(("arbitrary", "arbitrary"), IN_PROJ_VMEM_LIMIT_BYTES),
        name="in_proj",
    )(xn, w_in, *extra_weights)
    return out[0], out[1:]


LOG2_E = 1.4426950408889634


def _neg_c_softplus(lam):
    return (-LRU_C * LOG2_E) * (jnp.maximum(-lam, 0.0) + jnp.log(1.0 + jnp.exp(-jnp.abs(lam))))


def _lru_conv(taps, w, bias):
    out = bias
    for k in range(CONV_WIDTH):
        out = out + taps[k] * w[k:k + 1, :]
    return out


def _lru_gate_dots(xc, wa_ref, wx_ref):
    blk = wa_ref.shape[1]
    xcb = xc.astype(BF16)
    ga, gx = [], []
    for q in range(xc.shape[1] // blk):
        xq = xcb[:, q * blk:(q + 1) * blk]
        ga.append(_dot(xq, wa_ref[q]))
        gx.append(_dot(xq, wx_ref[q]))
    return jnp.concatenate(ga, axis=1), jnp.concatenate(gx, axis=1)


def _lru_coeffs(xc, ga, gx, b_a, b_x, ncs):
    r_a = _sigmoid(ga + b_a)
    r_x = _sigmoid(gx + b_x)
    a = jnp.exp2(ncs * r_a)
    b = jnp.exp2(0.5 * jnp.log2(1.0 - a * a)) * (r_x * xc)
    return a, b


def _scan_tile(a, b, sub):
    shift = 1
    while shift < SUBLANES:
        keep = sub >= shift
        a_sh = jnp.where(keep, pltpu.roll(a, shift, axis=0), 1.0)
        b_sh = jnp.where(keep, pltpu.roll(b, shift, axis=0), 0.0)
        b = b + a * b_sh
        a = a * a_sh
        shift *= 2
    return a, b


def _lru_prompt_conv(xcur, x_tail, sub, w, bias):
    n_tiles = xcur.shape[0] // SUBLANES
    tiles = [x_tail] + [xcur[v * SUBLANES:(v + 1) * SUBLANES] for v in range(n_tiles)]
    taps = []
    for k in range(CONV_WIDTH):
        back = CONV_WIDTH - 1 - k
        if back == 0:
            taps.append(xcur)
            continue
        rolled = [pltpu.roll(t, back, axis=0) for t in tiles]
        taps.append(jnp.concatenate(
            [jnp.where(sub >= back, rolled[v + 1], rolled[v]) for v in range(n_tiles)], axis=0))
    return _lru_conv(taps, w, bias), tiles[-1]


def _lru_prompt_scan(a, b, h_in, sub):
    hs = []
    for v in range(a.shape[0] // SUBLANES):
        rows = slice(v * SUBLANES, (v + 1) * SUBLANES)
        a_in, b_in = _scan_tile(a[rows], b[rows], sub)
        hv = a_in * h_in + b_in
        hs.append(hv)
        h_in = jnp.broadcast_to(hv[SUBLANES - 1:SUBLANES, :], hv.shape)
    return jnp.concatenate(hs, axis=0), h_in


def _lru_sample_kernel(x_ref, g_ref, cw_ref, cb_ref, wa_ref, ba_ref, wx_ref, bx_ref, lam_ref,
                       cst_ref, h0_ref, o_ref, hs_ref):
    nb = h0_ref.shape[0]
    ts = x_ref.shape[0] // nb
    w, bias = cw_ref[...], cb_ref[...]
    ncs = _neg_c_softplus(lam_ref[...])
    xs = [cst_ref[k] for k in range(CONV_WIDTH - 1)]
    xs += [x_ref[t * nb:(t + 1) * nb, :].astype(F32) for t in range(ts)]
    h = h0_ref[...]
    for t in range(ts):
        xc = _lru_conv(xs[t:t + CONV_WIDTH], w, bias)
        ga, gx = _lru_gate_dots(xc, wa_ref, wx_ref)
        a, b = _lru_coeffs(xc, ga, gx, ba_ref[...], bx_ref[...], ncs)
        h = a * h + b
        g = g_ref[t * nb:(t + 1) * nb, :].astype(F32)
        o_ref[t * nb:(t + 1) * nb, :] = (h * _silu(g)).astype(o_ref.dtype)
    hs_ref[...] = h


def _lru_sample(z, lru_params, cst_tm, h0, s, tb, cbw):
    conv_w, conv_b, w_a, b_a, w_x, b_x, lam = lru_params
    width = conv_w.shape[1]
    nb = h0.shape[0]
    blk = w_a.shape[1]
    n_c = width // cbw
    blk_s = s // tb
    row = lambda c: (0, c)
    gate_w = pl.BlockSpec((cbw // blk, blk, blk), lambda c: (c, 0, 0))
    return pl.pallas_call(
        _lru_sample_kernel,
        out_shape=(jax.ShapeDtypeStruct((tb, width), BF16), jax.ShapeDtypeStruct((nb, width), F32)),
        grid=(n_c,),
        in_specs=[
            pl.BlockSpec((tb, cbw), lambda c: (blk_s, c)),
            pl.BlockSpec((tb, cbw), lambda c: (blk_s, n_c + c)),
            pl.BlockSpec((CONV_WIDTH, cbw), row),
            pl.BlockSpec((1, cbw), row),
            gate_w,
            pl.BlockSpec((1, cbw), row),
            gate_w,
            pl.BlockSpec((1, cbw), row),
            pl.BlockSpec((1, cbw), row),
            pl.BlockSpec((CONV_WIDTH - 1, nb, cbw), lambda c: (0, 0, c)),
            pl.BlockSpec((nb, cbw), row),
        ],
        out_specs=(pl.BlockSpec((tb, cbw), row), pl.BlockSpec((nb, cbw), row)),
        compiler_params=_params(("arbitrary",)),
        name="lru_sample",
    )(z, z, conv_w, conv_b, w_a, b_a, w_x, b_x, lam, cst_tm, h0)


def _attn_prompt_kernel(sink_ref, q_ref, kp_ref, kc_ref, vp_ref, vc_ref, g0_ref, g1_ref, os_ref,
                        o_ref, *, n_q, n_kv, group):
    i = pl.program_id(0)

    @pl.when(i < n_q)
    def _():
        _attn_prompt_block(i, sink_ref, q_ref, kp_ref, kc_ref, vp_ref, vc_ref, g0_ref, g1_ref,
                           o_ref, n_kv, group)

    @pl.when(i >= n_q)
    def _():
        o_ref[...] = os_ref[...]


def _attn_prompt_block(i, sink_ref, q_ref, kp_ref, kc_ref, vp_ref, vc_ref, g0_ref, g1_ref, o_ref,
                       n_kv, group):
    qb = q_ref.shape[0]
    aw_half = g0_ref.shape[1]
    scale = HEAD_DIM ** -0.5
    hw = group * HEAD_DIM
    c = lax.broadcasted_iota(jnp.int32, (2 * qb, qb), 0)
    r = lax.broadcasted_iota(jnp.int32, (2 * qb, qb), 1)
    rel = qb + r - c
    mask = (rel >= 0) & (rel <= WINDOW) & ((c >= qb) | (i > 0))
    lane = lax.broadcasted_iota(jnp.int32, (qb, LANES), 1)
    half_mask = (lane < HEAD_DIM, lane >= HEAD_DIM)
    zero = jnp.zeros((qb, LANES), BF16)
    n_pairs = n_kv // 2
    kv = {}
    out_t = {}

    def heads_of(p, variant):
        return [(2 * p + half, g) for half in range(2) for g in range(group)
                if ((g % 2) == half) == (variant == 0)]

    def load_pair(p):
        lanes = slice(p * LANES, (p + 1) * LANES)
        k_f32 = jnp.concatenate([kp_ref[:, lanes], kc_ref[:, lanes]], axis=0).astype(F32) * scale
        v_pair = jnp.concatenate([vp_ref[:, lanes], vc_ref[:, lanes]], axis=0)
        kv[p] = (k_f32.astype(BF16), pltpu.roll(k_f32, HEAD_DIM, axis=1).astype(BF16),
                 v_pair.astype(F32).T.astype(BF16))

    def scores(p, variant):
        q_rows = []
        for kh, g in heads_of(p, variant):
            c0 = kh * hw + (g // 2) * LANES
            q_rows.append(jnp.where(half_mask[g % 2], q_ref[:, c0:c0 + LANES], zero))
        return _dot_nt(kv[p][variant], jnp.concatenate(q_rows, axis=0))

    def softmax_values(p, variant, s_t):
        heads = heads_of(p, variant)
        p_blocks, inv_den = [], []
        for b, (kh, g) in enumerate(heads):
            sink = sink_ref[kh * group + g]
            sb = jnp.where(mask, s_t[:, b * qb:(b + 1) * qb], NEG_INF)
            m = jnp.maximum(jnp.max(sb, axis=0, keepdims=True), sink)
            pe = jnp.exp(sb - m)
            den = jnp.sum(pe, axis=0, keepdims=True) + jnp.exp(sink - m)
            p_blocks.append(pe.astype(BF16))
            inv_den.append(1.0 / den)
        o_t = _dot(kv[p][2], jnp.concatenate(p_blocks, axis=1))
        for b, (kh, g) in enumerate(heads):
            r0 = (kh % 2) * HEAD_DIM
            out_t[(kh, g)] = o_t[r0:r0 + HEAD_DIM, b * qb:(b + 1) * qb] * inv_den[b]

    def store_pair(p):
        for half in range(2):
            kh = 2 * p + half
            for j in range(group // 2):
                c0 = kh * hw + j * LANES
                tile = jnp.concatenate([out_t[(kh, 2 * j)], out_t[(kh, 2 * j + 1)]], axis=0).T
                g_ref, gc = (g0_ref, c0) if c0 < aw_half else (g1_ref, c0 - aw_half)
                gate = g_ref[:, gc:gc + LANES].astype(F32)
                o_ref[:, c0:c0 + LANES] = (tile * _silu(gate)).astype(o_ref.dtype)

    groups = [(p, variant) for p in range(n_pairs) for variant in range(2)]
    load_pair(0)
    s_next = scores(*groups[0])
    for k, (p, variant) in enumerate(groups):
        s_cur = s_next
        if k + 1 < len(groups):
            if groups[k + 1][0] != p:
                load_pair(groups[k + 1][0])
            s_next = scores(*groups[k + 1])
        softmax_values(p, variant, s_cur)
        if variant == 1:
            store_pair(p)


def _attn_prompt(z, sinks, o_b_s, s, width, aw, kvw, n_kv, group):
    qb = WINDOW
    n_q = s // qb
    assert s % qb == 0 and o_b_s.shape[0] % qb == 0 and n_kv % 2 == 0 and group % 2 == 0
    q_blk = (2 * width) // aw
    k_blk = (2 * width + aw) // kvw
    v_blk = k_blk + 1
    g_blk = (2 * width + aw + 2 * kvw) // (aw // 2)
    assert (2 * width) % aw == 0 and (2 * width + aw) % kvw == 0
    assert (2 * width + aw + 2 * kvw) % (aw // 2) == 0
    cur = lambda i: jnp.minimum(i, n_q - 1)
    prev = lambda i: jnp.maximum(cur(i) - 1, 0)
    return pl.pallas_call(
        functools.partial(_attn_prompt_kernel, n_q=n_q, n_kv=n_kv, group=group),
        out_shape=jax.ShapeDtypeStruct((s + o_b_s.shape[0], aw), BF16),
        grid=(n_q + o_b_s.shape[0] // qb,),
        in_specs=[
            pl.BlockSpec(memory_space=pltpu.SMEM),
            pl.BlockSpec((qb, aw), lambda i: (cur(i), q_blk)),
            pl.BlockSpec((qb, kvw), lambda i: (prev(i), k_blk)),
            pl.BlockSpec((qb, kvw), lambda i: (cur(i), k_blk)),
            pl.BlockSpec((qb, kvw), lambda i: (prev(i), v_blk)),
            pl.BlockSpec((qb, kvw), lambda i: (cur(i), v_blk)),
            pl.BlockSpec((qb, aw // 2), lambda i: (cur(i), g_blk)),
            pl.BlockSpec((qb, aw // 2), lambda i: (cur(i), g_blk + 1)),
            pl.BlockSpec((qb, aw), lambda i: (jnp.maximum(i - n_q, 0), 0)),
        ],
        out_specs=pl.BlockSpec((qb, aw), lambda i: (i, 0)),
        compiler_params=_params(("arbitrary",)),
        name="attn_prompt",
    )(sinks, z, z, z, z, z, z, z, o_b_s)


def _attn_sample_kernel(q_ref, kn_ref, vn_ref, ck_ref, cv_ref, g_ref, sink_ref, o_ref,
                        s_ref, p_ref, *, n_kv, group, ts):
    sb, tq, _ = q_ref.shape
    wb = ck_ref.shape[3]
    tk = kn_ref.shape[1]
    n_keys = s_ref.shape[1]
    n_new = n_keys - wb
    n_pairs = n_kv // 2
    pair_rows = 2 * group * tq
    seq_rows = n_pairs * pair_rows
    scale = HEAD_DIM ** -0.5
    hw = group * HEAD_DIM
    lane = lax.broadcasted_iota(jnp.int32, (tq, LANES), 1)
    new_pad = jnp.zeros((n_new - tk, LANES), F32)

    def cached_t(c_ref, n, p):
        return jnp.concatenate([c_ref[n, 2 * p], c_ref[n, 2 * p + 1]], axis=0).astype(BF16)

    def new_rows(n_ref, n, p):
        lanes = slice(p * LANES, (p + 1) * LANES)
        return jnp.concatenate([n_ref[n, :, lanes], new_pad], axis=0).astype(BF16)

    def score_body(n, carry):
        for p in range(n_pairs):
            pieces = []
            for half in range(2):
                for g in range(group):
                    c0 = (2 * p + half) * hw + (g // 2) * LANES
                    tile = q_ref[n, :, c0:c0 + LANES] * scale
                    if (g % 2) != half:
                        tile = pltpu.roll(tile, HEAD_DIM, axis=1)
                    keep = (lane >= HEAD_DIM * half) & (lane < HEAD_DIM * (half + 1))
                    pieces.append(jnp.where(keep, tile, 0.0))
            lhs = jnp.concatenate(pieces, axis=0).astype(BF16)
            r0 = pl.multiple_of(n * seq_rows + p * pair_rows, pair_rows)
            s_ref[pl.ds(r0, pair_rows), 0:wb] = _dot(lhs, cached_t(ck_ref, n, p))
            s_ref[pl.ds(r0, pair_rows), wb:] = _dot_nt(lhs, new_rows(kn_ref, n, p))
        return carry

    lax.fori_loop(0, sb, score_body, 0)

    t = lax.broadcasted_iota(jnp.int32, (tq, n_keys), 0)
    c = lax.broadcasted_iota(jnp.int32, (tq, n_keys), 1)
    mask8 = ((c < wb) & (t + wb - c <= WINDOW)) | ((c >= wb) & (c - wb <= t) & (c - wb < ts))
    reps = sb * seq_rows // tq
    mask = jnp.concatenate([mask8] * reps, axis=0)
    sink = jnp.concatenate([sink_ref[:, 0:1]] * sb, axis=0)
    s = jnp.where(mask, s_ref[...], NEG_INF)
    m = jnp.maximum(jnp.max(s, axis=-1, keepdims=True), sink)
    pe = jnp.exp(s - m)
    den = jnp.sum(pe, axis=-1, keepdims=True) + jnp.exp(sink - m)
    p_ref[...] = (pe / den).astype(p_ref.dtype)

    def value_body(n, carry):
        for p in range(n_pairs):
            r0 = pl.multiple_of(n * seq_rows + p * pair_rows, pair_rows)
            o = (_dot_nt(p_ref[pl.ds(r0, pair_rows), 0:wb], cached_t(cv_ref, n, p))
                 + _dot(p_ref[pl.ds(r0, pair_rows), wb:], new_rows(vn_ref, n, p)))
            for half in range(2):
                kh = 2 * p + half
                for j in range(group // 2):
                    rows = (half * group + 2 * j) * tq
                    o_lo, o_hi = o[rows:rows + tq], o[rows + tq:rows + 2 * tq]
                    if half == 0:
                        o_hi = pltpu.roll(o_hi, HEAD_DIM, axis=1)
                    else:
                        o_lo = pltpu.roll(o_lo, HEAD_DIM, axis=1)
                    c0 = kh * hw + j * LANES
                    gate = g_ref[n, :, c0:c0 + LANES]
                    o_ref[n, :, c0:c0 + LANES] = jnp.where(lane < HEAD_DIM, o_lo, o_hi) * _silu(gate)
        return carry

    lax.fori_loop(0, sb, value_body, 0)


def _attn_sample(q8, kn, vn, cache_k_t, cache_v_t, g8, sink_rows, n_kv, group, ts):
    nb, tq, aw = q8.shape
    tk, kvw = kn.shape[1], kn.shape[2]
    wb = cache_k_t.shape[3]
    sb = _largest_divisor(nb, (8, 4, 2, 1))
    assert wb % LANES == 0
    n_keys = wb + -(-tk // LANES) * LANES
    rows = sb * (n_kv // 2) * 2 * group * tq
    assert sink_rows.shape[0] * sb == rows
    blk3 = lambda r, width: pl.BlockSpec((sb, r, width), lambda i: (i, 0, 0))
    cache_blk = pl.BlockSpec((sb, n_kv, HEAD_DIM, wb), lambda i: (i, 0, 0, 0))
    return pl.pallas_call(
        functools.partial(_attn_sample_kernel, n_kv=n_kv, group=group, ts=ts),
        out_shape=jax.ShapeDtypeStruct((nb, tq, aw), F32),
        grid=(nb // sb,),
        in_specs=[
            blk3(tq, aw), blk3(tk, kvw), blk3(tk, kvw), cache_blk, cache_blk, blk3(tq, aw),
            pl.BlockSpec(sink_rows.shape, lambda i: (0, 0)),
        ],
        out_specs=blk3(tq, aw),
        scratch_shapes=[pltpu.VMEM((rows, n_keys), F32), pltpu.VMEM((rows, n_keys), BF16)],
        compiler_params=_params(("arbitrary",)),
        name="attn_sample",
    )(q8, kn, vn, cache_k_t, cache_v_t, g8, sink_rows)


PRODUCT_PIECES = (4, 2)


def _branch_lru_kernel(x_ref, g_ref, cw_ref, cb_ref, wa_ref, wx_ref, ba_ref, bx_ref, lam_ref,
                       oas_ref, ob_ref, w_ref, ma_ref, mb_ref,
                       o_ref, hp_ref, oa_ref, xt_ref, hc_ref, xc_ref, ga_ref, gx_ref,
                       *, n_p, n_t, chunk):
    u = pl.program_id(0)
    tm, cbw = x_ref.shape

    @pl.when(u == 0)
    def _():
        for ref in (oa_ref, xt_ref, hc_ref, xc_ref, ga_ref, gx_ref):
            ref[...] = jnp.zeros_like(ref)

    unit_t = jnp.maximum(u - 1, 0)
    row_t, ch_t = lax.div(unit_t, n_t), lax.rem(unit_t, n_t)
    live_t = (u >= 1) & (unit_t < n_p * n_t)
    row_d = lax.div(jnp.maximum(u - 1 - n_t, 0), n_t)
    ch_c = lax.rem(u, n_t)
    live_c = u < n_p * n_t

    slot_d = lax.rem(row_d, 2)
    width = n_t * cbw
    tn = o_ref.shape[1]
    n_chunks = tm // chunk
    row_parts, col_parts = PRODUCT_PIECES
    pieces = [(rh, ch) for rh in range(row_parts) for ch in range(col_parts)]
    chunks_per_piece = n_chunks // len(pieces)
    assert chunks_per_piece * len(pieces) == n_chunks

    def product_piece(rh, ch):
        rows = slice(rh * tm // row_parts, (rh + 1) * tm // row_parts)
        cols = slice(ch * tn // col_parts, (ch + 1) * tn // col_parts)
        o_a = jnp.concatenate([oa_ref[slot_d, q, rows, :] for q in range(n_t)], axis=1)
        pa = _dot(o_a, w_ref[0:width, cols])
        pb = _dot(ob_ref[rows, :], w_ref[width:, cols])
        ma = ma_ref[rows, cols].astype(F32)
        mb = mb_ref[rows, cols].astype(F32)
        o_ref[rows, cols] = (_sigmoid(ma) * pa + _sigmoid(mb) * pb).astype(o_ref.dtype)

    sub = lax.broadcasted_iota(jnp.int32, (SUBLANES, cbw), 0)
    stage_w = lax.rem(u, 2)
    stage_r = 1 - stage_w

    slot_t = lax.rem(row_t, 2)
    cg = min(cbw, 2 * LANES)
    groups = [slice(k * cg, (k + 1) * cg) for k in range(cbw // cg)]
    sub_g = lax.broadcasted_iota(jnp.int32, (SUBLANES, cg), 0)
    coef = [(ba_ref[:, gs], bx_ref[:, gs], _neg_c_softplus(lam_ref[:, gs])) for gs in groups]
    h_ins = [hc_ref[ch_t, :, gs] for gs in groups]
    for c in range(n_chunks):
        if c % chunks_per_piece == 0:
            product_piece(*pieces[c // chunks_per_piece])
        rows = slice(c * chunk, (c + 1) * chunk)
        for k, gs in enumerate(groups):
            a, b = _lru_coeffs(xc_ref[stage_r, rows, gs], ga_ref[stage_r, rows, gs],
                               gx_ref[stage_r, rows, gs], *coef[k])
            h, h_ins[k] = _lru_prompt_scan(a, b, h_ins[k], sub_g)
            o_a_new = (h * _silu(g_ref[rows, gs].astype(F32))).astype(oa_ref.dtype)
            oa_ref[slot_t, ch_t, rows, gs] = jnp.where(live_t, o_a_new, oas_ref[rows, gs])
    for k, gs in enumerate(groups):
        h_in = jnp.where(live_t, h_ins[k], hc_ref[ch_t, :, gs])
        hc_ref[ch_t, :, gs] = h_in
        hp_ref[ch_t, :, gs] = h_in[0:1, :]

    w, bias = cw_ref[...], cb_ref[...]
    x_tail = xt_ref[ch_c]
    xcs = []
    for c in range(n_chunks):
        rows = slice(c * chunk, (c + 1) * chunk)
        xc, x_tail = _lru_prompt_conv(x_ref[rows, :].astype(F32), x_tail, sub, w, bias)
        xcs.append(xc)
    xt_ref[ch_c] = jnp.where(live_c, x_tail, xt_ref[ch_c])
    xc_all = jnp.concatenate(xcs, axis=0)
    ga, gx = _lru_gate_dots(xc_all, wa_ref, wx_ref)
    xc_ref[stage_w] = xc_all
    ga_ref[stage_w] = ga
    gx_ref[stage_w] = gx


def _branch_lru(z, lru_params, o_a_s, o_b, w_branch_bf16, ma_col, s, tm, tn):
    conv_w, conv_b, w_a, b_a, w_x, b_x, lam = lru_params
    m = z.shape[0]
    width = conv_w.shape[1]
    blk = w_a.shape[1]
    aw = o_b.shape[1]
    d = w_branch_bf16.shape[1]
    n_p = s // tm
    n_r = m // tm
    n_t = d // tn
    cbw = width // n_t
    assert cbw % blk == 0 and cbw % LANES == 0
    chunk = tm // 8
    assert chunk % SUBLANES == 0
    ma_blk = ma_col // tn
    mb_blk = (ma_col + d) // tn
    row_c = lambda u: jnp.minimum(u // n_t, n_p - 1)
    ch_c = lambda u: u % n_t
    unit_t = lambda u: jnp.maximum(u - 1, 0)
    row_t = lambda u: jnp.minimum(unit_t(u) // n_t, n_p - 1)
    ch_t = lambda u: unit_t(u) % n_t
    unit_d = lambda u: jnp.maximum(u - 1 - n_t, 0)
    row_d = lambda u: unit_d(u) // n_t
    tile_d = lambda u: unit_d(u) % n_t
    gate_w = pl.BlockSpec((cbw // blk, blk, blk), lambda u: (ch_c(u), 0, 0))
    chan_t = lambda rows: pl.BlockSpec((rows, cbw), lambda u: (0, ch_t(u)))
    return pl.pallas_call(
        functools.partial(_branch_lru_kernel, n_p=n_p, n_t=n_t, chunk=chunk),
        out_shape=(jax.ShapeDtypeStruct((m, d), BF16), jax.ShapeDtypeStruct((n_t, 1, cbw), F32)),
        grid=((n_r + 1) * n_t + 1,),
        in_specs=[
            pl.BlockSpec((tm, cbw), lambda u: (row_c(u), ch_c(u))),
            pl.BlockSpec((tm, cbw), lambda u: (row_t(u), n_t + ch_t(u))),
            pl.BlockSpec((CONV_WIDTH, cbw), lambda u: (0, ch_c(u))),
            pl.BlockSpec((1, cbw), lambda u: (0, ch_c(u))),
            gate_w,
            gate_w,
            chan_t(1),
            chan_t(1),
            chan_t(1),
            chan_t(tm),
            pl.BlockSpec((tm, aw), lambda u: (row_d(u), 0)),
            pl.BlockSpec((width + aw, tn), lambda u: (0, tile_d(u))),
            pl.BlockSpec((tm, tn), lambda u: (row_d(u), ma_blk + tile_d(u))),
            pl.BlockSpec((tm, tn), lambda u: (row_d(u), mb_blk + tile_d(u))),
        ],
        out_specs=(pl.BlockSpec((tm, tn), lambda u: (row_d(u), tile_d(u))),
                   pl.BlockSpec((n_t, 1, cbw), lambda u: (0, 0, 0))),
        scratch_shapes=[
            pltpu.VMEM((2, n_t, tm, cbw), BF16),
            pltpu.VMEM((n_t, SUBLANES, cbw), F32),
            pltpu.VMEM((n_t, SUBLANES, cbw), F32),
            pltpu.VMEM((2, tm, cbw), F32),
            pltpu.VMEM((2, tm, cbw), F32),
            pltpu.VMEM((2, tm, cbw), F32),
        ],
        compiler_params=_params(("arbitrary",)),
        name="branch_lru",
    )(z, z, conv_w, conv_b, w_a, w_x, b_a, b_x, lam, o_a_s, o_b, w_branch_bf16, z, z)


def _out_kernel(m_ref, w_ref, g_ref, x_ref, y_ref, acc_ref, ss_ref, *, d):
    i = pl.program_id(0)
    j = pl.program_id(1)
    slot = lax.rem(i, 2)
    prev = 1 - slot

    @pl.when((i == 0) & (j == 0))
    def _():
        acc_ref[1] = jnp.zeros(acc_ref.shape[1:], F32)
        ss_ref[...] = jnp.zeros_like(ss_ref)

    inv = lax.rsqrt(ss_ref[prev] / d + RMS_EPS)
    y_ref[...] = x_ref[...] + (acc_ref[prev, j] * inv) * g_ref[...]

    t = _dot(m_ref[...], w_ref[...])
    acc_ref[slot, j] = t
    ssq = jnp.sum(t * t, axis=-1, keepdims=True)
    ss_ref[slot] = jnp.where(j == 0, ssq, ss_ref[slot] + ssq)


def _out_proj(merged, w_out_bf16, g, x, row0, tm, tn):
    rows, d = x.shape
    n_r = rows // tm
    n_t = d // tn
    blk0 = row0 // tm
    xy_map = lambda i, j: (jnp.maximum(i - 1, 0), jnp.where(i == 0, 0, j))
    return pl.pallas_call(
        functools.partial(_out_kernel, d=d),
        out_shape=jax.ShapeDtypeStruct((rows, d), F32),
        grid=(n_r + 1, n_t),
        in_specs=[
            pl.BlockSpec((tm, d), lambda i, j: (blk0 + jnp.minimum(i, n_r - 1), 0)),
            pl.BlockSpec((d, tn), lambda i, j: (0, jnp.where(i == n_r, n_t - 1, j))),
            pl.BlockSpec((1, tn), lambda i, j: (0, j)),
            pl.BlockSpec((tm, tn), xy_map),
        ],
        out_specs=pl.BlockSpec((tm, tn), xy_map),
        scratch_shapes=[pltpu.VMEM((2, n_t, tm, tn), F32), pltpu.VMEM((2, tm, 1), F32)],
        compiler_params=_params(("arbitrary", "arbitrary")),
        name="out_proj",
    )(merged, w_out_bf16, g, x)


def _pad_rows(x, rows):
    return jnp.pad(x, ((0, 0), (0, rows - x.shape[1]), (0, 0)))


def _layer(xp, xs_tm, conv_state, h_state, cache_k, cache_v, norm_pre, norm_post, w_in, conv_w,
           conv_b, w_a, b_a, w_x, b_x, lam, sinks, w_branch, w_out):
    s, d = xp.shape
    nb, wb, n_kv, _ = cache_k.shape
    ts = xs_tm.shape[0] // nb
    width = conv_w.shape[1]
    n_heads = sinks.shape[0]
    group = n_heads // n_kv
    aw, kvw = n_heads * HEAD_DIM, n_kv * HEAD_DIM
    m = s + nb * ts
    tm = nb * ts
    assert s % tm == 0 and tm % SUBLANES == 0 and ts >= CONV_WIDTH - 1
    q_col = 2 * width
    k_col = q_col + aw
    v_col = k_col + kvw
    g_col = v_col + kvw
    ma_col = g_col + aw
    assert w_in.shape[1] == ma_col + 2 * d

    xn = _rmsnorm_pre(xp, xs_tm, norm_pre.reshape(1, d), _largest_divisor(tm, (512, 256, 128, 64, 8)))
    z, (w_branch_bf16, w_out_bf16) = _in_proj(xn, w_in, (w_branch, w_out))

    tn = _largest_divisor(ma_col, (512, 256, 128))
    assert d % tn == 0 and width % (d // tn) == 0
    lru_params = (conv_w, conv_b.reshape(1, width), w_a.astype(BF16), b_a.reshape(1, width),
                  w_x.astype(BF16), b_x.reshape(1, width), lam.reshape(1, width))
    o_a_s, h_s = _lru_sample(z, lru_params, conv_state.transpose(1, 0, 2), h_state, s, tm,
                             width // (d // tn))

    def sample_cols(c0, c1):
        blk = lax.slice(z, (s, c0), (m, c1)).reshape(ts, nb, c1 - c0)
        return blk.transpose(1, 0, 2).astype(F32)

    q_s, k_s, v_s = sample_cols(q_col, k_col), sample_cols(k_col, v_col), sample_cols(v_col, g_col)
    g_s = sample_cols(g_col, ma_col)
    tq = -(-ts // SUBLANES) * SUBLANES
    tk = -(-ts // BF16_ROWS) * BF16_ROWS
    ck = cache_k.reshape(nb, wb, kvw)
    cv = cache_v.reshape(nb, wb, kvw)
    sink_rows = jnp.broadcast_to(sinks.reshape(n_heads, 1, 1), (n_heads, tq, LANES))
    sink_rows = sink_rows.reshape(n_heads * tq, LANES)
    heads_t = lambda c: c.transpose(0, 2, 3, 1)
    o_b_s = _attn_sample(_pad_rows(q_s, tq), _pad_rows(k_s, tk), _pad_rows(v_s, tk),
                         heads_t(cache_k), heads_t(cache_v), _pad_rows(g_s, tq), sink_rows,
                         n_kv, group, ts)
    o_b_s = o_b_s[:, :ts].transpose(1, 0, 2).reshape(nb * ts, aw).astype(BF16)
    o_b = _attn_prompt(z, sinks, o_b_s, s, width, aw, kvw, n_kv, group)

    merged, h_p = _branch_lru(z, lru_params, o_a_s, o_b, w_branch_bf16, ma_col, s, tm, tn)
    h_p = h_p.reshape(1, width)
    g_post = norm_post.reshape(1, d)
    tn_out = _largest_divisor(d, (1024, 512, 256, 128))
    y_p = _out_proj(merged, w_out_bf16, g_post, xp, 0, tm, tn_out)
    y_s = _out_proj(merged, w_out_bf16, g_post, xs_tm, s, tm // 2, tn_out)

    keep = CONV_WIDTH - 1
    new_conv_p = lax.slice(z, (s - keep, 0), (s, width)).astype(F32)[None]
    wbp = min(WINDOW, s)
    new_k_p = lax.slice(z, (s - wbp, k_col), (s, v_col)).astype(F32).reshape(1, wbp, n_kv, HEAD_DIM)
    new_v_p = lax.slice(z, (s - wbp, v_col), (s, g_col)).astype(F32).reshape(1, wbp, n_kv, HEAD_DIM)
    x_lru_s = sample_cols(0, width)
    new_conv_s = jnp.concatenate([conv_state, x_lru_s], axis=1)[:, -keep:]
    new_k_s = jnp.concatenate([ck, k_s], axis=1)[:, -wb:].reshape(nb, wb, n_kv, HEAD_DIM)
    new_v_s = jnp.concatenate([cv, v_s], axis=1)[:, -wb:].reshape(nb, wb, n_kv, HEAD_DIM)
    return y_p, y_s, (new_conv_p, h_p, new_k_p, new_v_p), (new_conv_s, h_s, new_k_s, new_v_s)


def kernel(x_prompt, x_sample, state_conv, state_h, cache_k_win, cache_v_win, norm_pre, norm_post, w_in, conv_w, conv_b, lru_w_a, lru_b_a, lru_w_x, lru_b_x, lru_lambda, attn_sinks, w_branch, w_out):
    batch, s, d = x_prompt.shape
    nb, ts, _ = x_sample.shape
    assert batch == 1, "the prompt group is a single sequence"
    depth = w_in.shape[0]
    xp = x_prompt.reshape(s, d)
    xs_tm = x_sample.transpose(1, 0, 2).reshape(ts * nb, d)
    p_states, s_states = [], []
    for l in range(depth):
        xp, xs_tm, p_new, s_new = _layer(
            xp, xs_tm, state_conv[l], state_h[l], cache_k_win[l], cache_v_win[l], norm_pre[l],
            norm_post[l], w_in[l], conv_w[l], conv_b[l], lru_w_a[l], lru_b_a[l], lru_w_x[l],
            lru_b_x[l], lru_lambda[l], attn_sinks[l], w_branch[l], w_out[l])
        p_states.append(p_new)
        s_states.append(s_new)
    y_prompt = xp.reshape(1, s, d)
    y_sample = xs_tm.reshape(ts, nb, d).transpose(1, 0, 2)
    stack = lambda states, k: jnp.stack([st[k] for st in states])
    return (y_prompt, y_sample,
            stack(p_states, 0), stack(p_states, 1), stack(p_states, 2), stack(p_states, 3),
            stack(s_states, 0), stack(s_states, 1), stack(s_states, 2), stack(s_states, 3))
```

```python
import functools

import jax
import jax.numpy as jnp
from jax import lax
from jax.experimental import pallas as pl
from jax.experimental.pallas import tpu as pltpu

HEAD_DIM = 64
WINDOW = 128
CONV_WIDTH = 4
LRU_C = 8.0
RMS_EPS = 1e-6
NEG_INF = -1e30

LANES = 128
SUBLANES = 8
BF16_ROWS = 16
VMEM_LIMIT_BYTES = 58 * 1024 * 1024
IN_PROJ_VMEM_BUDGET = 52 * 1024 * 1024
IN_PROJ_VMEM_LIMIT_BYTES = 63 * 1024 * 1024

F32 = jnp.float32
BF16 = jnp.bfloat16


def _largest_divisor(n, candidates):
    for c in candidates:
        if n % c == 0:
            return c
    raise ValueError(f"no tile in {candidates} divides {n}")


def _params(sem, vmem_limit_bytes=VMEM_LIMIT_BYTES):
    return pltpu.CompilerParams(dimension_semantics=sem, vmem_limit_bytes=vmem_limit_bytes)


def _dot(a, b):
    return jnp.dot(a, b, preferred_element_type=F32)


def _dot_nt(a, b):
    return lax.dot_general(a, b, (((1,), (1,)), ((), ())), preferred_element_type=F32)


def _sigmoid(x):
    return jax.nn.sigmoid(x)


def _silu(x):
    return x * jax.nn.sigmoid(x)


def _rmsnorm_kernel(xp_ref, xs_ref, g_ref, o_ref, *, n_p):
    i = pl.program_id(0)

    def body(x_ref):
        x = x_ref[...]
        ms = jnp.mean(x * x, axis=-1, keepdims=True)
        o_ref[...] = ((x * lax.rsqrt(ms + RMS_EPS)) * g_ref[...]).astype(o_ref.dtype)

    @pl.when(i < n_p)
    def _():
        body(xp_ref)

    @pl.when(i >= n_p)
    def _():
        body(xs_ref)


def _rmsnorm_pre(xp, xs, g, tm):
    s, d = xp.shape
    ms = xs.shape[0]
    n_p, n_s = s // tm, ms // tm
    return pl.pallas_call(
        functools.partial(_rmsnorm_kernel, n_p=n_p),
        out_shape=jax.ShapeDtypeStruct((s + ms, d), BF16),
        grid=(n_p + n_s,),
        in_specs=[
            pl.BlockSpec((tm, d), lambda i: (jnp.minimum(i, n_p - 1), 0)),
            pl.BlockSpec((tm, d), lambda i: (jnp.maximum(i - n_p, 0), 0)),
            pl.BlockSpec((1, d), lambda i: (0, 0)),
        ],
        out_specs=pl.BlockSpec((tm, d), lambda i: (i, 0)),
        compiler_params=_params(("arbitrary",)),
        name="rmsnorm_pre",
    )(xp, xs, g)


def _cast_rows(src_ref, dst_ref, rows_per_step=256):
    n = src_ref.shape[0]
    step = _largest_divisor(n, (rows_per_step, 128, 64, 32, 16))

    def body(r, carry):
        r0 = pl.multiple_of(r * step, step)
        dst_ref[pl.ds(r0, step), :] = src_ref[pl.ds(r0, step), :].astype(dst_ref.dtype)
        return carry

    lax.fori_loop(0, n // step, body, 0)


def _in_proj_kernel(x_ref, w_ref, *rest):
    n_extra = (len(rest) - 2) // 2
    extra_in, o_ref = rest[:n_extra], rest[n_extra]
    extra_out, wb_ref = rest[n_extra + 1:2 * n_extra + 1], rest[-1]

    @pl.when(pl.program_id(1) == 0)
    def _():
        _cast_rows(w_ref, wb_ref)

    o_ref[...] = _dot(x_ref[...], wb_ref[...]).astype(o_ref.dtype)
    for src, dst in zip(extra_in, extra_out):
        dst[...] = src[...].astype(dst.dtype)


def _in_proj_tiles(m, d, n):
    best = None
    for tm in range(BF16_ROWS, m + 1, BF16_ROWS):
        if m % tm:
            continue
        for tn in range(2 * LANES, n + 1, 2 * LANES):
            if n % tn:
                continue
            vmem = d * tn * (2 * 4 + 2) + 2 * tm * d * 2 + 2 * tm * tn * 2
            if vmem <= IN_PROJ_VMEM_BUDGET and (best is None or tm * tn > best[0] * best[1]):
                best = (tm, tn)
    assert best is not None
    return best


def _in_proj(xn, w_in, extra_weights=()):
    m, d = xn.shape
    n = w_in.shape[1]
    tm, tn = _in_proj_tiles(m, d, n)
    n_i = m // tm
    n_steps = (n // tn) * n_i
    extra_specs = []
    for w in extra_weights:
        rows = w.shape[0]
        slab = next(r for r in range(BF16_ROWS, rows + 1, BF16_ROWS)
                    if rows % r == 0 and rows // r <= n_steps)
        last = rows // slab - 1
        extra_specs.append(pl.BlockSpec(
            (slab, w.shape[1]), lambda j, i, last=last: (jnp.minimum(j * n_i + i, last), 0)))
    out = pl.pallas_call(
        _in_proj_kernel,
        out_shape=(jax.ShapeDtypeStruct((m, n), BF16),
                   *[jax.ShapeDtypeStruct(w.shape, BF16) for w in extra_weights]),
        grid=(n // tn, n_i),
        in_specs=[
            pl.BlockSpec((tm, d), lambda j, i: (i, 0)),
            pl.BlockSpec((d, tn), lambda j, i: (0, j)),
            *extra_specs,
        ],
        out_specs=(pl.BlockSpec((tm, tn), lambda j, i: (i, j)), *extra_specs),
        scratch_shapes=[pltpu.VMEM((d, tn), BF16)],
        compiler_params=_params(("arbitrary", "arbitrary"), IN_PROJ_VMEM_LIMIT_BYTES),
        name="in_proj",
    )(xn, w_in, *extra_weights)
    return out[0], out[1:]


LOG2_E = 1.4426950408889634


def _neg_c_softplus(lam):
    return (-LRU_C * LOG2_E) * (jnp.maximum(-lam, 0.0) + jnp.log(1.0 + jnp.exp(-jnp.abs(lam))))


def _lru_conv(taps, w, bias):
    out = bias
    for k in range(CONV_WIDTH):
        out = out + taps[k] * w[k:k + 1, :]
    return out


def _lru_gate_dots(xc, wa_ref, wx_ref):
    blk = wa_ref.shape[1]
    xcb = xc.astype(BF16)
    ga, gx = [], []
    for q in range(xc.shape[1] // blk):
        xq = xcb[:, q * blk:(q + 1) * blk]
        ga.append(_dot(xq, wa_ref[q]))
        gx.append(_dot(xq, wx_ref[q]))
    return jnp.concatenate(ga, axis=1), jnp.concatenate(gx, axis=1)


def _lru_coeffs(xc, ga, gx, b_a, b_x, ncs):
    r_a = _sigmoid(ga + b_a)
    r_x = _sigmoid(gx + b_x)
    a = jnp.exp2(ncs * r_a)
    b = jnp.exp2(0.5 * jnp.log2(1.0 - a * a)) * (r_x * xc)
    return a, b


def _scan_tile(a, b, sub):
    shift = 1
    while shift < SUBLANES:
        keep = sub >= shift
        a_sh = jnp.where(keep, pltpu.roll(a, shift, axis=0), 1.0)
        b_sh = jnp.where(keep, pltpu.roll(b, shift, axis=0), 0.0)
        b = b + a * b_sh
        a = a * a_sh
        shift *= 2
    return a, b


def _lru_prompt_conv(xcur, x_tail, sub, w, bias):
    n_tiles = xcur.shape[0] // SUBLANES
    tiles = [x_tail] + [xcur[v * SUBLANES:(v + 1) * SUBLANES] for v in range(n_tiles)]
    taps = []
    for k in range(CONV_WIDTH):
        back = CONV_WIDTH - 1 - k
        if back == 0:
            taps.append(xcur)
            continue
        rolled = [pltpu.roll(t, back, axis=0) for t in tiles]
        taps.append(jnp.concatenate(
            [jnp.where(sub >= back, rolled[v + 1], rolled[v]) for v in range(n_tiles)], axis=0))
    return _lru_conv(taps, w, bias), tiles[-1]


def _lru_prompt_scan(a, b, h_in, sub):
    hs = []
    for v in range(a.shape[0] // SUBLANES):
        rows = slice(v * SUBLANES, (v + 1) * SUBLANES)
        a_in, b_in = _scan_tile(a[rows], b[rows], sub)
        hv = a_in * h_in + b_in
        hs.append(hv)
        h_in = jnp.broadcast_to(hv[SUBLANES - 1:SUBLANES, :], hv.shape)
    return jnp.concatenate(hs, axis=0), h_in


def _lru_sample_kernel(x_ref, g_ref, cw_ref, cb_ref, wa_ref, ba_ref, wx_ref, bx_ref, lam_ref,
                       cst_ref, h0_ref, o_ref, hs_ref):
    nb = h0_ref.shape[0]
    ts = x_ref.shape[0] // nb
    w, bias = cw_ref[...], cb_ref[...]
    ncs = _neg_c_softplus(lam_ref[...])
    xs = [cst_ref[k] for k in range(CONV_WIDTH - 1)]
    xs += [x_ref[t * nb:(t + 1) * nb, :].astype(F32) for t in range(ts)]
    h = h0_ref[...]
    for t in range(ts):
        xc = _lru_conv(xs[t:t + CONV_WIDTH], w, bias)
        ga, gx = _lru_gate_dots(xc, wa_ref, wx_ref)
        a, b = _lru_coeffs(xc, ga, gx, ba_ref[...], bx_ref[...], ncs)
        h = a * h + b
        g = g_ref[t * nb:(t + 1) * nb, :].astype(F32)
        o_ref[t * nb:(t + 1) * nb, :] = (h * _silu(g)).astype(o_ref.dtype)
    hs_ref[...] = h


def _lru_sample(z, lru_params, cst_tm, h0, s, tb, cbw):
    conv_w, conv_b, w_a, b_a, w_x, b_x, lam = lru_params
    width = conv_w.shape[1]
    nb = h0.shape[0]
    blk = w_a.shape[1]
    n_c = width // cbw
    blk_s = s // tb
    row = lambda c: (0, c)
    gate_w = pl.BlockSpec((cbw // blk, blk, blk), lambda c: (c, 0, 0))
    return pl.pallas_call(
        _lru_sample_kernel,
        out_shape=(jax.ShapeDtypeStruct((tb, width), BF16), jax.ShapeDtypeStruct((nb, width), F32)),
        grid=(n_c,),
        in_specs=[
            pl.BlockSpec((tb, cbw), lambda c: (blk_s, c)),
            pl.BlockSpec((tb, cbw), lambda c: (blk_s, n_c + c)),
            pl.BlockSpec((CONV_WIDTH, cbw), row),
            pl.BlockSpec((1, cbw), row),
            gate_w,
            pl.BlockSpec((1, cbw), row),
            gate_w,
            pl.BlockSpec((1, cbw), row),
            pl.BlockSpec((1, cbw), row),
            pl.BlockSpec((CONV_WIDTH - 1, nb, cbw), lambda c: (0, 0, c)),
            pl.BlockSpec((nb, cbw), row),
        ],
        out_specs=(pl.BlockSpec((tb, cbw), row), pl.BlockSpec((nb, cbw), row)),
        compiler_params=_params(("arbitrary",)),
        name="lru_sample",
    )(z, z, conv_w, conv_b, w_a, b_a, w_x, b_x, lam, cst_tm, h0)


def _attn_prompt_kernel(sink_ref, q_ref, kp_ref, kc_ref, vp_ref, vc_ref, g0_ref, g1_ref, os_ref,
                        o_ref, *, n_q, n_kv, group):
    i = pl.program_id(0)

    @pl.when(i < n_q)
    def _():
        _attn_prompt_block(i, sink_ref, q_ref, kp_ref, kc_ref, vp_ref, vc_ref, g0_ref, g1_ref,
                           o_ref, n_kv, group)

    @pl.when(i >= n_q)
    def _():
        o_ref[...] = os_ref[...]


def _attn_prompt_block(i, sink_ref, q_ref, kp_ref, kc_ref, vp_ref, vc_ref, g0_ref, g1_ref, o_ref,
                       n_kv, group):
    qb = q_ref.shape[0]
    aw_half = g0_ref.shape[1]
    scale = HEAD_DIM ** -0.5
    hw = group * HEAD_DIM
    c = lax.broadcasted_iota(jnp.int32, (2 * qb, qb), 0)
    r = lax.broadcasted_iota(jnp.int32, (2 * qb, qb), 1)
    rel = qb + r - c
    mask = (rel >= 0) & (rel <= WINDOW) & ((c >= qb) | (i > 0))
    lane = lax.broadcasted_iota(jnp.int32, (qb, LANES), 1)
    half_mask = (lane < HEAD_DIM, lane >= HEAD_DIM)
    zero = jnp.zeros((qb, LANES), BF16)
    n_pairs = n_kv // 2
    kv = {}
    out_t = {}

    def heads_of(p, variant):
        return [(2 * p + half, g) for half in range(2) for g in range(group)
                if ((g % 2) == half) == (variant == 0)]

    def load_pair(p):
        lanes = slice(p * LANES, (p + 1) * LANES)
        k_f32 = jnp.concatenate([kp_ref[:, lanes], kc_ref[:, lanes]], axis=0).astype(F32) * scale
        v_pair = jnp.concatenate([vp_ref[:, lanes], vc_ref[:, lanes]], axis=0)
        kv[p] = (k_f32.astype(BF16), pltpu.roll(k_f32, HEAD_DIM, axis=1).astype(BF16),
                 v_pair.astype(F32).T.astype(BF16))

    def scores(p, variant):
        q_rows = []
        for kh, g in heads_of(p, variant):
            c0 = kh * hw + (g // 2) * LANES
            q_rows.append(jnp.where(half_mask[g % 2], q_ref[:, c0:c0 + LANES], zero))
        return _dot_nt(kv[p][variant], jnp.concatenate(q_rows, axis=0))

    def softmax_values(p, variant, s_t):
        heads = heads_of(p, variant)
        p_blocks, inv_den = [], []
        for b, (kh, g) in enumerate(heads):
            sink = sink_ref[kh * group + g]
            sb = jnp.where(mask, s_t[:, b * qb:(b + 1) * qb], NEG_INF)
            m = jnp.maximum(jnp.max(sb, axis=0, keepdims=True), sink)
            pe = jnp.exp(sb - m)
            den = jnp.sum(pe, axis=0, keepdims=True) + jnp.exp(sink - m)
            p_blocks.append(pe.astype(BF16))
            inv_den.append(1.0 / den)
        o_t = _dot(kv[p][2], jnp.concatenate(p_blocks, axis=1))
        for b, (kh, g) in enumerate(heads):
            r0 = (kh % 2) * HEAD_DIM
            out_t[(kh, g)] = o_t[r0:r0 + HEAD_DIM, b * qb:(b + 1) * qb] * inv_den[b]

    def store_pair(p):
        for half in range(2):
            kh = 2 * p + half
            for j in range(group // 2):
                c0 = kh * hw + j * LANES
                tile = jnp.concatenate([out_t[(kh, 2 * j)], out_t[(kh, 2 * j + 1)]], axis=0).T
                g_ref, gc = (g0_ref, c0) if c0 < aw_half else (g1_ref, c0 - aw_half)
                gate = g_ref[:, gc:gc + LANES].astype(F32)
                o_ref[:, c0:c0 + LANES] = (tile * _silu(gate)).astype(o_ref.dtype)

    groups = [(p, variant) for p in range(n_pairs) for variant in range(2)]
    load_pair(0)
    s_next = scores(*groups[0])
    for k, (p, variant) in enumerate(groups):
        s_cur = s_next
        if k + 1 < len(groups):
            if groups[k + 1][0] != p:
                load_pair(groups[k + 1][0])
            s_next = scores(*groups[k + 1])
        softmax_values(p, variant, s_cur)
        if variant == 1:
            store_pair(p)


def _attn_prompt(z, sinks, o_b_s, s, width, aw, kvw, n_kv, group):
    qb = WINDOW
    n_q = s // qb
    assert s % qb == 0 and o_b_s.shape[0] % qb == 0 and n_kv % 2 == 0 and group % 2 == 0
    q_blk = (2 * width) // aw
    k_blk = (2 * width + aw) // kvw
    v_blk = k_blk + 1
    g_blk = (2 * width + aw + 2 * kvw) // (aw // 2)
    assert (2 * width) % aw == 0 and (2 * width + aw) % kvw == 0
    assert (2 * width + aw + 2 * kvw) % (aw // 2) == 0
    cur = lambda i: jnp.minimum(i, n_q - 1)
    prev = lambda i: jnp.maximum(cur(i) - 1, 0)
    return pl.pallas_call(
        functools.partial(_attn_prompt_kernel, n_q=n_q, n_kv=n_kv, group=group),
        out_shape=jax.ShapeDtypeStruct((s + o_b_s.shape[0], aw), BF16),
        grid=(n_q + o_b_s.shape[0] // qb,),
        in_specs=[
            pl.BlockSpec(memory_space=pltpu.SMEM),
            pl.BlockSpec((qb, aw), lambda i: (cur(i), q_blk)),
            pl.BlockSpec((qb, kvw), lambda i: (prev(i), k_blk)),
            pl.BlockSpec((qb, kvw), lambda i: (cur(i), k_blk)),
            pl.BlockSpec((qb, kvw), lambda i: (prev(i), v_blk)),
            pl.BlockSpec((qb, kvw), lambda i: (cur(i), v_blk)),
            pl.BlockSpec((qb, aw // 2), lambda i: (cur(i), g_blk)),
            pl.BlockSpec((qb, aw // 2), lambda i: (cur(i), g_blk + 1)),
            pl.BlockSpec((qb, aw), lambda i: (jnp.maximum(i - n_q, 0), 0)),
        ],
        out_specs=pl.BlockSpec((qb, aw), lambda i: (i, 0)),
        compiler_params=_params(("arbitrary",)),
        name="attn_prompt",
    )(sinks, z, z, z, z, z, z, z, o_b_s)


def _attn_sample_kernel(q_ref, kn_ref, vn_ref, ck_ref, cv_ref, g_ref, sink_ref, o_ref,
                        nk_ref, nv_ref, s_ref, p_ref, *, n_kv, group, ts):
    sb, tq, _ = q_ref.shape
    wb = ck_ref.shape[3]
    tk = kn_ref.shape[1]
    n_keys = s_ref.shape[1]
    n_new = n_keys - wb
    n_pairs = n_kv // 2
    pair_rows = 2 * group * tq
    seq_rows = n_pairs * pair_rows
    scale = HEAD_DIM ** -0.5
    hw = group * HEAD_DIM
    lane = lax.broadcasted_iota(jnp.int32, (tq, LANES), 1)
    new_pad = jnp.zeros((n_new - tk, LANES), F32)

    pos = lax.broadcasted_iota(jnp.int32, (2 * HEAD_DIM, wb), 1)
    t_new = lax.broadcasted_iota(jnp.int32, (tk, wb), 0)
    place = ((lax.broadcasted_iota(jnp.int32, (tk, wb), 1) == t_new + (wb - ts))
             & (t_new < ts)).astype(BF16)

    def cached_t(c_ref, n_ref, nxt_ref, n, p):
        old = jnp.concatenate([c_ref[n, 2 * p], c_ref[n, 2 * p + 1]], axis=0)
        rows = n_ref[n, :, p * LANES:(p + 1) * LANES].astype(BF16)
        placed = lax.dot_general(rows, place, (((0,), (0,)), ((), ())),
                                 preferred_element_type=F32)
        nxt = jnp.where(pos >= wb - ts, placed, pltpu.roll(old, wb - ts, axis=1))
        nxt_ref[n, 2 * p] = nxt[:HEAD_DIM]
        nxt_ref[n, 2 * p + 1] = nxt[HEAD_DIM:]
        return old.astype(BF16)

    def new_rows(n_ref, n, p):
        lanes = slice(p * LANES, (p + 1) * LANES)
        return jnp.concatenate([n_ref[n, :, lanes], new_pad], axis=0).astype(BF16)

    def score_body(n, carry):
        for p in range(n_pairs):
            pieces = []
            for half in range(2):
                for g in range(group):
                    c0 = (2 * p + half) * hw + (g // 2) * LANES
                    tile = q_ref[n, :, c0:c0 + LANES] * scale
                    if (g % 2) != half:
                        tile = pltpu.roll(tile, HEAD_DIM, axis=1)
                    keep = (lane >= HEAD_DIM * half) & (lane < HEAD_DIM * (half + 1))
                    pieces.append(jnp.where(keep, tile, 0.0))
            lhs = jnp.concatenate(pieces, axis=0).astype(BF16)
            r0 = pl.multiple_of(n * seq_rows + p * pair_rows, pair_rows)
            s_ref[pl.ds(r0, pair_rows), 0:wb] = _dot(lhs, cached_t(ck_ref, kn_ref, nk_ref, n, p))
            s_ref[pl.ds(r0, pair_rows), wb:] = _dot_nt(lhs, new_rows(kn_ref, n, p))
        return carry

    lax.fori_loop(0, sb, score_body, 0)

    t = lax.broadcasted_iota(jnp.int32, (tq, n_keys), 0)
    c = lax.broadcasted_iota(jnp.int32, (tq, n_keys), 1)
    mask8 = ((c < wb) & (t + wb - c <= WINDOW)) | ((c >= wb) & (c - wb <= t) & (c - wb < ts))
    reps = sb * seq_rows // tq
    mask = jnp.concatenate([mask8] * reps, axis=0)
    sink = jnp.concatenate([sink_ref[:, 0:1]] * sb, axis=0)
    s = jnp.where(mask, s_ref[...], NEG_INF)
    m = jnp.maximum(jnp.max(s, axis=-1, keepdims=True), sink)
    pe = jnp.exp(s - m)
    den = jnp.sum(pe, axis=-1, keepdims=True) + jnp.exp(sink - m)
    p_ref[...] = (pe / den).astype(p_ref.dtype)

    def value_body(n, carry):
        for p in range(n_pairs):
            r0 = pl.multiple_of(n * seq_rows + p * pair_rows, pair_rows)
            o = (_dot_nt(p_ref[pl.ds(r0, pair_rows), 0:wb], cached_t(cv_ref, vn_ref, nv_ref, n, p))
                 + _dot(p_ref[pl.ds(r0, pair_rows), wb:], new_rows(vn_ref, n, p)))
            for half in range(2):
                kh = 2 * p + half
                for j in range(group // 2):
                    rows = (half * group + 2 * j) * tq
                    o_lo, o_hi = o[rows:rows + tq], o[rows + tq:rows + 2 * tq]
                    if half == 0:
                        o_hi = pltpu.roll(o_hi, HEAD_DIM, axis=1)
                    else:
                        o_lo = pltpu.roll(o_lo, HEAD_DIM, axis=1)
                    c0 = kh * hw + j * LANES
                    gate = g_ref[n, :, c0:c0 + LANES]
                    o_ref[n, :, c0:c0 + LANES] = jnp.where(lane < HEAD_DIM, o_lo, o_hi) * _silu(gate)
        return carry

    lax.fori_loop(0, sb, value_body, 0)


def _attn_sample(q8, kn, vn, cache_k_t, cache_v_t, g8, sink_rows, n_kv, group, ts):
    nb, tq, aw = q8.shape
    tk, kvw = kn.shape[1], kn.shape[2]
    wb = cache_k_t.shape[3]
    sb = _largest_divisor(nb, (8, 4, 2, 1))
    assert wb % LANES == 0
    n_keys = wb + -(-tk // LANES) * LANES
    rows = sb * (n_kv // 2) * 2 * group * tq
    assert sink_rows.shape[0] * sb == rows
    blk3 = lambda r, width: pl.BlockSpec((sb, r, width), lambda i: (i, 0, 0))
    cache_blk = pl.BlockSpec((sb, n_kv, HEAD_DIM, wb), lambda i: (i, 0, 0, 0))
    return pl.pallas_call(
        functools.partial(_attn_sample_kernel, n_kv=n_kv, group=group, ts=ts),
        out_shape=(jax.ShapeDtypeStruct((nb, tq, aw), F32),
                   jax.ShapeDtypeStruct(cache_k_t.shape, cache_k_t.dtype),
                   jax.ShapeDtypeStruct(cache_v_t.shape, cache_v_t.dtype)),
        grid=(nb // sb,),
        in_specs=[
            blk3(tq, aw), blk3(tk, kvw), blk3(tk, kvw), cache_blk, cache_blk, blk3(tq, aw),
            pl.BlockSpec(sink_rows.shape, lambda i: (0, 0)),
        ],
        out_specs=(blk3(tq, aw), cache_blk, cache_blk),
        scratch_shapes=[pltpu.VMEM((rows, n_keys), F32), pltpu.VMEM((rows, n_keys), BF16)],
        compiler_params=_params(("arbitrary",)),
        name="attn_sample",
    )(q8, kn, vn, cache_k_t, cache_v_t, g8, sink_rows)


def _branch_lru_kernel(x_ref, g_ref, cw_ref, cb_ref, wa_ref, wx_ref, ba_ref, bx_ref, lam_ref,
                       oas_ref, ob_ref, w_ref, ma_ref, mb_ref,
                       o_ref, hp_ref, oa_ref, xt_ref, hc_ref, xc_ref, ga_ref, gx_ref,
                       *, n_p, n_t, chunk):
    u = pl.program_id(0)
    tm, cbw = x_ref.shape

    @pl.when(u == 0)
    def _():
        for ref in (oa_ref, xt_ref, hc_ref, xc_ref, ga_ref, gx_ref):
            ref[...] = jnp.zeros_like(ref)

    unit_t = jnp.maximum(u - 1, 0)
    row_t, ch_t = lax.div(unit_t, n_t), lax.rem(unit_t, n_t)
    live_t = (u >= 1) & (unit_t < n_p * n_t)
    row_d = lax.div(jnp.maximum(u - 1 - n_t, 0), n_t)
    ch_c = lax.rem(u, n_t)
    live_c = u < n_p * n_t

    slot_d = lax.rem(row_d, 2)
    width = n_t * cbw
    tn = o_ref.shape[1]
    n_chunks = tm // chunk
    pieces = [(rh, ch) for rh in range(2) for ch in range(2)]
    chunks_per_piece = n_chunks // len(pieces)
    assert chunks_per_piece * len(pieces) == n_chunks

    def product_piece(rh, ch):
        rows = slice(rh * tm // 2, (rh + 1) * tm // 2)
        cols = slice(ch * tn // 2, (ch + 1) * tn // 2)
        o_a = jnp.concatenate([oa_ref[slot_d, q, rows, :] for q in range(n_t)], axis=1)
        pa = _dot(o_a, w_ref[0:width, cols])
        pb = _dot(ob_ref[rows, :], w_ref[width:, cols])
        ma = ma_ref[rows, cols].astype(F32)
        mb = mb_ref[rows, cols].astype(F32)
        o_ref[rows, cols] = (_sigmoid(ma) * pa + _sigmoid(mb) * pb).astype(o_ref.dtype)

    sub = lax.broadcasted_iota(jnp.int32, (SUBLANES, cbw), 0)
    stage_w = lax.rem(u, 2)
    stage_r = 1 - stage_w

    slot_t = lax.rem(row_t, 2)
    cg = min(cbw, 2 * LANES)
    groups = [slice(k * cg, (k + 1) * cg) for k in range(cbw // cg)]
    sub_g = lax.broadcasted_iota(jnp.int32, (SUBLANES, cg), 0)
    coef = [(ba_ref[:, gs], bx_ref[:, gs], _neg_c_softplus(lam_ref[:, gs])) for gs in groups]
    h_ins = [hc_ref[ch_t, :, gs] for gs in groups]
    for c in range(n_chunks):
        if c % chunks_per_piece == 0:
            product_piece(*pieces[c // chunks_per_piece])
        rows = slice(c * chunk, (c + 1) * chunk)
        for k, gs in enumerate(groups):
            a, b = _lru_coeffs(xc_ref[stage_r, rows, gs], ga_ref[stage_r, rows, gs],
                               gx_ref[stage_r, rows, gs], *coef[k])
            h, h_ins[k] = _lru_prompt_scan(a, b, h_ins[k], sub_g)
            o_a_new = (h * _silu(g_ref[rows, gs].astype(F32))).astype(oa_ref.dtype)
            oa_ref[slot_t, ch_t, rows, gs] = jnp.where(live_t, o_a_new, oas_ref[rows, gs])
    for k, gs in enumerate(groups):
        h_in = jnp.where(live_t, h_ins[k], hc_ref[ch_t, :, gs])
        hc_ref[ch_t, :, gs] = h_in
        hp_ref[ch_t, :, gs] = h_in[0:1, :]

    w, bias = cw_ref[...], cb_ref[...]
    x_tail = xt_ref[ch_c]
    xcs = []
    for c in range(n_chunks):
        rows = slice(c * chunk, (c + 1) * chunk)
        xc, x_tail = _lru_prompt_conv(x_ref[rows, :].astype(F32), x_tail, sub, w, bias)
        xcs.append(xc)
    xt_ref[ch_c] = jnp.where(live_c, x_tail, xt_ref[ch_c])
    xc_all = jnp.concatenate(xcs, axis=0)
    ga, gx = _lru_gate_dots(xc_all, wa_ref, wx_ref)
    xc_ref[stage_w] = xc_all
    ga_ref[stage_w] = ga
    gx_ref[stage_w] = gx


def _branch_lru(z, lru_params, o_a_s, o_b, w_branch_bf16, ma_col, s, tm, tn):
    conv_w, conv_b, w_a, b_a, w_x, b_x, lam = lru_params
    m = z.shape[0]
    width = conv_w.shape[1]
    blk = w_a.shape[1]
    aw = o_b.shape[1]
    d = w_branch_bf16.shape[1]
    n_p = s // tm
    n_r = m // tm
    n_t = d // tn
    cbw = width // n_t
    assert cbw % blk == 0 and cbw % LANES == 0
    chunk = tm // 8
    assert chunk % SUBLANES == 0
    ma_blk = ma_col // tn
    mb_blk = (ma_col + d) // tn
    row_c = lambda u: jnp.minimum(u // n_t, n_p - 1)
    ch_c = lambda u: u % n_t
    unit_t = lambda u: jnp.maximum(u - 1, 0)
    row_t = lambda u: jnp.minimum(unit_t(u) // n_t, n_p - 1)
    ch_t = lambda u: unit_t(u) % n_t
    unit_d = lambda u: jnp.maximum(u - 1 - n_t, 0)
    row_d = lambda u: unit_d(u) // n_t
    tile_d = lambda u: unit_d(u) % n_t
    gate_w = pl.BlockSpec((cbw // blk, blk, blk), lambda u: (ch_c(u), 0, 0))
    chan_t = lambda rows: pl.BlockSpec((rows, cbw), lambda u: (0, ch_t(u)))
    return pl.pallas_call(
        functools.partial(_branch_lru_kernel, n_p=n_p, n_t=n_t, chunk=chunk),
        out_shape=(jax.ShapeDtypeStruct((m, d), BF16), jax.ShapeDtypeStruct((n_t, 1, cbw), F32)),
        grid=((n_r + 1) * n_t + 1,),
        in_specs=[
            pl.BlockSpec((tm, cbw), lambda u: (row_c(u), ch_c(u))),
            pl.BlockSpec((tm, cbw), lambda u: (row_t(u), n_t + ch_t(u))),
            pl.BlockSpec((CONV_WIDTH, cbw), lambda u: (0, ch_c(u))),
            pl.BlockSpec((1, cbw), lambda u: (0, ch_c(u))),
            gate_w,
            gate_w,
            chan_t(1),
            chan_t(1),
            chan_t(1),
            chan_t(tm),
            pl.BlockSpec((tm, aw), lambda u: (row_d(u), 0)),
            pl.BlockSpec((width + aw, tn), lambda u: (0, tile_d(u))),
            pl.BlockSpec((tm, tn), lambda u: (row_d(u), ma_blk + tile_d(u))),
            pl.BlockSpec((tm, tn), lambda u: (row_d(u), mb_blk + tile_d(u))),
        ],
        out_specs=(pl.BlockSpec((tm, tn), lambda u: (row_d(u), tile_d(u))),
                   pl.BlockSpec((n_t, 1, cbw), lambda u: (0, 0, 0))),
        scratch_shapes=[
            pltpu.VMEM((2, n_t, tm, cbw), BF16),
            pltpu.VMEM((n_t, SUBLANES, cbw), F32),
            pltpu.VMEM((n_t, SUBLANES, cbw), F32),
            pltpu.VMEM((2, tm, cbw), F32),
            pltpu.VMEM((2, tm, cbw), F32),
            pltpu.VMEM((2, tm, cbw), F32),
        ],
        compiler_params=_params(("arbitrary",)),
        name="branch_lru",
    )(z, z, conv_w, conv_b, w_a, w_x, b_a, b_x, lam, o_a_s, o_b, w_branch_bf16, z, z)


def _out_kernel(m_ref, w_ref, g_ref, x_ref, y_ref, acc_ref, ss_ref, *, d):
    i = pl.program_id(0)
    j = pl.program_id(1)
    slot = lax.rem(i, 2)
    prev = 1 - slot

    @pl.when((i == 0) & (j == 0))
    def _():
        acc_ref[1] = jnp.zeros(acc_ref.shape[1:], F32)
        ss_ref[...] = jnp.zeros_like(ss_ref)

    t = _dot(m_ref[...], w_ref[...])
    acc_ref[slot, j] = t
    ssq = jnp.sum(t * t, axis=-1, keepdims=True)
    ss_ref[slot] = jnp.where(j == 0, ssq, ss_ref[slot] + ssq)

    inv = lax.rsqrt(ss_ref[prev] / d + RMS_EPS)
    y_ref[...] = x_ref[...] + (acc_ref[prev, j] * inv) * g_ref[...]


def _out_proj(merged, w_out_bf16, g, x, row0, tm, tn):
    rows, d = x.shape
    n_r = rows // tm
    n_t = d // tn
    blk0 = row0 // tm
    xy_map = lambda i, j: (jnp.maximum(i - 1, 0), jnp.where(i == 0, 0, j))
    return pl.pallas_call(
        functools.partial(_out_kernel, d=d),
        out_shape=jax.ShapeDtypeStruct((rows, d), F32),
        grid=(n_r + 1, n_t),
        in_specs=[
            pl.BlockSpec((tm, d), lambda i, j: (blk0 + jnp.minimum(i, n_r - 1), 0)),
            pl.BlockSpec((d, tn), lambda i, j: (0, jnp.where(i == n_r, n_t - 1, j))),
            pl.BlockSpec((1, tn), lambda i, j: (0, j)),
            pl.BlockSpec((tm, tn), xy_map),
        ],
        out_specs=pl.BlockSpec((tm, tn), xy_map),
        scratch_shapes=[pltpu.VMEM((2, n_t, tm, tn), F32), pltpu.VMEM((2, tm, 1), F32)],
        compiler_params=_params(("arbitrary", "arbitrary")),
        name="out_proj",
    )(merged, w_out_bf16, g, x)


def _pad_rows(x, rows):
    return jnp.pad(x, ((0, 0), (0, rows - x.shape[1]), (0, 0)))


def _layer(xp, xs_tm, conv_state, h_state, cache_k, cache_v, norm_pre, norm_post, w_in, conv_w,
           conv_b, w_a, b_a, w_x, b_x, lam, sinks, w_branch, w_out):
    s, d = xp.shape
    nb, wb, n_kv, _ = cache_k.shape
    ts = xs_tm.shape[0] // nb
    width = conv_w.shape[1]
    n_heads = sinks.shape[0]
    group = n_heads // n_kv
    aw, kvw = n_heads * HEAD_DIM, n_kv * HEAD_DIM
    m = s + nb * ts
    tm = nb * ts
    assert s % tm == 0 and tm % SUBLANES == 0 and ts >= CONV_WIDTH - 1
    q_col = 2 * width
    k_col = q_col + aw
    v_col = k_col + kvw
    g_col = v_col + kvw
    ma_col = g_col + aw
    assert w_in.shape[1] == ma_col + 2 * d

    xn = _rmsnorm_pre(xp, xs_tm, norm_pre.reshape(1, d), _largest_divisor(tm, (512, 256, 128, 64, 8)))
    z, (w_branch_bf16, w_out_bf16) = _in_proj(xn, w_in, (w_branch, w_out))

    tn = _largest_divisor(ma_col, (512, 256, 128))
    assert d % tn == 0 and width % (d // tn) == 0
    lru_params = (conv_w, conv_b.reshape(1, width), w_a.astype(BF16), b_a.reshape(1, width),
                  w_x.astype(BF16), b_x.reshape(1, width), lam.reshape(1, width))
    o_a_s, h_s = _lru_sample(z, lru_params, conv_state.transpose(1, 0, 2), h_state, s, tm,
                             width // (d // tn))

    def sample_cols(c0, c1):
        blk = lax.slice(z, (s, c0), (m, c1)).reshape(ts, nb, c1 - c0)
        return blk.transpose(1, 0, 2).astype(F32)

    q_s, k_s, v_s = sample_cols(q_col, k_col), sample_cols(k_col, v_col), sample_cols(v_col, g_col)
    g_s = sample_cols(g_col, ma_col)
    tq = -(-ts // SUBLANES) * SUBLANES
    tk = -(-ts // BF16_ROWS) * BF16_ROWS
    sink_rows = jnp.broadcast_to(sinks.reshape(n_heads, 1, 1), (n_heads, tq, LANES))
    sink_rows = sink_rows.reshape(n_heads * tq, LANES)
    heads_t = lambda c: c.transpose(0, 2, 3, 1)
    assert ts <= wb
    o_b_s, new_k_t, new_v_t = _attn_sample(
        _pad_rows(q_s, tq), _pad_rows(k_s, tk), _pad_rows(v_s, tk), heads_t(cache_k),
        heads_t(cache_v), _pad_rows(g_s, tq), sink_rows, n_kv, group, ts)
    new_k_s, new_v_s = new_k_t.transpose(0, 3, 1, 2), new_v_t.transpose(0, 3, 1, 2)
    o_b_s = o_b_s[:, :ts].transpose(1, 0, 2).reshape(nb * ts, aw).astype(BF16)
    o_b = _attn_prompt(z, sinks, o_b_s, s, width, aw, kvw, n_kv, group)

    merged, h_p = _branch_lru(z, lru_params, o_a_s, o_b, w_branch_bf16, ma_col, s, tm, tn)
    h_p = h_p.reshape(1, width)
    g_post = norm_post.reshape(1, d)
    tn_out = _largest_divisor(d, (1024, 512, 256, 128))
    y_p = _out_proj(merged, w_out_bf16, g_post, xp, 0, tm, tn_out)
    y_s = _out_proj(merged, w_out_bf16, g_post, xs_tm, s, tm // 2, tn_out)

    keep = CONV_WIDTH - 1
    new_conv_p = lax.slice(z, (s - keep, 0), (s, width)).astype(F32)[None]
    wbp = min(WINDOW, s)
    new_k_p = lax.slice(z, (s - wbp, k_col), (s, v_col)).astype(F32).reshape(1, wbp, n_kv, HEAD_DIM)
    new_v_p = lax.slice(z, (s - wbp, v_col), (s, g_col)).astype(F32).reshape(1, wbp, n_kv, HEAD_DIM)
    x_lru_s = sample_cols(0, width)
    new_conv_s = jnp.concatenate([conv_state, x_lru_s], axis=1)[:, -keep:]
    return y_p, y_s, (new_conv_p, h_p, new_k_p, new_v_p), (new_conv_s, h_s, new_k_s, new_v_s)


def kernel(x_prompt, x_sample, state_conv, state_h, cache_k_win, cache_v_win, norm_pre, norm_post, w_in, conv_w, conv_b, lru_w_a, lru_b_a, lru_w_x, lru_b_x, lru_lambda, attn_sinks, w_branch, w_out):
    batch, s, d = x_prompt.shape
    nb, ts, _ = x_sample.shape
    assert batch == 1, "the prompt group is a single sequence"
    depth = w_in.shape[0]
    xp = x_prompt.reshape(s, d)
    xs_tm = x_sample.transpose(1, 0, 2).reshape(ts * nb, d)
    p_states, s_states = [], []
    for l in range(depth):
        xp, xs_tm, p_new, s_new = _layer(
            xp, xs_tm, state_conv[l], state_h[l], cache_k_win[l], cache_v_win[l], norm_pre[l],
            norm_post[l], w_in[l], conv_w[l], conv_b[l], lru_w_a[l], lru_b_a[l], lru_w_x[l],
            lru_b_x[l], lru_lambda[l], attn_sinks[l], w_branch[l], w_out[l])
        p_states.append(p_new)
        s_states.append(s_new)
    y_prompt = xp.reshape(1, s, d)
    y_sample = xs_tm.reshape(ts, nb, d).transpose(1, 0, 2)
    stack = lambda states, k: jnp.stack([st[k] for st in states])
    return (y_prompt, y_sample,
            stack(p_states, 0), stack(p_states, 1), stack(p_states, 2), stack(p_states, 3),
            stack(s_states, 0), stack(s_states, 1), stack(s_states, 2), stack(s_states, 3))
```

```python
import functools

import jax
import jax.numpy as jnp
from jax import lax
from jax.experimental import pallas as pl
from jax.experimental.pallas import tpu as pltpu

HEAD_DIM = 64
WINDOW = 128
CONV_WIDTH = 4
LRU_C = 8.0
RMS_EPS = 1e-6
NEG_INF = -1e30

LANES = 128
SUBLANES = 8
BF16_ROWS = 16
VMEM_LIMIT_BYTES = 58 * 1024 * 1024
IN_PROJ_VMEM_BUDGET = 52 * 1024 * 1024
IN_PROJ_VMEM_LIMIT_BYTES = 63 * 1024 * 1024

F32 = jnp.float32
BF16 = jnp.bfloat16


def _largest_divisor(n, candidates):
    for c in candidates:
        if n % c == 0:
            return c
    raise ValueError(f"no tile in {candidates} divides {n}")


def _params(sem, vmem_limit_bytes=VMEM_LIMIT_BYTES):
    return pltpu.CompilerParams(dimension_semantics=sem, vmem_limit_bytes=vmem_limit_bytes)


def _dot(a, b):
    return jnp.dot(a, b, preferred_element_type=F32)


def _dot_nt(a, b):
    return lax.dot_general(a, b, (((1,), (1,)), ((), ())), preferred_element_type=F32)


def _sigmoid(x):
    return jax.nn.sigmoid(x)


def _silu(x):
    return x * jax.nn.sigmoid(x)


def _rmsnorm_kernel(xp_ref, xs_ref, g_ref, o_ref, *, n_p):
    i = pl.program_id(0)

    def body(x_ref):
        x = x_ref[...]
        ms = jnp.mean(x * x, axis=-1, keepdims=True)
        o_ref[...] = ((x * lax.rsqrt(ms + RMS_EPS)) * g_ref[...]).astype(o_ref.dtype)

    @pl.when(i < n_p)
    def _():
        body(xp_ref)

    @pl.when(i >= n_p)
    def _():
        body(xs_ref)


def _rmsnorm_pre(xp, xs, g, tm):
    s, d = xp.shape
    ms = xs.shape[0]
    n_p, n_s = s // tm, ms // tm
    return pl.pallas_call(
        functools.partial(_rmsnorm_kernel, n_p=n_p),
        out_shape=jax.ShapeDtypeStruct((s + ms, d), BF16),
        grid=(n_p + n_s,),
        in_specs=[
            pl.BlockSpec((tm, d), lambda i: (jnp.minimum(i, n_p - 1), 0)),
            pl.BlockSpec((tm, d), lambda i: (jnp.maximum(i - n_p, 0), 0)),
            pl.BlockSpec((1, d), lambda i: (0, 0)),
        ],
        out_specs=pl.BlockSpec((tm, d), lambda i: (i, 0)),
        compiler_params=_params(("arbitrary",)),
        name="rmsnorm_pre",
    )(xp, xs, g)


def _cast_rows(src_ref, dst_ref, rows_per_step=256):
    n = src_ref.shape[0]
    step = _largest_divisor(n, (rows_per_step, 128, 64, 32, 16))

    def body(r, carry):
        r0 = pl.multiple_of(r * step, step)
        dst_ref[pl.ds(r0, step), :] = src_ref[pl.ds(r0, step), :].astype(dst_ref.dtype)
        return carry

    lax.fori_loop(0, n // step, body, 0)


def _in_proj_kernel(x_ref, w_ref, *rest):
    n_extra = (len(rest) - 2) // 2
    extra_in, o_ref = rest[:n_extra], rest[n_extra]
    extra_out, wb_ref = rest[n_extra + 1:2 * n_extra + 1], rest[-1]

    @pl.when(pl.program_id(1) == 0)
    def _():
        _cast_rows(w_ref, wb_ref)

    o_ref[...] = _dot(x_ref[...], wb_ref[...]).astype(o_ref.dtype)
    for src, dst in zip(extra_in, extra_out):
        dst[...] = src[...].astype(dst.dtype)


def _in_proj_tiles(m, d, n):
    best = None
    for tm in range(BF16_ROWS, m + 1, BF16_ROWS):
        if m % tm:
            continue
        for tn in range(2 * LANES, n + 1, 2 * LANES):
            if n % tn:
                continue
            vmem = d * tn * (2 * 4 + 2) + 2 * tm * d * 2 + 2 * tm * tn * 2
            if vmem <= IN_PROJ_VMEM_BUDGET and (best is None or tm * tn > best[0] * best[1]):
                best = (tm, tn)
    assert best is not None
    return best


def _in_proj(xn, w_in, extra_weights=()):
    m, d = xn.shape
    n = w_in.shape[1]
    tm, tn = _in_proj_tiles(m, d, n)
    n_i = m // tm
    n_steps = (n // tn) * n_i
    extra_specs = []
    for w in extra_weights:
        rows = w.shape[0]
        slab = next(r for r in range(BF16_ROWS, rows + 1, BF16_ROWS)
                    if rows % r == 0 and rows // r <= n_steps)
        last = rows // slab - 1
        extra_specs.append(pl.BlockSpec(
            (slab, w.shape[1]), lambda j, i, last=last: (jnp.minimum(j * n_i + i, last), 0)))
    out = pl.pallas_call(
        _in_proj_kernel,
        out_shape=(jax.ShapeDtypeStruct((m, n), BF16),
                   *[jax.ShapeDtypeStruct(w.shape, BF16) for w in extra_weights]),
        grid=(n // tn, n_i),
        in_specs=[
            pl.BlockSpec((tm, d), lambda j, i: (i, 0)),
            pl.BlockSpec((d, tn), lambda j, i: (0, j)),
            *extra_specs,
        ],
        out_specs=(pl.BlockSpec((tm, tn), lambda j, i: (i, j)), *extra_specs),
        scratch_shapes=[pltpu.VMEM((d, tn), BF16)],
        compiler_params=_params(("arbitrary", "arbitrary"), IN_PROJ_VMEM_LIMIT_BYTES),
        name="in_proj",
    )(xn, w_in, *extra_weights)
    return out[0], out[1:]


LOG2_E = 1.4426950408889634


def _neg_c_softplus(lam):
    return (-LRU_C * LOG2_E) * (jnp.maximum(-lam, 0.0) + jnp.log(1.0 + jnp.exp(-jnp.abs(lam))))


def _lru_conv(taps, w, bias):
    out = bias
    for k in range(CONV_WIDTH):
        out = out + taps[k] * w[k:k + 1, :]
    return out


def _lru_gate_dots(xc, wa_ref, wx_ref):
    blk = wa_ref.shape[1]
    xcb = xc.astype(BF16)
    ga, gx = [], []
    for q in range(xc.shape[1] // blk):
        xq = xcb[:, q * blk:(q + 1) * blk]
        ga.append(_dot(xq, wa_ref[q].astype(BF16)))
        gx.append(_dot(xq, wx_ref[q].astype(BF16)))
    return jnp.concatenate(ga, axis=1), jnp.concatenate(gx, axis=1)


def _lru_coeffs(xc, ga, gx, b_a, b_x, ncs):
    r_a = _sigmoid(ga + b_a)
    r_x = _sigmoid(gx + b_x)
    a = jnp.exp2(ncs * r_a)
    b = jnp.exp2(0.5 * jnp.log2(1.0 - a * a)) * (r_x * xc)
    return a, b


def _scan_tile(a, b, sub):
    shift = 1
    while shift < SUBLANES:
        keep = sub >= shift
        a_sh = jnp.where(keep, pltpu.roll(a, shift, axis=0), 1.0)
        b_sh = jnp.where(keep, pltpu.roll(b, shift, axis=0), 0.0)
        b = b + a * b_sh
        a = a * a_sh
        shift *= 2
    return a, b


def _lru_prompt_conv(xcur, x_tail, sub, w, bias):
    n_tiles = xcur.shape[0] // SUBLANES
    tiles = [x_tail] + [xcur[v * SUBLANES:(v + 1) * SUBLANES] for v in range(n_tiles)]
    taps = []
    for k in range(CONV_WIDTH):
        back = CONV_WIDTH - 1 - k
        if back == 0:
            taps.append(xcur)
            continue
        rolled = [pltpu.roll(t, back, axis=0) for t in tiles]
        taps.append(jnp.concatenate(
            [jnp.where(sub >= back, rolled[v + 1], rolled[v]) for v in range(n_tiles)], axis=0))
    return _lru_conv(taps, w, bias), tiles[-1]


def _lru_prompt_scan(a, b, h_in, sub):
    hs = []
    for v in range(a.shape[0] // SUBLANES):
        rows = slice(v * SUBLANES, (v + 1) * SUBLANES)
        a_in, b_in = _scan_tile(a[rows], b[rows], sub)
        hv = a_in * h_in + b_in
        hs.append(hv)
        h_in = jnp.broadcast_to(hv[SUBLANES - 1:SUBLANES, :], hv.shape)
    return jnp.concatenate(hs, axis=0), h_in


def _lru_sample_kernel(x_ref, g_ref, cw_ref, cb_ref, wa_ref, ba_ref, wx_ref, bx_ref, lam_ref,
                       cst_ref, h0_ref, o_ref, hs_ref):
    nb = h0_ref.shape[0]
    ts = x_ref.shape[0] // nb
    w, bias = cw_ref[...], cb_ref[...]
    ncs = _neg_c_softplus(lam_ref[...])
    xs = [cst_ref[k] for k in range(CONV_WIDTH - 1)]
    xs += [x_ref[t * nb:(t + 1) * nb, :].astype(F32) for t in range(ts)]
    h = h0_ref[...]
    for t in range(ts):
        xc = _lru_conv(xs[t:t + CONV_WIDTH], w, bias)
        ga, gx = _lru_gate_dots(xc, wa_ref, wx_ref)
        a, b = _lru_coeffs(xc, ga, gx, ba_ref[...], bx_ref[...], ncs)
        h = a * h + b
        g = g_ref[t * nb:(t + 1) * nb, :].astype(F32)
        o_ref[t * nb:(t + 1) * nb, :] = (h * _silu(g)).astype(o_ref.dtype)
    hs_ref[...] = h


def _lru_sample(z, lru_params, cst_tm, h0, s, tb, cbw):
    conv_w, conv_b, w_a, b_a, w_x, b_x, lam = lru_params
    width = conv_w.shape[1]
    nb = h0.shape[0]
    blk = w_a.shape[1]
    n_c = width // cbw
    blk_s = s // tb
    row = lambda c: (0, c)
    gate_w = pl.BlockSpec((cbw // blk, blk, blk), lambda c: (c, 0, 0))
    return pl.pallas_call(
        _lru_sample_kernel,
        out_shape=(jax.ShapeDtypeStruct((tb, width), BF16), jax.ShapeDtypeStruct((nb, width), F32)),
        grid=(n_c,),
        in_specs=[
            pl.BlockSpec((tb, cbw), lambda c: (blk_s, c)),
            pl.BlockSpec((tb, cbw), lambda c: (blk_s, n_c + c)),
            pl.BlockSpec((CONV_WIDTH, cbw), row),
            pl.BlockSpec((1, cbw), row),
            gate_w,
            pl.BlockSpec((1, cbw), row),
            gate_w,
            pl.BlockSpec((1, cbw), row),
            pl.BlockSpec((1, cbw), row),
            pl.BlockSpec((CONV_WIDTH - 1, nb, cbw), lambda c: (0, 0, c)),
            pl.BlockSpec((nb, cbw), row),
        ],
        out_specs=(pl.BlockSpec((tb, cbw), row), pl.BlockSpec((nb, cbw), row)),
        compiler_params=_params(("arbitrary",)),
        name="lru_sample",
    )(z, z, conv_w, conv_b, w_a, b_a, w_x, b_x, lam, cst_tm, h0)


def _attn_prompt_kernel(sink_ref, q_ref, kp_ref, kc_ref, vp_ref, vc_ref, g0_ref, g1_ref, os_ref,
                        o_ref, *, n_q, n_kv, group):
    i = pl.program_id(0)

    @pl.when(i < n_q)
    def _():
        _attn_prompt_block(i, sink_ref, q_ref, kp_ref, kc_ref, vp_ref, vc_ref, g0_ref, g1_ref,
                           o_ref, n_kv, group)

    @pl.when(i >= n_q)
    def _():
        o_ref[...] = os_ref[...]


def _attn_prompt_block(i, sink_ref, q_ref, kp_ref, kc_ref, vp_ref, vc_ref, g0_ref, g1_ref, o_ref,
                       n_kv, group):
    qb = q_ref.shape[0]
    aw_half = g0_ref.shape[1]
    scale = HEAD_DIM ** -0.5
    hw = group * HEAD_DIM
    c = lax.broadcasted_iota(jnp.int32, (2 * qb, qb), 0)
    r = lax.broadcasted_iota(jnp.int32, (2 * qb, qb), 1)
    rel = qb + r - c
    mask = (rel >= 0) & (rel <= WINDOW) & ((c >= qb) | (i > 0))
    lane = lax.broadcasted_iota(jnp.int32, (qb, LANES), 1)
    half_mask = (lane < HEAD_DIM, lane >= HEAD_DIM)
    zero = jnp.zeros((qb, LANES), BF16)
    n_pairs = n_kv // 2
    kv = {}
    out_t = {}

    def heads_of(p, variant):
        return [(2 * p + half, g) for half in range(2) for g in range(group)
                if ((g % 2) == half) == (variant == 0)]

    def load_pair(p):
        lanes = slice(p * LANES, (p + 1) * LANES)
        k_f32 = jnp.concatenate([kp_ref[:, lanes], kc_ref[:, lanes]], axis=0).astype(F32) * scale
        v_pair = jnp.concatenate([vp_ref[:, lanes], vc_ref[:, lanes]], axis=0)
        kv[p] = (k_f32.astype(BF16), pltpu.roll(k_f32, HEAD_DIM, axis=1).astype(BF16),
                 v_pair.astype(F32).T.astype(BF16))

    def scores(p, variant):
        q_rows = []
        for kh, g in heads_of(p, variant):
            c0 = kh * hw + (g // 2) * LANES
            q_rows.append(jnp.where(half_mask[g % 2], q_ref[:, c0:c0 + LANES], zero))
        return _dot_nt(kv[p][variant], jnp.concatenate(q_rows, axis=0))

    def softmax_values(p, variant, s_t):
        heads = heads_of(p, variant)
        p_blocks, inv_den = [], []
        for b, (kh, g) in enumerate(heads):
            sink = sink_ref[kh * group + g]
            sb = jnp.where(mask, s_t[:, b * qb:(b + 1) * qb], NEG_INF)
            m = jnp.maximum(jnp.max(sb, axis=0, keepdims=True), sink)
            pe = jnp.exp(sb - m)
            den = jnp.sum(pe, axis=0, keepdims=True) + jnp.exp(sink - m)
            p_blocks.append(pe.astype(BF16))
            inv_den.append(1.0 / den)
        o_t = _dot(kv[p][2], jnp.concatenate(p_blocks, axis=1))
        for b, (kh, g) in enumerate(heads):
            r0 = (kh % 2) * HEAD_DIM
            out_t[(kh, g)] = o_t[r0:r0 + HEAD_DIM, b * qb:(b + 1) * qb] * inv_den[b]

    def store_pair(p):
        for half in range(2):
            kh = 2 * p + half
            for j in range(group // 2):
                c0 = kh * hw + j * LANES
                tile = jnp.concatenate([out_t[(kh, 2 * j)], out_t[(kh, 2 * j + 1)]], axis=0).T
                g_ref, gc = (g0_ref, c0) if c0 < aw_half else (g1_ref, c0 - aw_half)
                gate = g_ref[:, gc:gc + LANES].astype(F32)
                o_ref[:, c0:c0 + LANES] = (tile * _silu(gate)).astype(o_ref.dtype)

    groups = [(p, variant) for p in range(n_pairs) for variant in range(2)]
    load_pair(0)
    s_next = scores(*groups[0])
    for k, (p, variant) in enumerate(groups):
        s_cur = s_next
        if k + 1 < len(groups):
            if groups[k + 1][0] != p:
                load_pair(groups[k + 1][0])
            s_next = scores(*groups[k + 1])
        softmax_values(p, variant, s_cur)
        if variant == 1:
            store_pair(p)


def _attn_prompt(z, sinks, o_b_s, s, width, aw, kvw, n_kv, group):
    qb = WINDOW
    n_q = s // qb
    assert s % qb == 0 and o_b_s.shape[0] % qb == 0 and n_kv % 2 == 0 and group % 2 == 0
    q_blk = (2 * width) // aw
    k_blk = (2 * width + aw) // kvw
    v_blk = k_blk + 1
    g_blk = (2 * width + aw + 2 * kvw) // (aw // 2)
    assert (2 * width) % aw == 0 and (2 * width + aw) % kvw == 0
    assert (2 * width + aw + 2 * kvw) % (aw // 2) == 0
    cur = lambda i: jnp.minimum(i, n_q - 1)
    prev = lambda i: jnp.maximum(cur(i) - 1, 0)
    return pl.pallas_call(
        functools.partial(_attn_prompt_kernel, n_q=n_q, n_kv=n_kv, group=group),
        out_shape=jax.ShapeDtypeStruct((s + o_b_s.shape[0], aw), BF16),
        grid=(n_q + o_b_s.shape[0] // qb,),
        in_specs=[
            pl.BlockSpec(memory_space=pltpu.SMEM),
            pl.BlockSpec((qb, aw), lambda i: (cur(i), q_blk)),
            pl.BlockSpec((qb, kvw), lambda i: (prev(i), k_blk)),
            pl.BlockSpec((qb, kvw), lambda i: (cur(i), k_blk)),
            pl.BlockSpec((qb, kvw), lambda i: (prev(i), v_blk)),
            pl.BlockSpec((qb, kvw), lambda i: (cur(i), v_blk)),
            pl.BlockSpec((qb, aw // 2), lambda i: (cur(i), g_blk)),
            pl.BlockSpec((qb, aw // 2), lambda i: (cur(i), g_blk + 1)),
            pl.BlockSpec((qb, aw), lambda i: (jnp.maximum(i - n_q, 0), 0)),
        ],
        out_specs=pl.BlockSpec((qb, aw), lambda i: (i, 0)),
        compiler_params=_params(("arbitrary",)),
        name="attn_prompt",
    )(sinks, z, z, z, z, z, z, z, o_b_s)


def _attn_sample_kernel(q_ref, kn_ref, vn_ref, ck_ref, cv_ref, g_ref, sink_ref, o_ref,
                        nk_ref, nv_ref, s_ref, p_ref, *, n_kv, group, ts):
    sb, tq, _ = q_ref.shape
    wb = ck_ref.shape[3]
    tk = kn_ref.shape[1]
    n_keys = s_ref.shape[1]
    n_new = n_keys - wb
    n_pairs = n_kv // 2
    pair_rows = 2 * group * tq
    seq_rows = n_pairs * pair_rows
    scale = HEAD_DIM ** -0.5
    hw = group * HEAD_DIM
    lane = lax.broadcasted_iota(jnp.int32, (tq, LANES), 1)
    new_pad = jnp.zeros((n_new - tk, LANES), F32)

    pos = lax.broadcasted_iota(jnp.int32, (2 * HEAD_DIM, wb), 1)
    t_new = lax.broadcasted_iota(jnp.int32, (tk, wb), 0)
    place = ((lax.broadcasted_iota(jnp.int32, (tk, wb), 1) == t_new + (wb - ts))
             & (t_new < ts)).astype(BF16)

    def cached_t(c_ref, n_ref, nxt_ref, n, p):
        old = jnp.concatenate([c_ref[n, 2 * p], c_ref[n, 2 * p + 1]], axis=0)
        rows = n_ref[n, :, p * LANES:(p + 1) * LANES].astype(BF16)
        placed = lax.dot_general(rows, place, (((0,), (0,)), ((), ())),
                                 preferred_element_type=F32)
        nxt = jnp.where(pos >= wb - ts, placed, pltpu.roll(old, wb - ts, axis=1))
        nxt_ref[n, 2 * p] = nxt[:HEAD_DIM]
        nxt_ref[n, 2 * p + 1] = nxt[HEAD_DIM:]
        return old.astype(BF16)

    def new_rows(n_ref, n, p):
        lanes = slice(p * LANES, (p + 1) * LANES)
        return jnp.concatenate([n_ref[n, :, lanes], new_pad], axis=0).astype(BF16)

    def score_body(n, carry):
        for p in range(n_pairs):
            pieces = []
            for half in range(2):
                for g in range(group):
                    c0 = (2 * p + half) * hw + (g // 2) * LANES
                    tile = q_ref[n, :, c0:c0 + LANES] * scale
                    if (g % 2) != half:
                        tile = pltpu.roll(tile, HEAD_DIM, axis=1)
                    keep = (lane >= HEAD_DIM * half) & (lane < HEAD_DIM * (half + 1))
                    pieces.append(jnp.where(keep, tile, 0.0))
            lhs = jnp.concatenate(pieces, axis=0).astype(BF16)
            r0 = pl.multiple_of(n * seq_rows + p * pair_rows, pair_rows)
            s_ref[pl.ds(r0, pair_rows), 0:wb] = _dot(lhs, cached_t(ck_ref, kn_ref, nk_ref, n, p))
            s_ref[pl.ds(r0, pair_rows), wb:] = _dot_nt(lhs, new_rows(kn_ref, n, p))
        return carry

    lax.fori_loop(0, sb, score_body, 0)

    t = lax.broadcasted_iota(jnp.int32, (tq, n_keys), 0)
    c = lax.broadcasted_iota(jnp.int32, (tq, n_keys), 1)
    mask8 = ((c < wb) & (t + wb - c <= WINDOW)) | ((c >= wb) & (c - wb <= t) & (c - wb < ts))
    reps = sb * seq_rows // tq
    mask = jnp.concatenate([mask8] * reps, axis=0)
    sink = jnp.concatenate([sink_ref[:, 0:1]] * sb, axis=0)
    s = jnp.where(mask, s_ref[...], NEG_INF)
    m = jnp.maximum(jnp.max(s, axis=-1, keepdims=True), sink)
    pe = jnp.exp(s - m)
    den = jnp.sum(pe, axis=-1, keepdims=True) + jnp.exp(sink - m)
    p_ref[...] = (pe / den).astype(p_ref.dtype)

    def value_body(n, carry):
        for p in range(n_pairs):
            r0 = pl.multiple_of(n * seq_rows + p * pair_rows, pair_rows)
            o = (_dot_nt(p_ref[pl.ds(r0, pair_rows), 0:wb], cached_t(cv_ref, vn_ref, nv_ref, n, p))
                 + _dot(p_ref[pl.ds(r0, pair_rows), wb:], new_rows(vn_ref, n, p)))
            for half in range(2):
                kh = 2 * p + half
                for j in range(group // 2):
                    rows = (half * group + 2 * j) * tq
                    o_lo, o_hi = o[rows:rows + tq], o[rows + tq:rows + 2 * tq]
                    if half == 0:
                        o_hi = pltpu.roll(o_hi, HEAD_DIM, axis=1)
                    else:
                        o_lo = pltpu.roll(o_lo, HEAD_DIM, axis=1)
                    c0 = kh * hw + j * LANES
                    gate = g_ref[n, :, c0:c0 + LANES]
                    o_ref[n, :, c0:c0 + LANES] = jnp.where(lane < HEAD_DIM, o_lo, o_hi) * _silu(gate)
        return carry

    lax.fori_loop(0, sb, value_body, 0)


def _attn_sample(q8, kn, vn, cache_k_t, cache_v_t, g8, sink_rows, n_kv, group, ts):
    nb, tq, aw = q8.shape
    tk, kvw = kn.shape[1], kn.shape[2]
    wb = cache_k_t.shape[3]
    sb = _largest_divisor(nb, (16, 8, 4, 2, 1))
    assert wb % LANES == 0
    n_keys = wb + -(-tk // LANES) * LANES
    rows = sb * (n_kv // 2) * 2 * group * tq
    assert sink_rows.shape[0] * sb == rows
    blk3 = lambda r, width: pl.BlockSpec((sb, r, width), lambda i: (i, 0, 0))
    cache_blk = pl.BlockSpec((sb, n_kv, HEAD_DIM, wb), lambda i: (i, 0, 0, 0))
    return pl.pallas_call(
        functools.partial(_attn_sample_kernel, n_kv=n_kv, group=group, ts=ts),
        out_shape=(jax.ShapeDtypeStruct((nb, tq, aw), F32),
                   jax.ShapeDtypeStruct(cache_k_t.shape, cache_k_t.dtype),
                   jax.ShapeDtypeStruct(cache_v_t.shape, cache_v_t.dtype)),
        grid=(nb // sb,),
        in_specs=[
            blk3(tq, aw), blk3(tk, kvw), blk3(tk, kvw), cache_blk, cache_blk, blk3(tq, aw),
            pl.BlockSpec(sink_rows.shape, lambda i: (0, 0)),
        ],
        out_specs=(blk3(tq, aw), cache_blk, cache_blk),
        scratch_shapes=[pltpu.VMEM((rows, n_keys), F32), pltpu.VMEM((rows, n_keys), BF16)],
        compiler_params=_params(("arbitrary",)),
        name="attn_sample",
    )(q8, kn, vn, cache_k_t, cache_v_t, g8, sink_rows)


def _branch_lru_kernel(x_ref, g_ref, cw_ref, cb_ref, wa_ref, wx_ref, ba_ref, bx_ref, lam_ref,
                       oas_ref, ob_ref, w_ref, ma_ref, mb_ref,
                       o_ref, hp_ref, oa_ref, xt_ref, hc_ref, xc_ref, ga_ref, gx_ref,
                       *, n_p, n_t, chunk):
    u = pl.program_id(0)
    tm, cbw = x_ref.shape

    @pl.when(u == 0)
    def _():
        for ref in (oa_ref, xt_ref, hc_ref, xc_ref, ga_ref, gx_ref):
            ref[...] = jnp.zeros_like(ref)

    unit_t = jnp.maximum(u - 1, 0)
    row_t, ch_t = lax.div(unit_t, n_t), lax.rem(unit_t, n_t)
    live_t = (u >= 1) & (unit_t < n_p * n_t)
    row_d = lax.div(jnp.maximum(u - 1 - n_t, 0), n_t)
    ch_c = lax.rem(u, n_t)
    live_c = u < n_p * n_t

    slot_d = lax.rem(row_d, 2)
    width = n_t * cbw
    tn = o_ref.shape[1]
    n_chunks = tm // chunk
    pieces = [(rh, ch) for rh in range(2) for ch in range(2)]
    chunks_per_piece = n_chunks // len(pieces)
    assert chunks_per_piece * len(pieces) == n_chunks

    def product_piece(rh, ch):
        rows = slice(rh * tm // 2, (rh + 1) * tm // 2)
        cols = slice(ch * tn // 2, (ch + 1) * tn // 2)
        o_a = jnp.concatenate([oa_ref[slot_d, q, rows, :] for q in range(n_t)], axis=1)
        pa = _dot(o_a, w_ref[0:width, cols])
        pb = _dot(ob_ref[rows, :], w_ref[width:, cols])
        ma = ma_ref[rows, cols].astype(F32)
        mb = mb_ref[rows, cols].astype(F32)
        o_ref[rows, cols] = (_sigmoid(ma) * pa + _sigmoid(mb) * pb).astype(o_ref.dtype)

    sub = lax.broadcasted_iota(jnp.int32, (SUBLANES, cbw), 0)
    stage_w = lax.rem(u, 2)
    stage_r = 1 - stage_w

    slot_t = lax.rem(row_t, 2)
    cg = min(cbw, 2 * LANES)
    groups = [slice(k * cg, (k + 1) * cg) for k in range(cbw // cg)]
    sub_g = lax.broadcasted_iota(jnp.int32, (SUBLANES, cg), 0)
    coef = [(ba_ref[:, gs], bx_ref[:, gs], _neg_c_softplus(lam_ref[:, gs])) for gs in groups]
    h_ins = [hc_ref[ch_t, :, gs] for gs in groups]
    for c in range(n_chunks):
        if c % chunks_per_piece == 0:
            product_piece(*pieces[c // chunks_per_piece])
        rows = slice(c * chunk, (c + 1) * chunk)
        for k, gs in enumerate(groups):
            a, b = _lru_coeffs(xc_ref[stage_r, rows, gs], ga_ref[stage_r, rows, gs],
                               gx_ref[stage_r, rows, gs], *coef[k])
            h, h_ins[k] = _lru_prompt_scan(a, b, h_ins[k], sub_g)
            o_a_new = (h * _silu(g_ref[rows, gs].astype(F32))).astype(oa_ref.dtype)
            oa_ref[slot_t, ch_t, rows, gs] = jnp.where(live_t, o_a_new, oas_ref[rows, gs])
    for k, gs in enumerate(groups):
        h_in = jnp.where(live_t, h_ins[k], hc_ref[ch_t, :, gs])
        hc_ref[ch_t, :, gs] = h_in
        hp_ref[ch_t, :, gs] = h_in[0:1, :]

    w, bias = cw_ref[...], cb_ref[...]
    x_tail = xt_ref[ch_c]
    xcs = []
    for c in range(n_chunks):
        rows = slice(c * chunk, (c + 1) * chunk)
        xc, x_tail = _lru_prompt_conv(x_ref[rows, :].astype(F32), x_tail, sub, w, bias)
        xcs.append(xc)
    xt_ref[ch_c] = jnp.where(live_c, x_tail, xt_ref[ch_c])
    xc_all = jnp.concatenate(xcs, axis=0)
    ga, gx = _lru_gate_dots(xc_all, wa_ref, wx_ref)
    xc_ref[stage_w] = xc_all
    ga_ref[stage_w] = ga
    gx_ref[stage_w] = gx


def _branch_lru(z, lru_params, o_a_s, o_b, w_branch_bf16, ma_col, s, tm, tn):
    conv_w, conv_b, w_a, b_a, w_x, b_x, lam = lru_params
    m = z.shape[0]
    width = conv_w.shape[1]
    blk = w_a.shape[1]
    aw = o_b.shape[1]
    d = w_branch_bf16.shape[1]
    n_p = s // tm
    n_r = m // tm
    n_t = d // tn
    cbw = width // n_t
    assert cbw % blk == 0 and cbw % LANES == 0
    chunk = tm // 8
    assert chunk % SUBLANES == 0
    ma_blk = ma_col // tn
    mb_blk = (ma_col + d) // tn
    row_c = lambda u: jnp.minimum(u // n_t, n_p - 1)
    ch_c = lambda u: u % n_t
    unit_t = lambda u: jnp.maximum(u - 1, 0)
    row_t = lambda u: jnp.minimum(unit_t(u) // n_t, n_p - 1)
    ch_t = lambda u: unit_t(u) % n_t
    unit_d = lambda u: jnp.maximum(u - 1 - n_t, 0)
    row_d = lambda u: unit_d(u) // n_t
    tile_d = lambda u: unit_d(u) % n_t
    gate_w = pl.BlockSpec((cbw // blk, blk, blk), lambda u: (ch_c(u), 0, 0))
    chan_t = lambda rows: pl.BlockSpec((rows, cbw), lambda u: (0, ch_t(u)))
    return pl.pallas_call(
        functools.partial(_branch_lru_kernel, n_p=n_p, n_t=n_t, chunk=chunk),
        out_shape=(jax.ShapeDtypeStruct((m, d), BF16), jax.ShapeDtypeStruct((n_t, 1, cbw), F32)),
        grid=((n_r + 1) * n_t + 1,),
        in_specs=[
            pl.BlockSpec((tm, cbw), lambda u: (row_c(u), ch_c(u))),
            pl.BlockSpec((tm, cbw), lambda u: (row_t(u), n_t + ch_t(u))),
            pl.BlockSpec((CONV_WIDTH, cbw), lambda u: (0, ch_c(u))),
            pl.BlockSpec((1, cbw), lambda u: (0, ch_c(u))),
            gate_w,
            gate_w,
            chan_t(1),
            chan_t(1),
            chan_t(1),
            chan_t(tm),
            pl.BlockSpec((tm, aw), lambda u: (row_d(u), 0)),
            pl.BlockSpec((width + aw, tn), lambda u: (0, tile_d(u))),
            pl.BlockSpec((tm, tn), lambda u: (row_d(u), ma_blk + tile_d(u))),
            pl.BlockSpec((tm, tn), lambda u: (row_d(u), mb_blk + tile_d(u))),
        ],
        out_specs=(pl.BlockSpec((tm, tn), lambda u: (row_d(u), tile_d(u))),
                   pl.BlockSpec((n_t, 1, cbw), lambda u: (0, 0, 0))),
        scratch_shapes=[
            pltpu.VMEM((2, n_t, tm, cbw), BF16),
            pltpu.VMEM((n_t, SUBLANES, cbw), F32),
            pltpu.VMEM((n_t, SUBLANES, cbw), F32),
            pltpu.VMEM((2, tm, cbw), F32),
            pltpu.VMEM((2, tm, cbw), F32),
            pltpu.VMEM((2, tm, cbw), F32),
        ],
        compiler_params=_params(("arbitrary",)),
        name="branch_lru",
    )(z, z, conv_w, conv_b, w_a, w_x, b_a, b_x, lam, o_a_s, o_b, w_branch_bf16, z, z)


def _out_kernel(m_ref, w_ref, g_ref, x_ref, y_ref, acc_ref, ss_ref, *, d):
    i = pl.program_id(0)
    j = pl.program_id(1)
    slot = lax.rem(i, 2)
    prev = 1 - slot

    @pl.when((i == 0) & (j == 0))
    def _():
        acc_ref[1] = jnp.zeros(acc_ref.shape[1:], F32)
        ss_ref[...] = jnp.zeros_like(ss_ref)

    t = _dot(m_ref[...], w_ref[...])
    acc_ref[slot, j] = t
    ssq = jnp.sum(t * t, axis=-1, keepdims=True)
    ss_ref[slot] = jnp.where(j == 0, ssq, ss_ref[slot] + ssq)

    inv = lax.rsqrt(ss_ref[prev] / d + RMS_EPS)
    y_ref[...] = x_ref[...] + (acc_ref[prev, j] * inv) * g_ref[...]


def _out_proj(merged, w_out_bf16, g, x, row0, tm, tn):
    rows, d = x.shape
    n_r = rows // tm
    n_t = d // tn
    blk0 = row0 // tm
    xy_map = lambda i, j: (jnp.maximum(i - 1, 0), jnp.where(i == 0, 0, j))
    return pl.pallas_call(
        functools.partial(_out_kernel, d=d),
        out_shape=jax.ShapeDtypeStruct((rows, d), F32),
        grid=(n_r + 1, n_t),
        in_specs=[
            pl.BlockSpec((tm, d), lambda i, j: (blk0 + jnp.minimum(i, n_r - 1), 0)),
            pl.BlockSpec((d, tn), lambda i, j: (0, jnp.where(i == n_r, n_t - 1, j))),
            pl.BlockSpec((1, tn), lambda i, j: (0, j)),
            pl.BlockSpec((tm, tn), xy_map),
        ],
        out_specs=pl.BlockSpec((tm, tn), xy_map),
        scratch_shapes=[pltpu.VMEM((2, n_t, tm, tn), F32), pltpu.VMEM((2, tm, 1), F32)],
        compiler_params=_params(("arbitrary", "arbitrary")),
        name="out_proj",
    )(merged, w_out_bf16, g, x)


def _pad_rows(x, rows):
    return jnp.pad(x, ((0, 0), (0, rows - x.shape[1]), (0, 0)))


def _layer(xp, xs_tm, conv_state, h_state, cache_k, cache_v, norm_pre, norm_post, w_in, conv_w,
           conv_b, w_a, b_a, w_x, b_x, lam, sinks, w_branch, w_out):
    s, d = xp.shape
    nb, wb, n_kv, _ = cache_k.shape
    ts = xs_tm.shape[0] // nb
    width = conv_w.shape[1]
    n_heads = sinks.shape[0]
    group = n_heads // n_kv
    aw, kvw = n_heads * HEAD_DIM, n_kv * HEAD_DIM
    m = s + nb * ts
    tm = nb * ts
    assert s % tm == 0 and tm % SUBLANES == 0 and ts >= CONV_WIDTH - 1
    q_col = 2 * width
    k_col = q_col + aw
    v_col = k_col + kvw
    g_col = v_col + kvw
    ma_col = g_col + aw
    assert w_in.shape[1] == ma_col + 2 * d

    xn = _rmsnorm_pre(xp, xs_tm, norm_pre.reshape(1, d), _largest_divisor(tm, (512, 256, 128, 64, 8)))
    z, (w_branch_bf16, w_out_bf16) = _in_proj(xn, w_in, (w_branch, w_out))

    tn = _largest_divisor(ma_col, (512, 256, 128))
    assert d % tn == 0 and width % (d // tn) == 0
    lru_params = (conv_w, conv_b.reshape(1, width), w_a, b_a.reshape(1, width),
                  w_x, b_x.reshape(1, width), lam.reshape(1, width))
    o_a_s, h_s = _lru_sample(z, lru_params, conv_state.transpose(1, 0, 2), h_state, s, tm,
                             width // (d // tn))

    def sample_cols(c0, c1):
        blk = lax.slice(z, (s, c0), (m, c1)).reshape(ts, nb, c1 - c0)
        return blk.transpose(1, 0, 2).astype(F32)

    q_s, k_s, v_s = sample_cols(q_col, k_col), sample_cols(k_col, v_col), sample_cols(v_col, g_col)
    g_s = sample_cols(g_col, ma_col)
    tq = -(-ts // SUBLANES) * SUBLANES
    tk = -(-ts // BF16_ROWS) * BF16_ROWS
    sink_rows = jnp.broadcast_to(sinks.reshape(n_heads, 1, 1), (n_heads, tq, LANES))
    sink_rows = sink_rows.reshape(n_heads * tq, LANES)
    heads_t = lambda c: c.transpose(0, 2, 3, 1)
    assert ts <= wb
    o_b_s, new_k_t, new_v_t = _attn_sample(
        _pad_rows(q_s, tq), _pad_rows(k_s, tk), _pad_rows(v_s, tk), heads_t(cache_k),
        heads_t(cache_v), _pad_rows(g_s, tq), sink_rows, n_kv, group, ts)
    new_k_s, new_v_s = new_k_t.transpose(0, 3, 1, 2), new_v_t.transpose(0, 3, 1, 2)
    o_b_s = o_b_s[:, :ts].transpose(1, 0, 2).reshape(nb * ts, aw).astype(BF16)
    o_b = _attn_prompt(z, sinks, o_b_s, s, width, aw, kvw, n_kv, group)

    merged, h_p = _branch_lru(z, lru_params, o_a_s, o_b, w_branch_bf16, ma_col, s, tm, tn)
    h_p = h_p.reshape(1, width)
    g_post = norm_post.reshape(1, d)
    tn_out = _largest_divisor(d, (1024, 512, 256, 128))
    y_p = _out_proj(merged, w_out_bf16, g_post, xp, 0, tm, tn_out)
    y_s = _out_proj(merged, w_out_bf16, g_post, xs_tm, s, tm // 2,
                    _largest_divisor(d, (2048, 1024, 512, 256, 128)))

    keep = CONV_WIDTH - 1
    new_conv_p = lax.slice(z, (s - keep, 0), (s, width)).astype(F32)[None]
    wbp = min(WINDOW, s)
    new_k_p = lax.slice(z, (s - wbp, k_col), (s, v_col)).astype(F32).reshape(1, wbp, n_kv, HEAD_DIM)
    new_v_p = lax.slice(z, (s - wbp, v_col), (s, g_col)).astype(F32).reshape(1, wbp, n_kv, HEAD_DIM)
    x_lru_s = sample_cols(0, width)
    new_conv_s = jnp.concatenate([conv_state, x_lru_s], axis=1)[:, -keep:]
    return y_p, y_s, (new_conv_p, h_p, new_k_p, new_v_p), (new_conv_s, h_s, new_k_s, new_v_s)


def kernel(x_prompt, x_sample, state_conv, state_h, cache_k_win, cache_v_win, norm_pre, norm_post, w_in, conv_w, conv_b, lru_w_a, lru_b_a, lru_w_x, lru_b_x, lru_lambda, attn_sinks, w_branch, w_out):
    batch, s, d = x_prompt.shape
    nb, ts, _ = x_sample.shape
    assert batch == 1, "the prompt group is a single sequence"
    depth = w_in.shape[0]
    xp = x_prompt.reshape(s, d)
    xs_tm = x_sample.transpose(1, 0, 2).reshape(ts * nb, d)
    p_states, s_states = [], []
    for l in range(depth):
        xp, xs_tm, p_new, s_new = _layer(
            xp, xs_tm, state_conv[l], state_h[l], cache_k_win[l], cache_v_win[l], norm_pre[l],
            norm_post[l], w_in[l], conv_w[l], conv_b[l], lru_w_a[l], lru_b_a[l], lru_w_x[l],
            lru_b_x[l], lru_lambda[l], attn_sinks[l], w_branch[l], w_out[l])
        p_states.append(p_new)
        s_states.append(s_new)
    y_prompt = xp.reshape(1, s, d)
    y_sample = xs_tm.reshape(ts, nb, d).transpose(1, 0, 2)
    stack = lambda states, k: jnp.stack([st[k] for st in states])
    return (y_prompt, y_sample,
            stack(p_states, 0), stack(p_states, 1), stack(p_states, 2), stack(p_states, 3),
            stack(s_states, 0), stack(s_states, 1), stack(s_states, 2), stack(s_states, 3))
```

```python
import functools

import jax
import jax.numpy as jnp
from jax import lax
from jax.experimental import pallas as pl
from jax.experimental.pallas import tpu as pltpu

HEAD_DIM = 64
WINDOW = 128
CONV_WIDTH = 4
LRU_C = 8.0
RMS_EPS = 1e-6
NEG_INF = -1e30

LANES = 128
SUBLANES = 8
BF16_ROWS = 16
VMEM_LIMIT_BYTES = 58 * 1024 * 1024
IN_PROJ_VMEM_BUDGET = 52 * 1024 * 1024
IN_PROJ_VMEM_LIMIT_BYTES = 63 * 1024 * 1024

F32 = jnp.float32
BF16 = jnp.bfloat16


def _largest_divisor(n, candidates):
    for c in candidates:
        if n % c == 0:
            return c
    raise ValueError(f"no tile in {candidates} divides {n}")


def _params(sem, vmem_limit_bytes=VMEM_LIMIT_BYTES):
    return pltpu.CompilerParams(dimension_semantics=sem, vmem_limit_bytes=vmem_limit_bytes)


def _dot(a, b):
    return jnp.dot(a, b, preferred_element_type=F32)


def _dot_nt(a, b):
    return lax.dot_general(a, b, (((1,), (1,)), ((), ())), preferred_element_type=F32)


def _sigmoid(x):
    return jax.nn.sigmoid(x)


def _silu(x):
    return x * jax.nn.sigmoid(x)


def _rmsnorm_kernel(xp_ref, xs_ref, g_ref, o_ref, *, n_p):
    i = pl.program_id(0)

    def body(x_ref):
        x = x_ref[...]
        ms = jnp.mean(x * x, axis=-1, keepdims=True)
        o_ref[...] = ((x * lax.rsqrt(ms + RMS_EPS)) * g_ref[...]).astype(o_ref.dtype)

    @pl.when(i < n_p)
    def _():
        body(xp_ref)

    @pl.when(i >= n_p)
    def _():
        body(xs_ref)


def _rmsnorm_pre(xp, xs, g, tm):
    s, d = xp.shape
    ms = xs.shape[0]
    n_p, n_s = s // tm, ms // tm
    return pl.pallas_call(
        functools.partial(_rmsnorm_kernel, n_p=n_p),
        out_shape=jax.ShapeDtypeStruct((s + ms, d), BF16),
        grid=(n_p + n_s,),
        in_specs=[
            pl.BlockSpec((tm, d), lambda i: (jnp.minimum(i, n_p - 1), 0)),
            pl.BlockSpec((tm, d), lambda i: (jnp.maximum(i - n_p, 0), 0)),
            pl.BlockSpec((1, d), lambda i: (0, 0)),
        ],
        out_specs=pl.BlockSpec((tm, d), lambda i: (i, 0)),
        compiler_params=_params(("arbitrary",)),
        name="rmsnorm_pre",
    )(xp, xs, g)


def _cast_rows(src_ref, dst_ref, rows_per_step=256):
    n = src_ref.shape[0]
    step = _largest_divisor(n, (rows_per_step, 128, 64, 32, 16))

    def body(r, carry):
        r0 = pl.multiple_of(r * step, step)
        dst_ref[pl.ds(r0, step), :] = src_ref[pl.ds(r0, step), :].astype(dst_ref.dtype)
        return carry

    lax.fori_loop(0, n // step, body, 0)


def _in_proj_kernel(x_ref, w_ref, *rest):
    n_extra = (len(rest) - 2) // 2
    extra_in, o_ref = rest[:n_extra], rest[n_extra]
    extra_out, wb_ref = rest[n_extra + 1:2 * n_extra + 1], rest[-1]

    @pl.when(pl.program_id(1) == 0)
    def _():
        _cast_rows(w_ref, wb_ref)

    o_ref[...] = _dot(x_ref[...], wb_ref[...]).astype(o_ref.dtype)
    for src, dst in zip(extra_in, extra_out):
        dst[...] = src[...].astype(dst.dtype)


def _in_proj_tiles(m, d, n):
    best = None
    for tm in range(BF16_ROWS, m + 1, BF16_ROWS):
        if m % tm:
            continue
        for tn in range(2 * LANES, n + 1, 2 * LANES):
            if n % tn:
                continue
            vmem = d * tn * (2 * 4 + 2) + 2 * tm * d * 2 + 2 * tm * tn * 2
            if vmem <= IN_PROJ_VMEM_BUDGET and (best is None or tm * tn > best[0] * best[1]):
                best = (tm, tn)
    assert best is not None
    return best


def _in_proj(xn, w_in, extra_weights=()):
    m, d = xn.shape
    n = w_in.shape[1]
    tm, tn = _in_proj_tiles(m, d, n)
    n_i = m // tm
    n_steps = (n // tn) * n_i
    extra_specs = []
    for w in extra_weights:
        rows = w.shape[0]
        slab = next(r for r in range(BF16_ROWS, rows + 1, BF16_ROWS)
                    if rows % r == 0 and rows // r <= n_steps)
        last = rows // slab - 1
        extra_specs.append(pl.BlockSpec(
            (slab, w.shape[1]), lambda j, i, last=last: (jnp.minimum(j * n_i + i, last), 0)))
    out = pl.pallas_call(
        _in_proj_kernel,
        out_shape=(jax.ShapeDtypeStruct((m, n), BF16),
                   *[jax.ShapeDtypeStruct(w.shape, BF16) for w in extra_weights]),
        grid=(n // tn, n_i),
        in_specs=[
            pl.BlockSpec((tm, d), lambda j, i: (i, 0)),
            pl.BlockSpec((d, tn), lambda j, i: (0, j)),
            *extra_specs,
        ],
        out_specs=(pl.BlockSpec((tm, tn), lambda j, i: (i, j)), *extra_specs),
        scratch_shapes=[pltpu.VMEM((d, tn), BF16)],
        compiler_params=_params(("arbitrary", "arbitrary"), IN_PROJ_VMEM_LIMIT_BYTES),
        name="in_proj",
    )(xn, w_in, *extra_weights)
    return out[0], out[1:]


LOG2_E = 1.4426950408889634


def _neg_c_softplus(lam):
    return (-LRU_C * LOG2_E) * (jnp.maximum(-lam, 0.0) + jnp.log(1.0 + jnp.exp(-jnp.abs(lam))))


def _lru_conv(taps, w, bias):
    out = bias
    for k in range(CONV_WIDTH):
        out = out + taps[k] * w[k:k + 1, :]
    return out


def _lru_gate_dots(xc, wa_ref, wx_ref):
    blk = wa_ref.shape[1]
    xcb = xc.astype(BF16)
    ga, gx = [], []
    for q in range(xc.shape[1] // blk):
        xq = xcb[:, q * blk:(q + 1) * blk]
        ga.append(_dot(xq, wa_ref[q]))
        gx.append(_dot(xq, wx_ref[q]))
    return jnp.concatenate(ga, axis=1), jnp.concatenate(gx, axis=1)


def _lru_coeffs(xc, ga, gx, b_a, b_x, ncs):
    r_a = _sigmoid(ga + b_a)
    r_x = _sigmoid(gx + b_x)
    a = jnp.exp2(ncs * r_a)
    b = jnp.exp2(0.5 * jnp.log2(1.0 - a * a)) * (r_x * xc)
    return a, b


def _scan_tile(a, b, sub):
    shift = 1
    while shift < SUBLANES:
        keep = sub >= shift
        a_sh = jnp.where(keep, pltpu.roll(a, shift, axis=0), 1.0)
        b_sh = jnp.where(keep, pltpu.roll(b, shift, axis=0), 0.0)
        b = b + a * b_sh
        a = a * a_sh
        shift *= 2
    return a, b


def _lru_prompt_conv(xcur, x_tail, sub, w, bias):
    n_tiles = xcur.shape[0] // SUBLANES
    tiles = [x_tail] + [xcur[v * SUBLANES:(v + 1) * SUBLANES] for v in range(n_tiles)]
    taps = []
    for k in range(CONV_WIDTH):
        back = CONV_WIDTH - 1 - k
        if back == 0:
            taps.append(xcur)
            continue
        rolled = [pltpu.roll(t, back, axis=0) for t in tiles]
        taps.append(jnp.concatenate(
            [jnp.where(sub >= back, rolled[v + 1], rolled[v]) for v in range(n_tiles)], axis=0))
    return _lru_conv(taps, w, bias), tiles[-1]


def _lru_prompt_scan(a, b, h_in, sub):
    hs = []
    for v in range(a.shape[0] // SUBLANES):
        rows = slice(v * SUBLANES, (v + 1) * SUBLANES)
        a_in, b_in = _scan_tile(a[rows], b[rows], sub)
        hv = a_in * h_in + b_in
        hs.append(hv)
        h_in = jnp.broadcast_to(hv[SUBLANES - 1:SUBLANES, :], hv.shape)
    return jnp.concatenate(hs, axis=0), h_in


def _lru_sample_kernel(x_ref, g_ref, cw_ref, cb_ref, wa_ref, ba_ref, wx_ref, bx_ref, lam_ref,
                       cst_ref, h0_ref, o_ref, hs_ref):
    nb = h0_ref.shape[0]
    ts = x_ref.shape[0] // nb
    w, bias = cw_ref[...], cb_ref[...]
    ncs = _neg_c_softplus(lam_ref[...])
    xs = [cst_ref[k] for k in range(CONV_WIDTH - 1)]
    xs += [x_ref[t * nb:(t + 1) * nb, :].astype(F32) for t in range(ts)]
    h = h0_ref[...]
    for t in range(ts):
        xc = _lru_conv(xs[t:t + CONV_WIDTH], w, bias)
        ga, gx = _lru_gate_dots(xc, wa_ref, wx_ref)
        a, b = _lru_coeffs(xc, ga, gx, ba_ref[...], bx_ref[...], ncs)
        h = a * h + b
        g = g_ref[t * nb:(t + 1) * nb, :].astype(F32)
        o_ref[t * nb:(t + 1) * nb, :] = (h * _silu(g)).astype(o_ref.dtype)
    hs_ref[...] = h


def _lru_sample(z, lru_params, cst_tm, h0, s, tb, cbw):
    conv_w, conv_b, w_a, b_a, w_x, b_x, lam = lru_params
    width = conv_w.shape[1]
    nb = h0.shape[0]
    blk = w_a.shape[1]
    n_c = width // cbw
    blk_s = s // tb
    row = lambda c: (0, c)
    gate_w = pl.BlockSpec((cbw // blk, blk, blk), lambda c: (c, 0, 0))
    return pl.pallas_call(
        _lru_sample_kernel,
        out_shape=(jax.ShapeDtypeStruct((tb, width), BF16), jax.ShapeDtypeStruct((nb, width), F32)),
        grid=(n_c,),
        in_specs=[
            pl.BlockSpec((tb, cbw), lambda c: (blk_s, c)),
            pl.BlockSpec((tb, cbw), lambda c: (blk_s, n_c + c)),
            pl.BlockSpec((CONV_WIDTH, cbw), row),
            pl.BlockSpec((1, cbw), row),
            gate_w,
            pl.BlockSpec((1, cbw), row),
            gate_w,
            pl.BlockSpec((1, cbw), row),
            pl.BlockSpec((1, cbw), row),
            pl.BlockSpec((CONV_WIDTH - 1, nb, cbw), lambda c: (0, 0, c)),
            pl.BlockSpec((nb, cbw), row),
        ],
        out_specs=(pl.BlockSpec((tb, cbw), row), pl.BlockSpec((nb, cbw), row)),
        compiler_params=_params(("arbitrary",)),
        name="lru_sample",
    )(z, z, conv_w, conv_b, w_a, b_a, w_x, b_x, lam, cst_tm, h0)


def _attn_prompt_kernel(sink_ref, q_ref, kp_ref, kc_ref, vp_ref, vc_ref, g0_ref, g1_ref, os_ref,
                        o_ref, *, n_q, n_kv, group):
    i = pl.program_id(0)

    @pl.when(i < n_q)
    def _():
        _attn_prompt_block(i, sink_ref, q_ref, kp_ref, kc_ref, vp_ref, vc_ref, g0_ref, g1_ref,
                           o_ref, n_kv, group)

    @pl.when(i >= n_q)
    def _():
        o_ref[...] = os_ref[...]


def _attn_prompt_block(i, sink_ref, q_ref, kp_ref, kc_ref, vp_ref, vc_ref, g0_ref, g1_ref, o_ref,
                       n_kv, group):
    qb = q_ref.shape[0]
    aw_half = g0_ref.shape[1]
    scale = HEAD_DIM ** -0.5
    hw = group * HEAD_DIM
    c = lax.broadcasted_iota(jnp.int32, (2 * qb, qb), 0)
    r = lax.broadcasted_iota(jnp.int32, (2 * qb, qb), 1)
    rel = qb + r - c
    mask = (rel >= 0) & (rel <= WINDOW) & ((c >= qb) | (i > 0))
    lane = lax.broadcasted_iota(jnp.int32, (qb, LANES), 1)
    half_mask = (lane < HEAD_DIM, lane >= HEAD_DIM)
    zero = jnp.zeros((qb, LANES), BF16)
    n_pairs = n_kv // 2
    kv = {}
    out_t = {}

    def heads_of(p, variant):
        return [(2 * p + half, g) for half in range(2) for g in range(group)
                if ((g % 2) == half) == (variant == 0)]

    def load_pair(p):
        lanes = slice(p * LANES, (p + 1) * LANES)
        k_f32 = jnp.concatenate([kp_ref[:, lanes], kc_ref[:, lanes]], axis=0).astype(F32) * scale
        v_pair = jnp.concatenate([vp_ref[:, lanes], vc_ref[:, lanes]], axis=0)
        kv[p] = (k_f32.astype(BF16), pltpu.roll(k_f32, HEAD_DIM, axis=1).astype(BF16),
                 v_pair.astype(F32).T.astype(BF16))

    def scores(p, variant):
        q_rows = []
        for kh, g in heads_of(p, variant):
            c0 = kh * hw + (g // 2) * LANES
            q_rows.append(jnp.where(half_mask[g % 2], q_ref[:, c0:c0 + LANES], zero))
        return _dot_nt(kv[p][variant], jnp.concatenate(q_rows, axis=0))

    def softmax_values(p, variant, s_t):
        heads = heads_of(p, variant)
        p_blocks, inv_den = [], []
        for b, (kh, g) in enumerate(heads):
            sink = sink_ref[kh * group + g]
            sb = jnp.where(mask, s_t[:, b * qb:(b + 1) * qb], NEG_INF)
            m = jnp.maximum(jnp.max(sb, axis=0, keepdims=True), sink)
            pe = jnp.exp(sb - m)
            den = jnp.sum(pe, axis=0, keepdims=True) + jnp.exp(sink - m)
            p_blocks.append(pe.astype(BF16))
            inv_den.append(1.0 / den)
        o_t = _dot(kv[p][2], jnp.concatenate(p_blocks, axis=1))
        for b, (kh, g) in enumerate(heads):
            r0 = (kh % 2) * HEAD_DIM
            out_t[(kh, g)] = o_t[r0:r0 + HEAD_DIM, b * qb:(b + 1) * qb] * inv_den[b]

    def store_pair(p):
        for half in range(2):
            kh = 2 * p + half
            for j in range(group // 2):
                c0 = kh * hw + j * LANES
                tile = jnp.concatenate([out_t[(kh, 2 * j)], out_t[(kh, 2 * j + 1)]], axis=0).T
                g_ref, gc = (g0_ref, c0) if c0 < aw_half else (g1_ref, c0 - aw_half)
                gate = g_ref[:, gc:gc + LANES].astype(F32)
                o_ref[:, c0:c0 + LANES] = (tile * _silu(gate)).astype(o_ref.dtype)

    groups = [(p, variant) for p in range(n_pairs) for variant in range(2)]
    load_pair(0)
    s_next = scores(*groups[0])
    for k, (p, variant) in enumerate(groups):
        s_cur = s_next
        if k + 1 < len(groups):
            if groups[k + 1][0] != p:
                load_pair(groups[k + 1][0])
            s_next = scores(*groups[k + 1])
        softmax_values(p, variant, s_cur)
        if variant == 1:
            store_pair(p)


def _attn_prompt(z, sinks, o_b_s, s, width, aw, kvw, n_kv, group):
    qb = WINDOW
    n_q = s // qb
    assert s % qb == 0 and o_b_s.shape[0] % qb == 0 and n_kv % 2 == 0 and group % 2 == 0
    q_blk = (2 * width) // aw
    k_blk = (2 * width + aw) // kvw
    v_blk = k_blk + 1
    g_blk = (2 * width + aw + 2 * kvw) // (aw // 2)
    assert (2 * width) % aw == 0 and (2 * width + aw) % kvw == 0
    assert (2 * width + aw + 2 * kvw) % (aw // 2) == 0
    cur = lambda i: jnp.minimum(i, n_q - 1)
    prev = lambda i: jnp.maximum(cur(i) - 1, 0)
    return pl.pallas_call(
        functools.partial(_attn_prompt_kernel, n_q=n_q, n_kv=n_kv, group=group),
        out_shape=jax.ShapeDtypeStruct((s + o_b_s.shape[0], aw), BF16),
        grid=(n_q + o_b_s.shape[0] // qb,),
        in_specs=[
            pl.BlockSpec(memory_space=pltpu.SMEM),
            pl.BlockSpec((qb, aw), lambda i: (cur(i), q_blk)),
            pl.BlockSpec((qb, kvw), lambda i: (prev(i), k_blk)),
            pl.BlockSpec((qb, kvw), lambda i: (cur(i), k_blk)),
            pl.BlockSpec((qb, kvw), lambda i: (prev(i), v_blk)),
            pl.BlockSpec((qb, kvw), lambda i: (cur(i), v_blk)),
            pl.BlockSpec((qb, aw // 2), lambda i: (cur(i), g_blk)),
            pl.BlockSpec((qb, aw // 2), lambda i: (cur(i), g_blk + 1)),
            pl.BlockSpec((qb, aw), lambda i: (jnp.maximum(i - n_q, 0), 0)),
        ],
        out_specs=pl.BlockSpec((qb, aw), lambda i: (i, 0)),
        compiler_params=_params(("arbitrary",)),
        name="attn_prompt",
    )(sinks, z, z, z, z, z, z, z, o_b_s)


def _attn_sample_kernel(q_ref, kn_ref, vn_ref, ck_ref, cv_ref, g_ref, sink_ref, o_ref,
                        nk_ref, nv_ref, s_ref, p_ref, *, n_kv, group, ts):
    sb, tq, _ = q_ref.shape
    wb = ck_ref.shape[3]
    tk = kn_ref.shape[1]
    n_keys = s_ref.shape[1]
    n_new = n_keys - wb
    n_pairs = n_kv // 2
    pair_rows = 2 * group * tq
    seq_rows = n_pairs * pair_rows
    scale = HEAD_DIM ** -0.5
    hw = group * HEAD_DIM
    lane = lax.broadcasted_iota(jnp.int32, (tq, LANES), 1)
    new_pad = jnp.zeros((n_new - tk, LANES), F32)

    pos = lax.broadcasted_iota(jnp.int32, (2 * HEAD_DIM, wb), 1)
    t_new = lax.broadcasted_iota(jnp.int32, (tk, wb), 0)
    place = ((lax.broadcasted_iota(jnp.int32, (tk, wb), 1) == t_new + (wb - ts))
             & (t_new < ts)).astype(BF16)

    def cached_t(c_ref, n_ref, nxt_ref, n, p):
        old = jnp.concatenate([c_ref[n, 2 * p], c_ref[n, 2 * p + 1]], axis=0)
        rows = n_ref[n, :, p * LANES:(p + 1) * LANES].astype(BF16)
        placed = lax.dot_general(rows, place, (((0,), (0,)), ((), ())),
                                 preferred_element_type=F32)
        nxt = jnp.where(pos >= wb - ts, placed, pltpu.roll(old, wb - ts, axis=1))
        nxt_ref[n, 2 * p] = nxt[:HEAD_DIM]
        nxt_ref[n, 2 * p + 1] = nxt[HEAD_DIM:]
        return old.astype(BF16)

    def new_rows(n_ref, n, p):
        lanes = slice(p * LANES, (p + 1) * LANES)
        return jnp.concatenate([n_ref[n, :, lanes], new_pad], axis=0).astype(BF16)

    def score_body(n, carry):
        for p in range(n_pairs):
            pieces = []
            for half in range(2):
                for g in range(group):
                    c0 = (2 * p + half) * hw + (g // 2) * LANES
                    tile = q_ref[n, :, c0:c0 + LANES] * scale
                    if (g % 2) != half:
                        tile = pltpu.roll(tile, HEAD_DIM, axis=1)
                    keep = (lane >= HEAD_DIM * half) & (lane < HEAD_DIM * (half + 1))
                    pieces.append(jnp.where(keep, tile, 0.0))
            lhs = jnp.concatenate(pieces, axis=0).astype(BF16)
            r0 = pl.multiple_of(n * seq_rows + p * pair_rows, pair_rows)
            s_ref[pl.ds(r0, pair_rows), 0:wb] = _dot(lhs, cached_t(ck_ref, kn_ref, nk_ref, n, p))
            s_ref[pl.ds(r0, pair_rows), wb:] = _dot_nt(lhs, new_rows(kn_ref, n, p))
        return carry

    lax.fori_loop(0, sb, score_body, 0)

    t = lax.broadcasted_iota(jnp.int32, (tq, n_keys), 0)
    c = lax.broadcasted_iota(jnp.int32, (tq, n_keys), 1)
    mask8 = ((c < wb) & (t + wb - c <= WINDOW)) | ((c >= wb) & (c - wb <= t) & (c - wb < ts))
    reps = sb * seq_rows // tq
    mask = jnp.concatenate([mask8] * reps, axis=0)
    sink = jnp.concatenate([sink_ref[:, 0:1]] * sb, axis=0)
    s = jnp.where(mask, s_ref[...], NEG_INF)
    m = jnp.maximum(jnp.max(s, axis=-1, keepdims=True), sink)
    pe = jnp.exp(s - m)
    den = jnp.sum(pe, axis=-1, keepdims=True) + jnp.exp(sink - m)
    p_ref[...] = (pe / den).astype(p_ref.dtype)

    def value_body(n, carry):
        for p in range(n_pairs):
            r0 = pl.multiple_of(n * seq_rows + p * pair_rows, pair_rows)
            o = (_dot_nt(p_ref[pl.ds(r0, pair_rows), 0:wb], cached_t(cv_ref, vn_ref, nv_ref, n, p))
                 + _dot(p_ref[pl.ds(r0, pair_rows), wb:], new_rows(vn_ref, n, p)))
            for half in range(2):
                kh = 2 * p + half
                for j in range(group // 2):
                    rows = (half * group + 2 * j) * tq
                    o_lo, o_hi = o[rows:rows + tq], o[rows + tq:rows + 2 * tq]
                    if half == 0:
                        o_hi = pltpu.roll(o_hi, HEAD_DIM, axis=1)
                    else:
                        o_lo = pltpu.roll(o_lo, HEAD_DIM, axis=1)
                    c0 = kh * hw + j * LANES
                    gate = g_ref[n, :, c0:c0 + LANES]
                    o_ref[n, :, c0:c0 + LANES] = jnp.where(lane < HEAD_DIM, o_lo, o_hi) * _silu(gate)
        return carry

    lax.fori_loop(0, sb, value_body, 0)


def _attn_sample(q8, kn, vn, cache_k_t, cache_v_t, g8, sink_rows, n_kv, group, ts):
    nb, tq, aw = q8.shape
    tk, kvw = kn.shape[1], kn.shape[2]
    wb = cache_k_t.shape[3]
    sb = _largest_divisor(nb, (8, 4, 2, 1))
    assert wb % LANES == 0
    n_keys = wb + -(-tk // LANES) * LANES
    rows = sb * (n_kv // 2) * 2 * group * tq
    assert sink_rows.shape[0] * sb == rows
    blk3 = lambda r, width: pl.BlockSpec((sb, r, width), lambda i: (i, 0, 0))
    cache_blk = pl.BlockSpec((sb, n_kv, HEAD_DIM, wb), lambda i: (i, 0, 0, 0))
    return pl.pallas_call(
        functools.partial(_attn_sample_kernel, n_kv=n_kv, group=group, ts=ts),
        out_shape=(jax.ShapeDtypeStruct((nb, tq, aw), F32),
                   jax.ShapeDtypeStruct(cache_k_t.shape, cache_k_t.dtype),
                   jax.ShapeDtypeStruct(cache_v_t.shape, cache_v_t.dtype)),
        grid=(nb // sb,),
        in_specs=[
            blk3(tq, aw), blk3(tk, kvw), blk3(tk, kvw), cache_blk, cache_blk, blk3(tq, aw),
            pl.BlockSpec(sink_rows.shape, lambda i: (0, 0)),
        ],
        out_specs=(blk3(tq, aw), cache_blk, cache_blk),
        scratch_shapes=[pltpu.VMEM((rows, n_keys), F32), pltpu.VMEM((rows, n_keys), BF16)],
        compiler_params=_params(("arbitrary",)),
        name="attn_sample",
    )(q8, kn, vn, cache_k_t, cache_v_t, g8, sink_rows)


def _branch_lru_kernel(x_ref, g_ref, cw_ref, cb_ref, wa_ref, wx_ref, ba_ref, bx_ref, lam_ref,
                       oas_ref, ob_ref, w_ref, ma_ref, mb_ref,
                       o_ref, hp_ref, oa_ref, xt_ref, hc_ref, xc_ref, ga_ref, gx_ref,
                       *, n_p, n_t, chunk):
    u = pl.program_id(0)
    tm, cbw = x_ref.shape

    @pl.when(u == 0)
    def _():
        for ref in (oa_ref, xt_ref, hc_ref, xc_ref, ga_ref, gx_ref):
            ref[...] = jnp.zeros_like(ref)

    unit_t = jnp.maximum(u - 1, 0)
    row_t, ch_t = lax.div(unit_t, n_t), lax.rem(unit_t, n_t)
    live_t = (u >= 1) & (unit_t < n_p * n_t)
    row_d = lax.div(jnp.maximum(u - 1 - n_t, 0), n_t)
    ch_c = lax.rem(u, n_t)
    live_c = u < n_p * n_t

    slot_d = lax.rem(row_d, 2)
    width = n_t * cbw
    tn = o_ref.shape[1]
    n_chunks = tm // chunk
    pieces = [(rh, ch) for rh in range(2) for ch in range(2)]
    chunks_per_piece = n_chunks // len(pieces)
    assert chunks_per_piece * len(pieces) == n_chunks

    def product_piece(rh, ch):
        rows = slice(rh * tm // 2, (rh + 1) * tm // 2)
        cols = slice(ch * tn // 2, (ch + 1) * tn // 2)
        o_a = jnp.concatenate([oa_ref[slot_d, q, rows, :] for q in range(n_t)], axis=1)
        pa = _dot(o_a, w_ref[0:width, cols])
        pb = _dot(ob_ref[rows, :], w_ref[width:, cols])
        ma = ma_ref[rows, cols].astype(F32)
        mb = mb_ref[rows, cols].astype(F32)
        o_ref[rows, cols] = (_sigmoid(ma) * pa + _sigmoid(mb) * pb).astype(o_ref.dtype)

    sub = lax.broadcasted_iota(jnp.int32, (SUBLANES, cbw), 0)
    stage_w = lax.rem(u, 2)
    stage_r = 1 - stage_w

    slot_t = lax.rem(row_t, 2)
    cg = min(cbw, 2 * LANES)
    groups = [slice(k * cg, (k + 1) * cg) for k in range(cbw // cg)]
    sub_g = lax.broadcasted_iota(jnp.int32, (SUBLANES, cg), 0)
    coef = [(ba_ref[:, gs], bx_ref[:, gs], _neg_c_softplus(lam_ref[:, gs])) for gs in groups]
    h_ins = [hc_ref[ch_t, :, gs] for gs in groups]
    for c in range(n_chunks):
        if c % chunks_per_piece == 0:
            product_piece(*pieces[c // chunks_per_piece])
        rows = slice(c * chunk, (c + 1) * chunk)
        for k, gs in enumerate(groups):
            a, b = _lru_coeffs(xc_ref[stage_r, rows, gs], ga_ref[stage_r, rows, gs],
                               gx_ref[stage_r, rows, gs], *coef[k])
            h, h_ins[k] = _lru_prompt_scan(a, b, h_ins[k], sub_g)
            o_a_new = (h * _silu(g_ref[rows, gs].astype(F32))).astype(oa_ref.dtype)
            oa_ref[slot_t, ch_t, rows, gs] = jnp.where(live_t, o_a_new, oas_ref[rows, gs])
    for k, gs in enumerate(groups):
        h_in = jnp.where(live_t, h_ins[k], hc_ref[ch_t, :, gs])
        hc_ref[ch_t, :, gs] = h_in
        hp_ref[ch_t, :, gs] = h_in[0:1, :]

    w, bias = cw_ref[...], cb_ref[...]
    x_tail = xt_ref[ch_c]
    xcs = []
    for c in range(n_chunks):
        rows = slice(c * chunk, (c + 1) * chunk)
        xc, x_tail = _lru_prompt_conv(x_ref[rows, :].astype(F32), x_tail, sub, w, bias)
        xcs.append(xc)
    xt_ref[ch_c] = jnp.where(live_c, x_tail, xt_ref[ch_c])
    xc_all = jnp.concatenate(xcs, axis=0)
    ga, gx = _lru_gate_dots(xc_all, wa_ref, wx_ref)
    xc_ref[stage_w] = xc_all
    ga_ref[stage_w] = ga
    gx_ref[stage_w] = gx


def _branch_lru(z, lru_params, o_a_s, o_b, w_branch_bf16, ma_col, s, tm, tn):
    conv_w, conv_b, w_a, b_a, w_x, b_x, lam = lru_params
    m = z.shape[0]
    width = conv_w.shape[1]
    blk = w_a.shape[1]
    aw = o_b.shape[1]
    d = w_branch_bf16.shape[1]
    n_p = s // tm
    n_r = m // tm
    n_t = d // tn
    cbw = width // n_t
    assert cbw % blk == 0 and cbw % LANES == 0
    chunk = tm // 8
    assert chunk % SUBLANES == 0
    ma_blk = ma_col // tn
    mb_blk = (ma_col + d) // tn
    row_c = lambda u: jnp.minimum(u // n_t, n_p - 1)
    ch_c = lambda u: u % n_t
    unit_t = lambda u: jnp.maximum(u - 1, 0)
    row_t = lambda u: jnp.minimum(unit_t(u) // n_t, n_p - 1)
    ch_t = lambda u: unit_t(u) % n_t
    unit_d = lambda u: jnp.maximum(u - 1 - n_t, 0)
    row_d = lambda u: unit_d(u) // n_t
    tile_d = lambda u: unit_d(u) % n_t
    gate_w = pl.BlockSpec((cbw // blk, blk, blk), lambda u: (ch_c(u), 0, 0))
    chan_t = lambda rows: pl.BlockSpec((rows, cbw), lambda u: (0, ch_t(u)))
    return pl.pallas_call(
        functools.partial(_branch_lru_kernel, n_p=n_p, n_t=n_t, chunk=chunk),
        out_shape=(jax.ShapeDtypeStruct((m, d), BF16), jax.ShapeDtypeStruct((n_t, 1, cbw), F32)),
        grid=((n_r + 1) * n_t + 1,),
        in_specs=[
            pl.BlockSpec((tm, cbw), lambda u: (row_c(u), ch_c(u))),
            pl.BlockSpec((tm, cbw), lambda u: (row_t(u), n_t + ch_t(u))),
            pl.BlockSpec((CONV_WIDTH, cbw), lambda u: (0, ch_c(u))),
            pl.BlockSpec((1, cbw), lambda u: (0, ch_c(u))),
            gate_w,
            gate_w,
            chan_t(1),
            chan_t(1),
            chan_t(1),
            chan_t(tm),
            pl.BlockSpec((tm, aw), lambda u: (row_d(u), 0)),
            pl.BlockSpec((width + aw, tn), lambda u: (0, tile_d(u))),
            pl.BlockSpec((tm, tn), lambda u: (row_d(u), ma_blk + tile_d(u))),
            pl.BlockSpec((tm, tn), lambda u: (row_d(u), mb_blk + tile_d(u))),
        ],
        out_specs=(pl.BlockSpec((tm, tn), lambda u: (row_d(u), tile_d(u))),
                   pl.BlockSpec((n_t, 1, cbw), lambda u: (0, 0, 0))),
        scratch_shapes=[
            pltpu.VMEM((2, n_t, tm, cbw), BF16),
            pltpu.VMEM((n_t, SUBLANES, cbw), F32),
            pltpu.VMEM((n_t, SUBLANES, cbw), F32),
            pltpu.VMEM((2, tm, cbw), F32),
            pltpu.VMEM((2, tm, cbw), F32),
            pltpu.VMEM((2, tm, cbw), F32),
        ],
        compiler_params=_params(("arbitrary",)),
        name="branch_lru",
    )(z, z, conv_w, conv_b, w_a, w_x, b_a, b_x, lam, o_a_s, o_b, w_branch_bf16, z, z)


def _out_kernel(m_ref, w_ref, g_ref, x_ref, y_ref, acc_ref, ss_ref, *, d, n_r):
    i = pl.program_id(0)
    j = pl.program_id(1)
    slot = lax.rem(i, 2)
    prev = 1 - slot

    @pl.when(i < n_r)
    def _():
        t = _dot(m_ref[...], w_ref[...])
        acc_ref[slot, j] = t
        ssq = jnp.sum(t * t, axis=-1, keepdims=True)

        @pl.when(j == 0)
        def _():
            ss_ref[slot] = ssq

        @pl.when(j > 0)
        def _():
            ss_ref[slot] += ssq

    @pl.when(i >= 1)
    def _():
        inv = lax.rsqrt(ss_ref[prev] / d + RMS_EPS)
        y_ref[...] = x_ref[...] + (acc_ref[prev, j] * inv) * g_ref[...]


def _out_proj(merged, w_out_bf16, g, x, row0, tm, tn):
    rows, d = x.shape
    n_r = rows // tm
    n_t = d // tn
    blk0 = row0 // tm
    xy_map = lambda i, j: (jnp.maximum(i - 1, 0), jnp.where(i == 0, 0, j))
    return pl.pallas_call(
        functools.partial(_out_kernel, d=d, n_r=n_r),
        out_shape=jax.ShapeDtypeStruct((rows, d), F32),
        grid=(n_r + 1, n_t),
        in_specs=[
            pl.BlockSpec((tm, d), lambda i, j: (blk0 + jnp.minimum(i, n_r - 1), 0)),
            pl.BlockSpec((d, tn), lambda i, j: (0, jnp.where(i == n_r, n_t - 1, j))),
            pl.BlockSpec((1, tn), lambda i, j: (0, j)),
            pl.BlockSpec((tm, tn), xy_map),
        ],
        out_specs=pl.BlockSpec((tm, tn), xy_map),
        scratch_shapes=[pltpu.VMEM((2, n_t, tm, tn), F32), pltpu.VMEM((2, tm, 1), F32)],
        compiler_params=_params(("arbitrary", "arbitrary")),
        name="out_proj",
    )(merged, w_out_bf16, g, x)


def _pad_rows(x, rows):
    return jnp.pad(x, ((0, 0), (0, rows - x.shape[1]), (0, 0)))


def _layer(xp, xs_tm, conv_state, h_state, cache_k, cache_v, norm_pre, norm_post, w_in, conv_w,
           conv_b, w_a, b_a, w_x, b_x, lam, sinks, w_branch, w_out):
    s, d = xp.shape
    nb, wb, n_kv, _ = cache_k.shape
    ts = xs_tm.shape[0] // nb
    width = conv_w.shape[1]
    n_heads = sinks.shape[0]
    group = n_heads // n_kv
    aw, kvw = n_heads * HEAD_DIM, n_kv * HEAD_DIM
    m = s + nb * ts
    tm = nb * ts
    assert s % tm == 0 and tm % SUBLANES == 0 and ts >= CONV_WIDTH - 1
    q_col = 2 * width
    k_col = q_col + aw
    v_col = k_col + kvw
    g_col = v_col + kvw
    ma_col = g_col + aw
    assert w_in.shape[1] == ma_col + 2 * d

    xn = _rmsnorm_pre(xp, xs_tm, norm_pre.reshape(1, d), _largest_divisor(tm, (512, 256, 128, 64, 8)))
    z, (w_branch_bf16, w_out_bf16) = _in_proj(xn, w_in, (w_branch, w_out))

    tn = _largest_divisor(ma_col, (512, 256, 128))
    assert d % tn == 0 and width % (d // tn) == 0
    lru_params = (conv_w, conv_b.reshape(1, width), w_a.astype(BF16), b_a.reshape(1, width),
                  w_x.astype(BF16), b_x.reshape(1, width), lam.reshape(1, width))
    o_a_s, h_s = _lru_sample(z, lru_params, conv_state.transpose(1, 0, 2), h_state, s, tm,
                             width // (d // tn))

    def sample_cols(c0, c1):
        blk = lax.slice(z, (s, c0), (m, c1)).reshape(ts, nb, c1 - c0)
        return blk.transpose(1, 0, 2).astype(F32)

    q_s, k_s, v_s = sample_cols(q_col, k_col), sample_cols(k_col, v_col), sample_cols(v_col, g_col)
    g_s = sample_cols(g_col, ma_col)
    tq = -(-ts // SUBLANES) * SUBLANES
    tk = -(-ts // BF16_ROWS) * BF16_ROWS
    sink_rows = jnp.broadcast_to(sinks.reshape(n_heads, 1, 1), (n_heads, tq, LANES))
    sink_rows = sink_rows.reshape(n_heads * tq, LANES)
    heads_t = lambda c: c.transpose(0, 2, 3, 1)
    assert ts <= wb
    o_b_s, new_k_t, new_v_t = _attn_sample(
        _pad_rows(q_s, tq), _pad_rows(k_s, tk), _pad_rows(v_s, tk), heads_t(cache_k),
        heads_t(cache_v), _pad_rows(g_s, tq), sink_rows, n_kv, group, ts)
    new_k_s, new_v_s = new_k_t.transpose(0, 3, 1, 2), new_v_t.transpose(0, 3, 1, 2)
    o_b_s = o_b_s[:, :ts].transpose(1, 0, 2).reshape(nb * ts, aw).astype(BF16)
    o_b = _attn_prompt(z, sinks, o_b_s, s, width, aw, kvw, n_kv, group)

    merged, h_p = _branch_lru(z, lru_params, o_a_s, o_b, w_branch_bf16, ma_col, s, tm, tn)
    h_p = h_p.reshape(1, width)
    g_post = norm_post.reshape(1, d)
    tn_out = _largest_divisor(d, (1024, 512, 256, 128))
    y_p = _out_proj(merged, w_out_bf16, g_post, xp, 0, tm, tn_out)
    y_s = _out_proj(merged, w_out_bf16, g_post, xs_tm, s, tm // 2, tn_out)

    keep = CONV_WIDTH - 1
    new_conv_p = lax.slice(z, (s - keep, 0), (s, width)).astype(F32)[None]
    wbp = min(WINDOW, s)
    new_k_p = lax.slice(z, (s - wbp, k_col), (s, v_col)).astype(F32).reshape(1, wbp, n_kv, HEAD_DIM)
    new_v_p = lax.slice(z, (s - wbp, v_col), (s, g_col)).astype(F32).reshape(1, wbp, n_kv, HEAD_DIM)
    x_lru_s = sample_cols(0, width)
    new_conv_s = jnp.concatenate([conv_state, x_lru_s], axis=1)[:, -keep:]
    return y_p, y_s, (new_conv_p, h_p, new_k_p, new_v_p), (new_conv_s, h_s, new_k_s, new_v_s)


def kernel(x_prompt, x_sample, state_conv, state_h, cache_k_win, cache_v_win, norm_pre, norm_post, w_in, conv_w, conv_b, lru_w_a, lru_b_a, lru_w_x, lru_b_x, lru_lambda, attn_sinks, w_branch, w_out):
    batch, s, d = x_prompt.shape
    nb, ts, _ = x_sample.shape
    assert batch == 1, "the prompt group is a single sequence"
    depth = w_in.shape[0]
    xp = x_prompt.reshape(s, d)
    xs_tm = x_sample.transpose(1, 0, 2).reshape(ts * nb, d)
    p_states, s_states = [], []
    for l in range(depth):
        xp, xs_tm, p_new, s_new = _layer(
            xp, xs_tm, state_conv[l], state_h[l], cache_k_win[l], cache_v_win[l], norm_pre[l],
            norm_post[l], w_in[l], conv_w[l], conv_b[l], lru_w_a[l], lru_b_a[l], lru_w_x[l],
            lru_b_x[l], lru_lambda[l], attn_sinks[l], w_branch[l], w_out[l])
        p_states.append(p_new)
        s_states.append(s_new)
    y_prompt = xp.reshape(1, s, d)
    y_sample = xs_tm.reshape(ts, nb, d).transpose(1, 0, 2)
    stack = lambda states, k: jnp.stack([st[k] for st in states])
    return (y_prompt, y_sample,
            stack(p_states, 0), stack(p_states, 1), stack(p_states, 2), stack(p_states, 3),
            stack(s_states, 0), stack(s_states, 1), stack(s_states, 2), stack(s_states, 3))
```
